```python
import jax, jax.numpy as jnp
from jax import lax
import numpy as np

D_MODEL = 2048
BATCH = 4
SEQ = 2048
DEPTH = 2
DEC_BATCH = 32
DEC_SEQ = 32
PAST_LEN = 1024

CHUNK = 64
LEFT_CHUNKS = 8
WINDOW = LEFT_CHUNKS * CHUNK
HEAD_DIM = 128
N_MIX_HEADS = 12
N_MEM_HEADS = 4
MIX_W = N_MIX_HEADS * HEAD_DIM
MEM_W = N_MEM_HEADS * HEAD_DIM
N_MEM = 256
REL_CLIP = 256
HG_DK = 128
HG_DV = 128
PEER_HEADS = 8
N_KEYS = 128
N_EXPERTS = N_KEYS * N_KEYS
PEER_DQ = 256
PEER_HALF = PEER_DQ // 2
PEER_TOPK = 16
PEER_BLOCK = 64
N_A = (DEPTH + 1) // 2
N_B = DEPTH // 2
EPS = 1e-6
NEG_INF = -1e30

kernel_name = "streaming_chunkattn_hgrn2_peer_hybrid_step"


def rmsnorm(x, g):
    xf = x.astype(jnp.float32)
    y = xf * lax.rsqrt(jnp.mean(xf * xf, axis=-1, keepdims=True) + EPS)
    return (y * g.astype(jnp.float32)).astype(x.dtype)


def softmax_attend(q, k, v, bias=None, mask=None):
    s = jnp.einsum('bqhd,bkhd->bhqk', q, k, preferred_element_type=jnp.float32) * (HEAD_DIM ** -0.5)
    if bias is not None:
        s = jnp.where(mask, s + bias, NEG_INF)
    p = jax.nn.softmax(s, axis=-1).astype(v.dtype)
    return jnp.einsum('bhqk,bkhd->bqhd', p, v)


def band_bias_mask(q_pos, k_pos, table):
    rel = jnp.clip(q_pos[:, None] - k_pos[None, :], -REL_CLIP, REL_CLIP) + REL_CLIP
    bias = jnp.transpose(table[rel], (2, 0, 1)).astype(jnp.float32)
    qc = q_pos[:, None] // CHUNK
    kc = k_pos[None, :] // CHUNK
    mask = (k_pos[None, :] >= 0) & (kc <= qc) & (kc >= qc - LEFT_CHUNKS)
    return bias, mask


def band_attention_prompt(q, k, v, table):
    B, S, H, Dh = q.shape
    nc = S // CHUNK
    pad = ((0, 0), (WINDOW, 0), (0, 0), (0, 0))
    kp = jnp.pad(k, pad)
    vp = jnp.pad(v, pad)

    def one_chunk(c):
        qb = lax.dynamic_slice_in_dim(q, c * CHUNK, CHUNK, axis=1)
        kb = lax.dynamic_slice_in_dim(kp, c * CHUNK, WINDOW + CHUNK, axis=1)
        vb = lax.dynamic_slice_in_dim(vp, c * CHUNK, WINDOW + CHUNK, axis=1)
        q_pos = c * CHUNK + jnp.arange(CHUNK)
        k_pos = c * CHUNK - WINDOW + jnp.arange(WINDOW + CHUNK)
        bias, mask = band_bias_mask(q_pos, k_pos, table)
        return softmax_attend(qb, kb, vb, bias, mask)

    out = lax.map(one_chunk, jnp.arange(nc))
    return jnp.moveaxis(out, 0, 1).reshape(B, S, H, Dh)


def band_attention_sample(q, k_new, v_new, k_cache, v_cache, table):
    P = k_cache.shape[1]
    T = q.shape[1]
    k = jnp.concatenate([k_cache.astype(k_new.dtype), k_new], axis=1)
    v = jnp.concatenate([v_cache.astype(v_new.dtype), v_new], axis=1)
    q_pos = PAST_LEN + jnp.arange(T)
    k_pos = jnp.concatenate([PAST_LEN - P + jnp.arange(P), PAST_LEN + jnp.arange(T)])
    bias, mask = band_bias_mask(q_pos, k_pos, table)
    return softmax_attend(q, k, v, bias, mask)


def gla_chunk(s0, q, k, v, log_f):
    T = q.shape[1]
    b = jnp.cumsum(log_f, axis=1)
    causal = jnp.tril(jnp.ones((T, T), dtype=bool))[None, :, :, None, None]
    decay = jnp.exp(jnp.where(causal, b[:, :, None] - b[:, None, :], -jnp.inf))
    scores = jnp.einsum('bthd,bshd,btshd->bhts', q, k, decay)
    o = jnp.einsum('bhts,bshe->bthe', scores, v) + jnp.einsum('bthd,bhde->bthe', q * jnp.exp(b), s0)
    b_end = b[:, -1]
    k_dec = k * jnp.exp(b_end[:, None] - b)
    s_new = jnp.exp(b_end)[..., None] * s0 + jnp.einsum('bshd,bshe->bhde', k_dec, v)
    return s_new, o


def hgrn2_prompt(q, k, v, log_f):
    B, S, H, _ = q.shape
    nc = S // CHUNK
    blocks = lambda a: jnp.moveaxis(a.reshape(B, nc, CHUNK, H, a.shape[-1]), 1, 0)
    s0 = jnp.zeros((B, H, HG_DK, HG_DV), jnp.float32)
    s_fin, o = lax.scan(lambda s, xs: gla_chunk(s, *xs), s0,
                        (blocks(q), blocks(k), blocks(v), blocks(log_f)))
    return s_fin, jnp.moveaxis(o, 0, 1).reshape(B, S, H, HG_DV)


def mixer_chunk_attention(hn, w_in, table, k_cache, v_cache):
    B, T, _ = hn.shape
    z = hn @ w_in
    q, k, v, mq = jnp.split(z, [MIX_W, 2 * MIX_W, 3 * MIX_W], axis=-1)
    heads = lambda a: a.reshape(B, T, N_MIX_HEADS, HEAD_DIM)
    q, k, v = heads(q), heads(k), heads(v)
    if k_cache is None:
        o = band_attention_prompt(q, k, v, table)
        keep = min(WINDOW, T)
        k_new, v_new = k[:, T - keep:], v[:, T - keep:]
    else:
        o = band_attention_sample(q, k, v, k_cache, v_cache, table)
        k_new, v_new = k, v
    return o.reshape(B, T, MIX_W), mq, k_new, v_new


def mixer_hgrn2(hn, w_in, lb, g_head, state):
    B, T, _ = hn.shape
    z = hn @ w_in
    q, f, i, g, mq = jnp.split(z, [MIX_W, 2 * MIX_W, 3 * MIX_W, 4 * MIX_W], axis=-1)
    ff = f.astype(jnp.float32)
    log_forget = jnp.logaddexp(jnp.log(lb), jnp.log1p(-lb) + jax.nn.log_sigmoid(ff))
    key_in = (1.0 - lb) * jax.nn.sigmoid(-ff)
    heads = lambda a, d: a.astype(jnp.float32).reshape(B, T, N_MIX_HEADS, d)
    qh, kh, vh, lf = heads(q, HG_DK), heads(key_in, HG_DK), heads(i, HG_DV), heads(log_forget, HG_DK)
    if state is None:
        s_new, o = hgrn2_prompt(qh, kh, vh, lf)
    else:
        s_new, o = gla_chunk(state.astype(jnp.float32), qh, kh, vh, lf)
    o = o * lax.rsqrt(jnp.mean(o * o, axis=-1, keepdims=True) + EPS) \
        * g_head.astype(jnp.float32).reshape(N_MIX_HEADS, HG_DV)
    o = o.reshape(B, T, MIX_W) * jax.nn.silu(g.astype(jnp.float32))
    return o.astype(hn.dtype), mq, s_new.astype(hn.dtype)


def memory_attend(mq, mk, mv):
    B, T, _ = mq.shape
    o = softmax_attend(mq.reshape(B, T, N_MEM_HEADS, HEAD_DIM), mk.astype(mq.dtype), mv.astype(mq.dtype))
    return o.reshape(B, T, MEM_W)


def peer_route(xb, w_pq, keys):
    n = xb.shape[0]
    qq = (xb @ w_pq).astype(jnp.float32).reshape(n, PEER_HEADS, 2, PEER_HALF)
    s = jnp.einsum('thcd,chkd->thck', qq, keys.astype(jnp.float32))
    top_s, top_i = lax.top_k(s, PEER_TOPK)
    cand_s = (top_s[:, :, 0, :, None] + top_s[:, :, 1, None, :]).reshape(n, PEER_HEADS, PEER_TOPK * PEER_TOPK)
    cand_i = (top_i[:, :, 0, :, None] * N_KEYS + top_i[:, :, 1, None, :]).reshape(n, PEER_HEADS, PEER_TOPK * PEER_TOPK)
    best_s, best_j = lax.top_k(cand_s, PEER_TOPK)
    idx = jnp.take_along_axis(cand_i, best_j, axis=-1)
    return idx, jax.nn.softmax(best_s, axis=-1)


def peer(hn, w_pq, keys, u, v):
    B, T, D = hn.shape
    n = B * T
    nblk = -(-n // PEER_BLOCK)
    xs = jnp.pad(hn.reshape(n, D), ((0, nblk * PEER_BLOCK - n), (0, 0))).reshape(nblk, PEER_BLOCK, D)

    def block(xb):
        idx, gate = peer_route(xb, w_pq, keys)
        a = jnp.einsum('td,thkd->thk', xb, u[idx], preferred_element_type=jnp.float32)
        coef = (gate * jax.nn.gelu(a, approximate=False)).astype(xb.dtype)
        return jnp.einsum('thk,thkd->td', coef, v[idx])

    y = lax.map(block, xs).reshape(nblk * PEER_BLOCK, D)[:n]
    return y.reshape(B, T, D)


def run_trunk(x, mem, k_cache, v_cache, s_cache, mk_cache, mv_cache, params):
    (g_norm1, g_norm2, g_mem, g_final, w_in_a, w_out_a, rel_bias, w_in_b, w_out_b,
     lb_param, g_hgrn, w_mem_kv, w_pq, peer_keys, peer_u, peer_v) = params
    prompt = mem is not None
    B = x.shape[0]
    lb_soft = jax.nn.softmax(lb_param.astype(jnp.float32), axis=0)
    lower_bounds = jnp.cumsum(lb_soft, axis=0) - lb_soft[0]
    new_k, new_v, new_s, new_mk, new_mv = [], [], [], [], []
    for i in range(DEPTH):
        j = i // 2
        hn = rmsnorm(x, g_norm1[i])
        if prompt:
            mkv = rmsnorm(mem, g_mem[i]) @ w_mem_kv[i]
            mk, mv = jnp.split(mkv, 2, axis=-1)
            mk = mk.reshape(B, N_MEM, N_MEM_HEADS, HEAD_DIM)
            mv = mv.reshape(B, N_MEM, N_MEM_HEADS, HEAD_DIM)
            new_mk.append(mk)
            new_mv.append(mv)
        else:
            mk, mv = mk_cache[i], mv_cache[i]
        if i % 2 == 0:
            o_mix, mq, kn, vn = mixer_chunk_attention(
                hn, w_in_a[j], rel_bias[j],
                None if prompt else k_cache[j], None if prompt else v_cache[j])
            new_k.append(kn)
            new_v.append(vn)
            w_out = w_out_a[j]
        else:
            o_mix, mq, sn = mixer_hgrn2(hn, w_in_b[j], lower_bounds[i], g_hgrn[j],
                                        None if prompt else s_cache[j])
            new_s.append(sn)
            w_out = w_out_b[j]
        o_mem = memory_attend(mq, mk, mv)
        x = x + jnp.concatenate([o_mix, o_mem], axis=-1) @ w_out
        x = x + peer(rmsnorm(x, g_norm2[i]), w_pq[i], peer_keys[i], peer_u[i], peer_v[i])
    y = rmsnorm(x, g_final)
    mk_out = jnp.stack(new_mk) if new_mk else None
    mv_out = jnp.stack(new_mv) if new_mv else None
    return y, jnp.stack(new_k), jnp.stack(new_v), jnp.stack(new_s), mk_out, mv_out


def setup_inputs(seed: int = 0) -> dict:
    key = jax.random.key(seed)
    ks = jax.random.split(key, 26)
    f32 = jnp.float32
    nrm = lambda k, shape, scale: jax.random.normal(k, shape, f32) * scale
    gain = lambda k, shape: 1.0 + 0.02 * jax.random.normal(k, shape, f32)
    band = min(WINDOW, PAST_LEN)
    return {
        'x_prompt': nrm(ks[0], (BATCH, SEQ, D_MODEL), 1.0),
        'x_sample': nrm(ks[1], (DEC_BATCH, DEC_SEQ, D_MODEL), 1.0),
        'cache_attn_k': nrm(ks[2], (N_A, DEC_BATCH, band, N_MIX_HEADS, HEAD_DIM), 1.0),
        'cache_attn_v': nrm(ks[3], (N_A, DEC_BATCH, band, N_MIX_HEADS, HEAD_DIM), 1.0),
        'state_hgrn': nrm(ks[4], (N_B, DEC_BATCH, N_MIX_HEADS, HG_DK, HG_DV), 0.5),
        'cache_mem_k': nrm(ks[5], (DEPTH, DEC_BATCH, N_MEM, N_MEM_HEADS, HEAD_DIM), 1.0),
        'cache_mem_v': nrm(ks[6], (DEPTH, DEC_BATCH, N_MEM, N_MEM_HEADS, HEAD_DIM), 1.0),
        'mem_prompt': nrm(ks[7], (BATCH, N_MEM, D_MODEL), 1.0),
        'g_norm1': gain(ks[8], (DEPTH, D_MODEL)),
        'g_norm2': gain(ks[9], (DEPTH, D_MODEL)),
        'g_mem': gain(ks[10], (DEPTH, D_MODEL)),
        'g_final': gain(ks[11], (D_MODEL,)),
        'w_in_a': nrm(ks[12], (N_A, D_MODEL, 3 * MIX_W + MEM_W), D_MODEL ** -0.5),
        'w_out_a': nrm(ks[13], (N_A, MIX_W + MEM_W, D_MODEL), (MIX_W + MEM_W) ** -0.5),
        'rel_bias': nrm(ks[14], (N_A, 2 * REL_CLIP + 1, N_MIX_HEADS), 0.5),
        'w_in_b': nrm(ks[15], (N_B, D_MODEL, 4 * MIX_W + MEM_W), D_MODEL ** -0.5),
        'w_out_b': nrm(ks[16], (N_B, MIX_W + MEM_W, D_MODEL), (MIX_W + MEM_W) ** -0.5),
        'lb_param': nrm(ks[17], (DEPTH, MIX_W), 1.0),
        'g_hgrn': gain(ks[18], (N_B, MIX_W)),
        'w_mem_kv': nrm(ks[19], (DEPTH, D_MODEL, 2 * MEM_W), D_MODEL ** -0.5),
        'w_pq': nrm(ks[20], (DEPTH, D_MODEL, PEER_HEADS * PEER_DQ), D_MODEL ** -0.5),
        'peer_keys': nrm(ks[21], (DEPTH, 2, PEER_HEADS, N_KEYS, PEER_HALF), PEER_HALF ** -0.5),
        'peer_u': nrm(ks[22], (DEPTH, N_EXPERTS, D_MODEL), D_MODEL ** -0.5),
        'peer_v': nrm(ks[23], (DEPTH, N_EXPERTS, D_MODEL), 0.25),
    }


def reference(x_prompt, x_sample, cache_attn_k, cache_attn_v, state_hgrn, cache_mem_k, cache_mem_v,
              mem_prompt, g_norm1, g_norm2, g_mem, g_final, w_in_a, w_out_a, rel_bias, w_in_b, w_out_b,
              lb_param, g_hgrn, w_mem_kv, w_pq, peer_keys, peer_u, peer_v):
    params = (g_norm1, g_norm2, g_mem, g_final, w_in_a, w_out_a, rel_bias, w_in_b, w_out_b,
              lb_param, g_hgrn, w_mem_kv, w_pq, peer_keys, peer_u, peer_v)
    y_prompt, attn_k_prompt, attn_v_prompt, hgrn_state_prompt, mem_k_prompt, mem_v_prompt = run_trunk(
        x_prompt, mem_prompt, None, None, None, None, None, params)
    y_sample, attn_k_sample, attn_v_sample, hgrn_state_sample, _, _ = run_trunk(
        x_sample, None, cache_attn_k, cache_attn_v, state_hgrn, cache_mem_k, cache_mem_v, params)
    return (y_prompt, y_sample, attn_k_prompt, attn_v_prompt, attn_k_sample, attn_v_sample,
            hgrn_state_prompt, hgrn_state_sample, mem_k_prompt, mem_v_prompt)
```

```python
import functools
import math

import jax
import jax.numpy as jnp
from jax import lax
from jax.experimental import pallas as pl
from jax.experimental.pallas import tpu as pltpu

F32 = jnp.float32
BF16 = jnp.bfloat16

D_MODEL = 2048
PAST_LEN = 1024
CHUNK = 64
LEFT_CHUNKS = 8
WINDOW = LEFT_CHUNKS * CHUNK
HEAD_DIM = 128
N_MIX_HEADS = 12
N_MEM_HEADS = 4
MIX_W = N_MIX_HEADS * HEAD_DIM
MEM_W = N_MEM_HEADS * HEAD_DIM
N_MEM = 256
REL_CLIP = 256
PEER_HEADS = 8
N_KEYS = 128
N_EXPERTS = N_KEYS * N_KEYS
PEER_TOPK = 16
EPS = 1e-6
NEG_INF = -1e30

LANES = 128
SUBLANES = 8
SUB = SUBLANES
VMEM_LIMIT = 56 * 1024 * 1024

TM_PROJ = 512
TN_PROJ = 512
TB_ROUTE = 256
TB_PEER = 512
EB_PEER = 512
LC_PEER = 256
HB_MIX = 4


def _cparams(sem):
    return pltpu.CompilerParams(dimension_semantics=sem, vmem_limit_bytes=VMEM_LIMIT)


def _rms(x, g):
    return x * lax.rsqrt(jnp.mean(x * x, axis=-1, keepdims=True) + EPS) * g


def _norm_matmul_kernel(x_ref, g_ref, w_ref, o_ref, hn_ref):
    @pl.when(pl.program_id(1) == 0)
    def _():
        hn_ref[...] = _rms(x_ref[...], g_ref[...]).astype(BF16)

    o_ref[...] = jnp.dot(hn_ref[...], w_ref[...], preferred_element_type=F32)


def _norm_matmul(x, g, w):
    m, d = x.shape
    n = w.shape[1]
    tm = min(TM_PROJ, m)
    tn = min(TN_PROJ, n)
    return pl.pallas_call(
        _norm_matmul_kernel,
        grid=(m // tm, n // tn),
        in_specs=[
            pl.BlockSpec((tm, d), lambda i, j: (i, 0)),
            pl.BlockSpec((1, d), lambda i, j: (0, 0)),
            pl.BlockSpec((d, tn), lambda i, j: (0, j)),
        ],
        out_specs=pl.BlockSpec((tm, tn), lambda i, j: (i, j)),
        out_shape=jax.ShapeDtypeStruct((m, n), F32),
        scratch_shapes=[pltpu.VMEM((tm, d), BF16)],
        compiler_params=_cparams(("parallel", "arbitrary")),
        name="norm_matmul",
    )(x, g.reshape(1, d), w)


def _bias_kernel(tab_ref, o_ref, *, tq, tk, off):
    h = pl.program_id(0)
    qi = lax.broadcasted_iota(jnp.int32, (tq, tk), 0)
    ki = lax.broadcasted_iota(jnp.int32, (tq, tk), 1)
    rel = jnp.clip(qi - ki + off, -REL_CLIP, REL_CLIP) + REL_CLIP
    lo = max(min(-(tk - 1) + off, REL_CLIP), -REL_CLIP) + REL_CLIP
    hi = max(min((tq - 1) + off, REL_CLIP), -REL_CLIP) + REL_CLIP

    def body(r, acc):
        return jnp.where(rel == r, tab_ref[h, r], acc)

    o_ref[0] = lax.fori_loop(lo, hi + 1, body, jnp.zeros((tq, tk), F32))


def _band_bias(table, tq, tk, off):
    nh = table.shape[1]
    return pl.pallas_call(
        functools.partial(_bias_kernel, tq=tq, tk=tk, off=off),
        grid=(nh,),
        in_specs=[pl.BlockSpec(memory_space=pltpu.SMEM)],
        out_specs=pl.BlockSpec((1, tq, tk), lambda h: (h, 0, 0)),
        out_shape=jax.ShapeDtypeStruct((nh, tq, tk), F32),
        compiler_params=_cparams(("arbitrary",)),
        name="band_bias",
    )(table.T)


def _softmax_rows(s):
    m = jnp.max(s, axis=-1, keepdims=True)
    p = jnp.exp(s - m)
    return p / jnp.sum(p, axis=-1, keepdims=True)


def _nt_dot(a, b):
    return lax.dot_general(a, b, (((1,), (1,)), ((), ())), preferred_element_type=F32)


def _tn_dot(a, b):
    return lax.dot_general(a, b, (((0,), (0,)), ((), ())), preferred_element_type=F32)


def _band_prompt_kernel(q_ref, k_ref, v_ref, bias_ref, o_ref, *, hb):
    c = pl.program_id(2)
    start = pl.multiple_of(c * CHUNK, CHUNK)
    tk = WINDOW + CHUNK
    k_pos = lax.broadcasted_iota(jnp.int32, (CHUNK, tk), 1) + (c * CHUNK - WINDOW)
    mask = k_pos >= 0
    scale = HEAD_DIM ** -0.5
    for h in range(hb):
        sl = slice(h * HEAD_DIM, (h + 1) * HEAD_DIM)
        q = q_ref[:, sl].astype(BF16)
        kw = k_ref[pl.ds(start, tk), sl].astype(BF16)
        vw = v_ref[pl.ds(start, tk), sl].astype(BF16)
        s = _nt_dot(q, kw) * scale
        s = jnp.where(mask, s + bias_ref[h], NEG_INF)
        p = _softmax_rows(s).astype(BF16)
        o_ref[:, sl] = jnp.dot(p, vw, preferred_element_type=F32).astype(BF16)


def _band_attention_prompt(z3, kv_pad, bias, hb=HB_MIX):
    b, s, _ = z3.shape
    nc = s // CHUNK
    ng = N_MIX_HEADS // hb
    w = hb * HEAD_DIM
    tk = WINDOW + CHUNK
    return pl.pallas_call(
        functools.partial(_band_prompt_kernel, hb=hb),
        grid=(b, ng, nc),
        in_specs=[
            pl.BlockSpec((None, CHUNK, w), lambda i, g, c: (i, c, g)),
            pl.BlockSpec((None, WINDOW + s, w), lambda i, g, c: (i, 0, g)),
            pl.BlockSpec((None, WINDOW + s, w), lambda i, g, c: (i, 0, ng + g)),
            pl.BlockSpec((hb, CHUNK, tk), lambda i, g, c: (g, 0, 0)),
        ],
        out_specs=pl.BlockSpec((None, CHUNK, w), lambda i, g, c: (i, c, g)),
        out_shape=jax.ShapeDtypeStruct((b, s, MIX_W), BF16),
        compiler_params=_cparams(("parallel", "parallel", "arbitrary")),
        name="band_attn_prompt",
    )(z3, kv_pad, kv_pad, bias)


def _band_sample_kernel(q_ref, kn_ref, vn_ref, kc_ref, vc_ref, bias_ref, o_ref, *, hb, p_len):
    scale = HEAD_DIM ** -0.5
    for h in range(hb):
        sl = slice(h * HEAD_DIM, (h + 1) * HEAD_DIM)
        q = q_ref[:, sl].astype(BF16)
        s_c = _nt_dot(q, kc_ref[:, sl].astype(BF16)) * scale + bias_ref[h, :, :p_len]
        s_n = _nt_dot(q, kn_ref[:, sl].astype(BF16)) * scale + bias_ref[h, :, p_len:]
        m = jnp.maximum(jnp.max(s_c, axis=-1, keepdims=True), jnp.max(s_n, axis=-1, keepdims=True))
        e_c = jnp.exp(s_c - m)
        e_n = jnp.exp(s_n - m)
        l = jnp.sum(e_c, axis=-1, keepdims=True) + jnp.sum(e_n, axis=-1, keepdims=True)
        o = jnp.dot((e_c / l).astype(BF16), vc_ref[:, sl].astype(BF16), preferred_element_type=F32)
        o = o + jnp.dot((e_n / l).astype(BF16), vn_ref[:, sl].astype(BF16), preferred_element_type=F32)
        o_ref[:, sl] = o.astype(BF16)


def _band_attention_sample(z3, k_cache, v_cache, bias, hb=HB_MIX):
    b, t, _ = z3.shape
    p_len = k_cache.shape[1]
    ng = N_MIX_HEADS // hb
    w = hb * HEAD_DIM
    return pl.pallas_call(
        functools.partial(_band_sample_kernel, hb=hb, p_len=p_len),
        grid=(b, ng),
        in_specs=[
            pl.BlockSpec((None, t, w), lambda i, g: (i, 0, g)),
            pl.BlockSpec((None, t, w), lambda i, g: (i, 0, ng + g)),
            pl.BlockSpec((None, t, w), lambda i, g: (i, 0, 2 * ng + g)),
            pl.BlockSpec((None, p_len, w), lambda i, g: (i, 0, g)),
            pl.BlockSpec((None, p_len, w), lambda i, g: (i, 0, g)),
            pl.BlockSpec((hb, t, p_len + t), lambda i, g: (g, 0, 0)),
        ],
        out_specs=pl.BlockSpec((None, t, w), lambda i, g: (i, 0, g)),
        out_shape=jax.ShapeDtypeStruct((b, t, MIX_W), BF16),
        compiler_params=_cparams(("parallel", "parallel")),
        name="band_attn_sample",
    )(z3, z3, z3, k_cache, v_cache, bias)


def _mem_attn_kernel(q_ref, k_ref, v_ref, o_ref):
    q = q_ref[...].astype(BF16)
    s = _nt_dot(q, k_ref[...].astype(BF16)) * (HEAD_DIM ** -0.5)
    p = _softmax_rows(s).astype(BF16)
    o_ref[...] = jnp.dot(p, v_ref[...].astype(BF16), preferred_element_type=F32).astype(BF16)


def _mem_attention(z3, q_blk, karr, k_blk, varr, v_blk):
    b, t, _ = z3.shape
    tq = min(t, 512)
    return pl.pallas_call(
        _mem_attn_kernel,
        grid=(b, N_MEM_HEADS, t // tq),
        in_specs=[
            pl.BlockSpec((None, tq, HEAD_DIM), lambda i, h, j: (i, j, q_blk + h)),
            pl.BlockSpec((None, N_MEM, HEAD_DIM), lambda i, h, j: (i, 0, k_blk + h)),
            pl.BlockSpec((None, N_MEM, HEAD_DIM), lambda i, h, j: (i, 0, v_blk + h)),
        ],
        out_specs=pl.BlockSpec((None, tq, HEAD_DIM), lambda i, h, j: (i, j, h)),
        out_shape=jax.ShapeDtypeStruct((b, t, MEM_W), BF16),
        compiler_params=_cparams(("parallel", "parallel", "arbitrary")),
        name="mem_attn",
    )(z3, karr, varr)


def _log_sigmoid(x):
    return jnp.minimum(x, 0.0) - jnp.log1p(jnp.exp(-jnp.abs(x)))


def _logaddexp(a, b):
    hi = jnp.maximum(a, b)
    lo = jnp.minimum(a, b)
    return jnp.where(hi == -jnp.inf, hi, hi + jnp.log1p(jnp.exp(lo - hi)))


def _hgrn_kernel(*refs, hb, t, layer, has_state):
    if has_state:
        q_ref, f_ref, i_ref, g_ref, lbp_ref, gh_ref, s0_ref, o_ref, sout_ref, st_ref = refs
    else:
        q_ref, f_ref, i_ref, g_ref, lbp_ref, gh_ref, o_ref, sout_ref, st_ref = refs
    c = pl.program_id(2)
    nsub = t // SUB

    @pl.when(c == 0)
    def _():
        for h in range(hb):
            if has_state:
                st_ref[h] = s0_ref[h].T
            else:
                st_ref[h] = jnp.zeros((HEAD_DIM, HEAD_DIM), F32)

    row = lax.broadcasted_iota(jnp.int32, (t, t), 0)
    col = lax.broadcasted_iota(jnp.int32, (t, t), 1)
    rowd = lax.broadcasted_iota(jnp.int32, (t, HEAD_DIM), 0)

    for h in range(hb):
        sl = slice(h * HEAD_DIM, (h + 1) * HEAD_DIM)
        q = q_ref[:, sl]
        ff = f_ref[:, sl]
        v = i_ref[:, sl]
        g = g_ref[:, sl]

        lbp = lbp_ref[:, sl]
        e = jnp.exp(lbp - jnp.max(lbp, axis=0, keepdims=True))
        soft = e / jnp.sum(e, axis=0, keepdims=True)
        acc = soft[0:1]
        for r in range(1, layer + 1):
            acc = acc + soft[r:r + 1]
        lb = acc - soft[0:1]

        log_f = _logaddexp(jnp.log(lb), jnp.log1p(-lb) + _log_sigmoid(ff))
        k = (1.0 - lb) * jax.nn.sigmoid(-ff)

        b = log_f
        sh = 1
        while sh < t:
            b = b + jnp.where(rowd >= sh, pltpu.roll(b, sh, 0), 0.0)
            sh *= 2
        b_end = b[t - 1:t]

        st = st_ref[h]
        o = _nt_dot((q * jnp.exp(b)).astype(BF16), st.astype(BF16))

        sc = jnp.zeros((t, t), F32)
        for dl in range(SUB):
            if dl == 0:
                prod = q * k
            else:
                prod = q * pltpu.roll(k, dl, 0) * jnp.exp(jnp.minimum(b - pltpu.roll(b, dl, 0), 0.0))
            r = jnp.sum(prod, axis=1, keepdims=True)
            hit = jnp.logical_and(row - col == dl, jnp.bitwise_and(row, SUB - 1) >= dl)
            sc = jnp.where(hit, r, sc)

        q_parts = []
        k_rows = []
        for i in range(nsub):
            b_ref = b[i * SUB + SUB - 1:i * SUB + SUB]
            q_parts.append((q * jnp.exp(jnp.minimum(b - b_ref, 0.0))).astype(BF16))
            kh = k[i * SUB:(i + 1) * SUB] * jnp.exp(jnp.minimum(b_ref - b[i * SUB:(i + 1) * SUB], 0.0))
            zero = jnp.zeros_like(kh)
            k_rows.append(jnp.concatenate([kh if j == i else zero for j in range(nsub)], axis=1))
        q_cat = jnp.concatenate(q_parts, axis=1)
        k_bd = jnp.concatenate(k_rows, axis=0).astype(BF16)
        sc_far = _nt_dot(q_cat, k_bd)
        far = jnp.right_shift(col, 3) < jnp.right_shift(row, 3)
        scores = jnp.where(far, sc_far, sc)

        o = o + jnp.dot(scores.astype(BF16), v.astype(BF16), preferred_element_type=F32)

        k_dec = (k * jnp.exp(jnp.minimum(b_end - b, 0.0))).astype(BF16)
        st_new = st * jnp.exp(b_end) + _tn_dot(v.astype(BF16), k_dec)
        st_ref[h] = st_new

        o = _rms(o, gh_ref[:, sl])
        o = o * (g * jax.nn.sigmoid(g))
        o_ref[:, sl] = o.astype(BF16)

        @pl.when(c == pl.num_programs(2) - 1)
        def _():
            sout_ref[h] = st_new.T


def _hgrn(z3, lb_param, g_head, state, layer, hb=HB_MIX):
    assert SUB == 8
    b, s, _ = z3.shape
    t = min(CHUNK, s)
    nc = s // t
    ng = N_MIX_HEADS // hb
    w = hb * HEAD_DIM
    depth = lb_param.shape[0]
    has_state = state is not None

    def zspec(seg):
        return pl.BlockSpec((None, t, w), lambda i, g, c, seg=seg: (i, c, seg * ng + g))

    in_specs = [zspec(0), zspec(1), zspec(2), zspec(3),
                pl.BlockSpec((depth, w), lambda i, g, c: (0, g)),
                pl.BlockSpec((1, w), lambda i, g, c: (0, g))]
    args = [z3, z3, z3, z3, lb_param, g_head.reshape(1, MIX_W)]
    if has_state:
        in_specs.append(pl.BlockSpec((None, hb, HEAD_DIM, HEAD_DIM), lambda i, g, c: (i, g, 0, 0)))
        args.append(state)
    return pl.pallas_call(
        functools.partial(_hgrn_kernel, hb=hb, t=t, layer=layer, has_state=has_state),
        grid=(b, ng, nc),
        in_specs=in_specs,
        out_specs=[
            pl.BlockSpec((None, t, w), lambda i, g, c: (i, c, g)),
            pl.BlockSpec((None, hb, HEAD_DIM, HEAD_DIM), lambda i, g, c: (i, g, 0, 0)),
        ],
        out_shape=[
            jax.ShapeDtypeStruct((b, s, MIX_W), BF16),
            jax.ShapeDtypeStruct((b, N_MIX_HEADS, HEAD_DIM, HEAD_DIM), F32),
        ],
        scratch_shapes=[pltpu.VMEM((hb, HEAD_DIM, HEAD_DIM), F32)],
        compiler_params=_cparams(("parallel", "parallel", "arbitrary")),
        name="hgrn2",
    )(*args)


def _out_proj_kernel(om_ref, oe_ref, wm_ref, we_ref, x_ref, g_ref, x1_ref, xnt_ref):
    x1 = x_ref[...] + jnp.dot(om_ref[...], wm_ref[...], preferred_element_type=F32)
    x1 = x1 + jnp.dot(oe_ref[...], we_ref[...], preferred_element_type=F32)
    x1_ref[...] = x1
    xnt_ref[...] = _rms(x1, g_ref[...]).T.astype(BF16)


def _out_proj(o_mix, o_mem, w_out, x, g2):
    n, d = x.shape
    tm = min(256, n)
    return pl.pallas_call(
        _out_proj_kernel,
        grid=(n // tm,),
        in_specs=[
            pl.BlockSpec((tm, MIX_W), lambda i: (i, 0)),
            pl.BlockSpec((tm, MEM_W), lambda i: (i, 0)),
            pl.BlockSpec((MIX_W, d), lambda i: (0, 0)),
            pl.BlockSpec((MEM_W, d), lambda i: (MIX_W // MEM_W, 0)),
            pl.BlockSpec((tm, d), lambda i: (i, 0)),
            pl.BlockSpec((1, d), lambda i: (0, 0)),
        ],
        out_specs=[
            pl.BlockSpec((tm, d), lambda i: (i, 0)),
            pl.BlockSpec((d, tm), lambda i: (0, i)),
        ],
        out_shape=[jax.ShapeDtypeStruct((n, d), F32), jax.ShapeDtypeStruct((d, n), BF16)],
        compiler_params=_cparams(("parallel",)),
        name="out_proj",
    )(o_mix, o_mem, w_out, w_out, x, g2.reshape(1, d))


def _extract_top(vals, order, rounds):
    big = jnp.int32(1 << 20)
    rank = jnp.full(vals.shape, PEER_TOPK, jnp.int32)
    tops = []
    for r in range(rounds):
        m = jnp.max(vals, axis=0, keepdims=True)
        pick = jnp.min(jnp.where(vals == m, order, big), axis=0, keepdims=True)
        sel = order == pick
        rank = jnp.where(sel, r, rank)
        vals = jnp.where(sel, -jnp.inf, vals)
        tops.append(m)
    return jnp.concatenate(tops, axis=0), rank


def _route_kernel(xnt_ref, wq_ref, keys_ref, r2_ref, e2_ref, c1_ref, e1_ref):
    tb = xnt_ref.shape[1]
    qq = jnp.dot(wq_ref[...], xnt_ref[...], preferred_element_type=F32).astype(BF16)
    s1 = jnp.dot(keys_ref[0], qq[:N_KEYS], preferred_element_type=F32)
    s2 = jnp.dot(keys_ref[1], qq[N_KEYS:], preferred_element_type=F32)
    key_id = lax.broadcasted_iota(jnp.int32, (N_KEYS, tb), 0)
    top1, rank1 = _extract_top(s1, key_id, PEER_TOPK)
    top2, rank2 = _extract_top(s2, key_id, PEER_TOPK)

    blk_a = top1[0:1] + top2
    blk_b = [top1[a:a + 1] + top2[0:8] for a in range(1, 8)]
    blk_c = top1[8:16] + top2[0:1]
    cand = jnp.concatenate([blk_a] + blk_b + [blk_c], axis=0)
    nrow = cand.shape[0]
    r = lax.broadcasted_iota(jnp.int32, (nrow, tb), 0)
    a_mid = 1 + jnp.right_shift(r - 16, 3)
    b_mid = jnp.bitwise_and(r - 16, 7)
    cand_id = jnp.where(r < 16, r, jnp.where(r < 72, a_mid * PEER_TOPK + b_mid, (r - 64) * PEER_TOPK))
    valid = jnp.logical_or(jnp.logical_or(r < 16, r >= 72), (a_mid + 1) * (b_mid + 1) <= PEER_TOPK)
    cand = jnp.where(valid, cand, -jnp.inf)
    best, crank = _extract_top(cand, cand_id, PEER_TOPK)
    chosen = (crank < PEER_TOPK).astype(F32)
    z = jnp.sum(jnp.exp(best - best[0:1]), axis=0, keepdims=True)

    counts = [jnp.sum(chosen[0:16], axis=0, keepdims=True)]
    counts += [jnp.sum(chosen[16 + 8 * (a - 1):16 + 8 * a], axis=0, keepdims=True) for a in range(1, 8)]
    counts += [chosen[72 + a - 8:72 + a - 7] for a in range(8, 16)]
    c1 = jnp.zeros((N_KEYS, tb), F32)
    for a in range(PEER_TOPK):
        c1 = jnp.where(rank1 == a, counts[a], c1)

    r2_ref[...] = rank2.astype(F32)
    c1_ref[...] = c1
    e1_ref[...] = jnp.exp(s1 - top1[0:1])
    e2_ref[...] = jnp.exp(s2 - top2[0:1]) / z


def _peer_route(xnt, w_pq_t, keys):
    d, n = xnt.shape
    tb = min(TB_ROUTE, n)
    out = jax.ShapeDtypeStruct((PEER_HEADS, N_KEYS, n), F32)
    ospec = pl.BlockSpec((None, N_KEYS, tb), lambda i, h: (h, 0, i))
    return pl.pallas_call(
        _route_kernel,
        grid=(n // tb, PEER_HEADS),
        in_specs=[
            pl.BlockSpec((d, tb), lambda i, h: (0, i)),
            pl.BlockSpec((2 * N_KEYS, d), lambda i, h: (h, 0)),
            pl.BlockSpec((2, None, N_KEYS, N_KEYS), lambda i, h: (0, h, 0, 0)),
        ],
        out_specs=[ospec, ospec, ospec, ospec],
        out_shape=[out, out, out, out],
        compiler_params=_cparams(("parallel", "arbitrary")),
        name="peer_route",
    )(xnt, w_pq_t, keys)


def _gelu(x):
    return 0.5 * x * (1.0 + lax.erf(x * (0.5 ** 0.5)))


def _peer_kernel(xnt_ref, u_ref, vt_ref, r2_ref, e2_ref, c1_ref, e1_ref, y_ref, a_ref, coef_ref, *, eb, lc):
    e = pl.program_id(1)
    tb = xnt_ref.shape[1]

    @pl.when(e == 0)
    def _():
        y_ref[...] = jnp.zeros_like(y_ref)

    a_ref[...] = jnp.dot(u_ref[...], xnt_ref[...], preferred_element_type=F32)
    for ii in range(eb // N_KEYS):
        i = e * (eb // N_KEYS) + ii
        rows = slice(ii * N_KEYS, (ii + 1) * N_KEYS)
        for l0 in range(0, tb, lc):
            cols = slice(l0, l0 + lc)
            gate = jnp.zeros((N_KEYS, lc), F32)
            for h in range(PEER_HEADS):
                c1 = c1_ref[h, pl.ds(i, 1), cols]
                e1 = e1_ref[h, pl.ds(i, 1), cols]
                gate = gate + jnp.where(r2_ref[h, :, cols] < c1, e2_ref[h, :, cols], 0.0) * e1
            coef_ref[rows, cols] = (gate * _gelu(a_ref[rows, cols])).astype(BF16)
    y_ref[...] += jnp.dot(vt_ref[...], coef_ref[...], preferred_element_type=F32)


def _peer_sweep(xnt, u, vt, r2, e2, c1, e1):
    d, n = xnt.shape
    tb = min(TB_PEER, n)
    eb = EB_PEER
    lc = min(LC_PEER, tb)
    rspec = pl.BlockSpec((PEER_HEADS, N_KEYS, tb), lambda t, e: (0, 0, t))
    return pl.pallas_call(
        functools.partial(_peer_kernel, eb=eb, lc=lc),
        grid=(n // tb, N_EXPERTS // eb),
        in_specs=[
            pl.BlockSpec((d, tb), lambda t, e: (0, t)),
            pl.BlockSpec((eb, d), lambda t, e: (e, 0)),
            pl.BlockSpec((d, eb), lambda t, e: (0, e)),
            rspec, rspec, rspec, rspec,
        ],
        out_specs=pl.BlockSpec((d, tb), lambda t, e: (0, t)),
        out_shape=jax.ShapeDtypeStruct((d, n), F32),
        scratch_shapes=[pltpu.VMEM((eb, tb), F32), pltpu.VMEM((eb, tb), BF16)],
        compiler_params=_cparams(("parallel", "arbitrary")),
        name="peer_sweep",
    )(xnt, u, vt, r2, e2, c1, e1)


def _add_t_kernel(x_ref, yt_ref, g_ref, o_ref, *, final):
    x = x_ref[...] + yt_ref[...].T
    if final:
        x = _rms(x, g_ref[...])
    o_ref[...] = x


def _add_transposed(x1, yt, g, final):
    n, d = x1.shape
    tm = min(256, n)
    return pl.pallas_call(
        functools.partial(_add_t_kernel, final=final),
        grid=(n // tm,),
        in_specs=[
            pl.BlockSpec((tm, d), lambda i: (i, 0)),
            pl.BlockSpec((d, tm), lambda i: (0, i)),
            pl.BlockSpec((1, d), lambda i: (0, 0)),
        ],
        out_specs=pl.BlockSpec((tm, d), lambda i: (i, 0)),
        out_shape=jax.ShapeDtypeStruct((n, d), F32),
        compiler_params=_cparams(("parallel",)),
        name="peer_residual",
    )(x1, yt, g.reshape(1, d))


def _trunk(x, mem, k_cache, v_cache, s_cache, mk_cache, mv_cache, p):
    prompt = mem is not None
    bsz, t, d = x.shape
    n = bsz * t
    depth = p["g_norm1"].shape[0]
    x2 = x.reshape(n, d)
    new_k, new_v, new_s, new_mk, new_mv = [], [], [], [], []
    for i in range(depth):
        j = i // 2
        attn = i % 2 == 0
        w_in = p["w_in_a"][j] if attn else p["w_in_b"][j]
        w_out = p["w_out_a"][j] if attn else p["w_out_b"][j]
        z = _norm_matmul(x2, p["g_norm1"][i], w_in)
        z3 = z.reshape(bsz, t, z.shape[1])
        mq_blk = (z.shape[1] - MEM_W) // HEAD_DIM

        if prompt:
            mkv = _norm_matmul(mem.reshape(bsz * N_MEM, d), p["g_mem"][i], p["w_mem_kv"][i])
            new_mk.append(mkv[:, :MEM_W].reshape(bsz, N_MEM, N_MEM_HEADS, HEAD_DIM))
            new_mv.append(mkv[:, MEM_W:].reshape(bsz, N_MEM, N_MEM_HEADS, HEAD_DIM))
            mkv3 = mkv.reshape(bsz, N_MEM, 2 * MEM_W)
            o_mem = _mem_attention(z3, mq_blk, mkv3, 0, mkv3, N_MEM_HEADS)
        else:
            o_mem = _mem_attention(z3, mq_blk, mk_cache[i].reshape(bsz, N_MEM, MEM_W), 0,
                                   mv_cache[i].reshape(bsz, N_MEM, MEM_W), 0)

        if attn:
            k_new = z3[:, :, MIX_W:2 * MIX_W]
            v_new = z3[:, :, 2 * MIX_W:3 * MIX_W]
            if prompt:
                bias = _band_bias(p["rel_bias"][j], CHUNK, WINDOW + CHUNK, WINDOW)
                kv_pad = jnp.pad(z3[:, :, MIX_W:3 * MIX_W], ((0, 0), (WINDOW, 0), (0, 0)))
                o_mix = _band_attention_prompt(z3, kv_pad, bias)
                keep = min(WINDOW, t)
                k_new, v_new = k_new[:, t - keep:], v_new[:, t - keep:]
            else:
                p_len = k_cache.shape[2]
                bias = _band_bias(p["rel_bias"][j], t, p_len + t, p_len)
                o_mix = _band_attention_sample(z3, k_cache[j].reshape(bsz, p_len, MIX_W),
                                               v_cache[j].reshape(bsz, p_len, MIX_W), bias)
            new_k.append(k_new.reshape(bsz, -1, N_MIX_HEADS, HEAD_DIM))
            new_v.append(v_new.reshape(bsz, -1, N_MIX_HEADS, HEAD_DIM))
        else:
            o_mix, s_new = _hgrn(z3, p["lb_param"], p["g_hgrn"][j], None if prompt else s_cache[j], i)
            new_s.append(s_new)

        x1, xnt = _out_proj(o_mix.reshape(n, MIX_W), o_mem.reshape(n, MEM_W), w_out, x2, p["g_norm2"][i])
        r2, e2, c1, e1 = _peer_route(xnt, p["w_pq_t"][i], p["peer_keys"][i])
        yt = _peer_sweep(xnt, p["peer_u"][i], p["peer_v_t"][i], r2, e2, c1, e1)
        x2 = _add_transposed(x1, yt, p["g_final"], final=(i == depth - 1))

    y = x2.reshape(bsz, t, d)
    mk_out = jnp.stack(new_mk) if new_mk else None
    mv_out = jnp.stack(new_mv) if new_mv else None
    return y, jnp.stack(new_k), jnp.stack(new_v), jnp.stack(new_s), mk_out, mv_out


def kernel(x_prompt, x_sample, cache_attn_k, cache_attn_v, state_hgrn, cache_mem_k, cache_mem_v, mem_prompt,
           g_norm1, g_norm2, g_mem, g_final, w_in_a, w_out_a, rel_bias, w_in_b, w_out_b, lb_param, g_hgrn,
           w_mem_kv, w_pq, peer_keys, peer_u, peer_v):
    p = dict(
        g_norm1=g_norm1, g_norm2=g_norm2, g_mem=g_mem, g_final=g_final, rel_bias=rel_bias,
        lb_param=lb_param, g_hgrn=g_hgrn,
        w_in_a=w_in_a.astype(BF16), w_out_a=w_out_a.astype(BF16),
        w_in_b=w_in_b.astype(BF16), w_out_b=w_out_b.astype(BF16),
        w_mem_kv=w_mem_kv.astype(BF16),
        w_pq_t=jnp.swapaxes(w_pq, 1, 2).astype(BF16),
        peer_keys=peer_keys.astype(BF16),
        peer_u=peer_u.astype(BF16),
        peer_v_t=jnp.swapaxes(peer_v, 1, 2).astype(BF16),
    )
    y_p, k_p, v_p, s_p, mk_p, mv_p = _trunk(x_prompt, mem_prompt, None, None, None, None, None, p)
    y_s, k_s, v_s, s_s, _, _ = _trunk(x_sample, None, cache_attn_k, cache_attn_v, state_hgrn,
                                      cache_mem_k, cache_mem_v, p)
    return (y_p, y_s, k_p, v_p, k_s, v_s, s_p, s_s, mk_p, mv_p)
```

```python
import functools
import math

import jax
import jax.numpy as jnp
from jax import lax
from jax.experimental import pallas as pl
from jax.experimental.pallas import tpu as pltpu

F32 = jnp.float32
BF16 = jnp.bfloat16

D_MODEL = 2048
PAST_LEN = 1024
CHUNK = 64
LEFT_CHUNKS = 8
WINDOW = LEFT_CHUNKS * CHUNK
HEAD_DIM = 128
N_MIX_HEADS = 12
N_MEM_HEADS = 4
MIX_W = N_MIX_HEADS * HEAD_DIM
MEM_W = N_MEM_HEADS * HEAD_DIM
N_MEM = 256
REL_CLIP = 256
PEER_HEADS = 8
N_KEYS = 128
N_EXPERTS = N_KEYS * N_KEYS
PEER_TOPK = 16
EPS = 1e-6
NEG_INF = -1e30

LANES = 128
SUBLANES = 8
SUB = SUBLANES
VMEM_LIMIT = 56 * 1024 * 1024

TM_PROJ = 512
TN_PROJ = 512
TB_ROUTE = 256
TB_PEER = 512
EB_PEER = 1024
CK_PEER = 256
LC_PEER = 256
HB_MIX = 4


def _cparams(sem, flags=None):
    return pltpu.CompilerParams(dimension_semantics=sem, vmem_limit_bytes=VMEM_LIMIT, flags=flags)


def _rms(x, g):
    return x * lax.rsqrt(jnp.mean(x * x, axis=-1, keepdims=True) + EPS) * g


def _norm_matmul_kernel(x_ref, g_ref, w_ref, o_ref, hn_ref):
    @pl.when(pl.program_id(1) == 0)
    def _():
        hn_ref[...] = _rms(x_ref[...], g_ref[...]).astype(BF16)

    o_ref[...] = jnp.dot(hn_ref[...], w_ref[...], preferred_element_type=F32)


def _norm_matmul(x, g, w):
    m, d = x.shape
    n = w.shape[1]
    tm = min(TM_PROJ, m)
    tn = min(TN_PROJ, n)
    return pl.pallas_call(
        _norm_matmul_kernel,
        grid=(m // tm, n // tn),
        in_specs=[
            pl.BlockSpec((tm, d), lambda i, j: (i, 0)),
            pl.BlockSpec((1, d), lambda i, j: (0, 0)),
            pl.BlockSpec((d, tn), lambda i, j: (0, j)),
        ],
        out_specs=pl.BlockSpec((tm, tn), lambda i, j: (i, j)),
        out_shape=jax.ShapeDtypeStruct((m, n), F32),
        scratch_shapes=[pltpu.VMEM((tm, d), BF16)],
        compiler_params=_cparams(("parallel", "arbitrary")),
        name="norm_matmul",
    )(x, g.reshape(1, d), w)


def _bias_kernel(tab_ref, o_ref, *, tq, tk, off):
    h = pl.program_id(0)
    qi = lax.broadcasted_iota(jnp.int32, (tq, tk), 0)
    ki = lax.broadcasted_iota(jnp.int32, (tq, tk), 1)
    rel = jnp.clip(qi - ki + off, -REL_CLIP, REL_CLIP) + REL_CLIP
    lo = max(min(-(tk - 1) + off, REL_CLIP), -REL_CLIP) + REL_CLIP
    hi = max(min((tq - 1) + off, REL_CLIP), -REL_CLIP) + REL_CLIP

    def body(r, acc):
        return jnp.where(rel == r, tab_ref[h, r], acc)

    o_ref[0] = lax.fori_loop(lo, hi + 1, body, jnp.zeros((tq, tk), F32))


def _band_bias(table, tq, tk, off):
    nh = table.shape[1]
    return pl.pallas_call(
        functools.partial(_bias_kernel, tq=tq, tk=tk, off=off),
        grid=(nh,),
        in_specs=[pl.BlockSpec(memory_space=pltpu.SMEM)],
        out_specs=pl.BlockSpec((1, tq, tk), lambda h: (h, 0, 0)),
        out_shape=jax.ShapeDtypeStruct((nh, tq, tk), F32),
        compiler_params=_cparams(("arbitrary",)),
        name="band_bias",
    )(table.T)


def _softmax_rows(s):
    m = jnp.max(s, axis=-1, keepdims=True)
    p = jnp.exp(s - m)
    return p / jnp.sum(p, axis=-1, keepdims=True)


def _nt_dot(a, b):
    return lax.dot_general(a, b, (((1,), (1,)), ((), ())), preferred_element_type=F32)


def _tn_dot(a, b):
    return lax.dot_general(a, b, (((0,), (0,)), ((), ())), preferred_element_type=F32)


def _band_prompt_kernel(q_ref, k_ref, v_ref, bias_ref, o_ref, *, hb):
    c = pl.program_id(2)
    start = pl.multiple_of(c * CHUNK, CHUNK)
    tk = WINDOW + CHUNK
    k_pos = lax.broadcasted_iota(jnp.int32, (CHUNK, tk), 1) + (c * CHUNK - WINDOW)
    mask = k_pos >= 0
    scale = HEAD_DIM ** -0.5
    for h in range(hb):
        sl = slice(h * HEAD_DIM, (h + 1) * HEAD_DIM)
        q = q_ref[:, sl].astype(BF16)
        kw = k_ref[pl.ds(start, tk), sl].astype(BF16)
        vw = v_ref[pl.ds(start, tk), sl].astype(BF16)
        s = _nt_dot(q, kw) * scale
        s = jnp.where(mask, s + bias_ref[h], NEG_INF)
        p = _softmax_rows(s).astype(BF16)
        o_ref[:, sl] = jnp.dot(p, vw, preferred_element_type=F32).astype(BF16)


def _band_attention_prompt(z3, kv_pad, bias, hb=HB_MIX):
    b, s, _ = z3.shape
    nc = s // CHUNK
    ng = N_MIX_HEADS // hb
    w = hb * HEAD_DIM
    tk = WINDOW + CHUNK
    return pl.pallas_call(
        functools.partial(_band_prompt_kernel, hb=hb),
        grid=(b, ng, nc),
        in_specs=[
            pl.BlockSpec((None, CHUNK, w), lambda i, g, c: (i, c, g)),
            pl.BlockSpec((None, WINDOW + s, w), lambda i, g, c: (i, 0, g)),
            pl.BlockSpec((None, WINDOW + s, w), lambda i, g, c: (i, 0, ng + g)),
            pl.BlockSpec((hb, CHUNK, tk), lambda i, g, c: (g, 0, 0)),
        ],
        out_specs=pl.BlockSpec((None, CHUNK, w), lambda i, g, c: (i, c, g)),
        out_shape=jax.ShapeDtypeStruct((b, s, MIX_W), BF16),
        compiler_params=_cparams(("parallel", "parallel", "arbitrary")),
        name="band_attn_prompt",
    )(z3, kv_pad, kv_pad, bias)


def _band_sample_kernel(q_ref, kn_ref, vn_ref, kc_ref, vc_ref, bias_ref, o_ref, *, hb, p_len):
    scale = HEAD_DIM ** -0.5
    for h in range(hb):
        sl = slice(h * HEAD_DIM, (h + 1) * HEAD_DIM)
        q = q_ref[:, sl].astype(BF16)
        s_c = _nt_dot(q, kc_ref[:, sl].astype(BF16)) * scale + bias_ref[h, :, :p_len]
        s_n = _nt_dot(q, kn_ref[:, sl].astype(BF16)) * scale + bias_ref[h, :, p_len:]
        m = jnp.maximum(jnp.max(s_c, axis=-1, keepdims=True), jnp.max(s_n, axis=-1, keepdims=True))
        e_c = jnp.exp(s_c - m)
        e_n = jnp.exp(s_n - m)
        l = jnp.sum(e_c, axis=-1, keepdims=True) + jnp.sum(e_n, axis=-1, keepdims=True)
        o = jnp.dot((e_c / l).astype(BF16), vc_ref[:, sl].astype(BF16), preferred_element_type=F32)
        o = o + jnp.dot((e_n / l).astype(BF16), vn_ref[:, sl].astype(BF16), preferred_element_type=F32)
        o_ref[:, sl] = o.astype(BF16)


def _band_attention_sample(z3, k_cache, v_cache, bias, hb=HB_MIX):
    b, t, _ = z3.shape
    p_len = k_cache.shape[1]
    ng = N_MIX_HEADS // hb
    w = hb * HEAD_DIM
    return pl.pallas_call(
        functools.partial(_band_sample_kernel, hb=hb, p_len=p_len),
        grid=(b, ng),
        in_specs=[
            pl.BlockSpec((None, t, w), lambda i, g: (i, 0, g)),
            pl.BlockSpec((None, t, w), lambda i, g: (i, 0, ng + g)),
            pl.BlockSpec((None, t, w), lambda i, g: (i, 0, 2 * ng + g)),
            pl.BlockSpec((None, p_len, w), lambda i, g: (i, 0, g)),
            pl.BlockSpec((None, p_len, w), lambda i, g: (i, 0, g)),
            pl.BlockSpec((hb, t, p_len + t), lambda i, g: (g, 0, 0)),
        ],
        out_specs=pl.BlockSpec((None, t, w), lambda i, g: (i, 0, g)),
        out_shape=jax.ShapeDtypeStruct((b, t, MIX_W), BF16),
        compiler_params=_cparams(("parallel", "parallel")),
        name="band_attn_sample",
    )(z3, z3, z3, k_cache, v_cache, bias)


def _mem_attn_kernel(q_ref, k_ref, v_ref, o_ref):
    q = q_ref[...].astype(BF16)
    s = _nt_dot(q, k_ref[...].astype(BF16)) * (HEAD_DIM ** -0.5)
    p = _softmax_rows(s).astype(BF16)
    o_ref[...] = jnp.dot(p, v_ref[...].astype(BF16), preferred_element_type=F32).astype(BF16)


def _mem_attention(z3, q_blk, karr, k_blk, varr, v_blk):
    b, t, _ = z3.shape
    tq = min(t, 512)
    return pl.pallas_call(
        _mem_attn_kernel,
        grid=(b, N_MEM_HEADS, t // tq),
        in_specs=[
            pl.BlockSpec((None, tq, HEAD_DIM), lambda i, h, j: (i, j, q_blk + h)),
            pl.BlockSpec((None, N_MEM, HEAD_DIM), lambda i, h, j: (i, 0, k_blk + h)),
            pl.BlockSpec((None, N_MEM, HEAD_DIM), lambda i, h, j: (i, 0, v_blk + h)),
        ],
        out_specs=pl.BlockSpec((None, tq, HEAD_DIM), lambda i, h, j: (i, j, h)),
        out_shape=jax.ShapeDtypeStruct((b, t, MEM_W), BF16),
        compiler_params=_cparams(("parallel", "parallel", "arbitrary")),
        name="mem_attn",
    )(z3, karr, varr)


def _log_sigmoid(x):
    return jnp.minimum(x, 0.0) - jnp.log1p(jnp.exp(-jnp.abs(x)))


def _logaddexp(a, b):
    hi = jnp.maximum(a, b)
    lo = jnp.minimum(a, b)
    return jnp.where(hi == -jnp.inf, hi, hi + jnp.log1p(jnp.exp(lo - hi)))


def _hgrn_kernel(*refs, hb, t, layer, has_state):
    if has_state:
        q_ref, f_ref, i_ref, g_ref, lbp_ref, gh_ref, s0_ref, o_ref, sout_ref, st_ref = refs
    else:
        q_ref, f_ref, i_ref, g_ref, lbp_ref, gh_ref, o_ref, sout_ref, st_ref = refs
    c = pl.program_id(2)
    nsub = t // SUB

    @pl.when(c == 0)
    def _():
        for h in range(hb):
            if has_state:
                st_ref[h] = s0_ref[h].T
            else:
                st_ref[h] = jnp.zeros((HEAD_DIM, HEAD_DIM), F32)

    row = lax.broadcasted_iota(jnp.int32, (t, t), 0)
    col = lax.broadcasted_iota(jnp.int32, (t, t), 1)
    rowd = lax.broadcasted_iota(jnp.int32, (t, HEAD_DIM), 0)

    for h in range(hb):
        sl = slice(h * HEAD_DIM, (h + 1) * HEAD_DIM)
        q = q_ref[:, sl]
        ff = f_ref[:, sl]
        v = i_ref[:, sl]
        g = g_ref[:, sl]

        lbp = lbp_ref[:, sl]
        e = jnp.exp(lbp - jnp.max(lbp, axis=0, keepdims=True))
        soft = e / jnp.sum(e, axis=0, keepdims=True)
        acc = soft[0:1]
        for r in range(1, layer + 1):
            acc = acc + soft[r:r + 1]
        lb = acc - soft[0:1]

        log_f = _logaddexp(jnp.log(lb), jnp.log1p(-lb) + _log_sigmoid(ff))
        k = (1.0 - lb) * jax.nn.sigmoid(-ff)

        b = log_f
        sh = 1
        while sh < t:
            b = b + jnp.where(rowd >= sh, pltpu.roll(b, sh, 0), 0.0)
            sh *= 2
        b_end = b[t - 1:t]

        st = st_ref[h]
        o = _nt_dot((q * jnp.exp(b)).astype(BF16), st.astype(BF16))

        sc = jnp.zeros((t, t), F32)
        for dl in range(SUB):
            if dl == 0:
                prod = q * k
            else:
                prod = q * pltpu.roll(k, dl, 0) * jnp.exp(jnp.minimum(b - pltpu.roll(b, dl, 0), 0.0))
            r = jnp.sum(prod, axis=1, keepdims=True)
            hit = jnp.logical_and(row - col == dl, jnp.bitwise_and(row, SUB - 1) >= dl)
            sc = jnp.where(hit, r, sc)

        q_parts = []
        k_rows = []
        for i in range(nsub):
            b_ref = b[i * SUB + SUB - 1:i * SUB + SUB]
            q_parts.append((q * jnp.exp(jnp.minimum(b - b_ref, 0.0))).astype(BF16))
            kh = k[i * SUB:(i + 1) * SUB] * jnp.exp(jnp.minimum(b_ref - b[i * SUB:(i + 1) * SUB], 0.0))
            zero = jnp.zeros_like(kh)
            k_rows.append(jnp.concatenate([kh if j == i else zero for j in range(nsub)], axis=1))
        q_cat = jnp.concatenate(q_parts, axis=1)
        k_bd = jnp.concatenate(k_rows, axis=0).astype(BF16)
        sc_far = _nt_dot(q_cat, k_bd)
        far = jnp.right_shift(col, 3) < jnp.right_shift(row, 3)
        scores = jnp.where(far, sc_far, sc)

        o = o + jnp.dot(scores.astype(BF16), v.astype(BF16), preferred_element_type=F32)

        k_dec = (k * jnp.exp(jnp.minimum(b_end - b, 0.0))).astype(BF16)
        st_new = st * jnp.exp(b_end) + _tn_dot(v.astype(BF16), k_dec)
        st_ref[h] = st_new

        o = _rms(o, gh_ref[:, sl])
        o = o * (g * jax.nn.sigmoid(g))
        o_ref[:, sl] = o.astype(BF16)

        @pl.when(c == pl.num_programs(2) - 1)
        def _():
            sout_ref[h] = st_new.T


def _hgrn(z3, lb_param, g_head, state, layer, hb=HB_MIX):
    assert SUB == 8
    b, s, _ = z3.shape
    t = min(CHUNK, s)
    nc = s // t
    ng = N_MIX_HEADS // hb
    w = hb * HEAD_DIM
    depth = lb_param.shape[0]
    has_state = state is not None

    def zspec(seg):
        return pl.BlockSpec((None, t, w), lambda i, g, c, seg=seg: (i, c, seg * ng + g))

    in_specs = [zspec(0), zspec(1), zspec(2), zspec(3),
                pl.BlockSpec((depth, w), lambda i, g, c: (0, g)),
                pl.BlockSpec((1, w), lambda i, g, c: (0, g))]
    args = [z3, z3, z3, z3, lb_param, g_head.reshape(1, MIX_W)]
    if has_state:
        in_specs.append(pl.BlockSpec((None, hb, HEAD_DIM, HEAD_DIM), lambda i, g, c: (i, g, 0, 0)))
        args.append(state)
    return pl.pallas_call(
        functools.partial(_hgrn_kernel, hb=hb, t=t, layer=layer, has_state=has_state),
        grid=(b, ng, nc),
        in_specs=in_specs,
        out_specs=[
            pl.BlockSpec((None, t, w), lambda i, g, c: (i, c, g)),
            pl.BlockSpec((None, hb, HEAD_DIM, HEAD_DIM), lambda i, g, c: (i, g, 0, 0)),
        ],
        out_shape=[
            jax.ShapeDtypeStruct((b, s, MIX_W), BF16),
            jax.ShapeDtypeStruct((b, N_MIX_HEADS, HEAD_DIM, HEAD_DIM), F32),
        ],
        scratch_shapes=[pltpu.VMEM((hb, HEAD_DIM, HEAD_DIM), F32)],
        compiler_params=_cparams(("parallel", "parallel", "arbitrary")),
        name="hgrn2",
    )(*args)


def _out_proj_kernel(om_ref, oe_ref, wm_ref, we_ref, x_ref, g_ref, x1_ref, xnt_ref):
    x1 = x_ref[...] + jnp.dot(om_ref[...], wm_ref[...], preferred_element_type=F32)
    x1 = x1 + jnp.dot(oe_ref[...], we_ref[...], preferred_element_type=F32)
    x1_ref[...] = x1
    xnt_ref[...] = _rms(x1, g_ref[...]).T.astype(BF16)


def _out_proj(o_mix, o_mem, w_out, x, g2):
    n, d = x.shape
    tm = min(256, n)
    return pl.pallas_call(
        _out_proj_kernel,
        grid=(n // tm,),
        in_specs=[
            pl.BlockSpec((tm, MIX_W), lambda i: (i, 0)),
            pl.BlockSpec((tm, MEM_W), lambda i: (i, 0)),
            pl.BlockSpec((MIX_W, d), lambda i: (0, 0)),
            pl.BlockSpec((MEM_W, d), lambda i: (MIX_W // MEM_W, 0)),
            pl.BlockSpec((tm, d), lambda i: (i, 0)),
            pl.BlockSpec((1, d), lambda i: (0, 0)),
        ],
        out_specs=[
            pl.BlockSpec((tm, d), lambda i: (i, 0)),
            pl.BlockSpec((d, tm), lambda i: (0, i)),
        ],
        out_shape=[jax.ShapeDtypeStruct((n, d), F32), jax.ShapeDtypeStruct((d, n), BF16)],
        compiler_params=_cparams(("parallel",)),
        name="out_proj",
    )(o_mix, o_mem, w_out, w_out, x, g2.reshape(1, d))


_REMOVED = 2.0 ** 100


def _extract_exact(vals, order, rounds):
    rank = jnp.full(vals.shape, float(PEER_TOPK), F32)
    tops = []
    for r in range(rounds):
        m = jnp.max(vals, axis=0, keepdims=True)
        pick = jnp.min(jnp.where(vals == m, order, 1e9), axis=0, keepdims=True)
        sel = order == pick
        rank = jnp.where(sel, float(r), rank)
        vals = jnp.where(sel, -jnp.inf, vals)
        tops.append(m)
    return jnp.concatenate(tops, axis=0), rank


def _extract_fast(vals, rounds):
    tops = []
    for r in range(rounds):
        m = jnp.max(vals, axis=0, keepdims=True)
        vals = jnp.where(vals == m, -_REMOVED * (r + 1), vals)
        tops.append(m)
    rank = jnp.where(vals <= -_REMOVED, vals * (-1.0 / _REMOVED) - 1.0, float(PEER_TOPK))
    return jnp.concatenate(tops, axis=0), rank


def _route_lanes(s1, s2, exact):
    lanes = s1.shape[1]
    if exact:
        key_id = lax.broadcasted_iota(jnp.int32, (N_KEYS, lanes), 0).astype(F32)
        top1, rank1 = _extract_exact(s1, key_id, PEER_TOPK)
        top2, rank2 = _extract_exact(s2, key_id, PEER_TOPK)
    else:
        top1, rank1 = _extract_fast(s1, PEER_TOPK)
        top2, rank2 = _extract_fast(s2, PEER_TOPK)

    blk_a = top1[0:1] + top2
    blk_b = [top1[a:a + 1] + top2[0:8] for a in range(1, 8)]
    blk_c = top1[8:16] + top2[0:1]
    cand = jnp.concatenate([blk_a] + blk_b + [blk_c], axis=0)
    nrow = cand.shape[0]
    r = lax.broadcasted_iota(jnp.int32, (nrow, lanes), 0)
    a_mid = 1 + jnp.right_shift(r - 16, 3)
    b_mid = jnp.bitwise_and(r - 16, 7)
    valid = jnp.logical_or(jnp.logical_or(r < 16, r >= 72), (a_mid + 1) * (b_mid + 1) <= PEER_TOPK)
    cand = jnp.where(valid, cand, -jnp.inf)
    if exact:
        cand_id = jnp.where(r < 16, r, jnp.where(r < 72, a_mid * PEER_TOPK + b_mid, (r - 64) * PEER_TOPK))
        best, crank = _extract_exact(cand, cand_id.astype(F32), PEER_TOPK)
    else:
        best, crank = _extract_fast(cand, PEER_TOPK)
    chosen = jnp.logical_and(valid, crank < PEER_TOPK).astype(F32)
    z = jnp.sum(jnp.exp(best - best[0:1]), axis=0, keepdims=True)

    counts = [jnp.sum(chosen[0:16], axis=0, keepdims=True)]
    counts += [jnp.sum(chosen[16 + 8 * (a - 1):16 + 8 * a], axis=0, keepdims=True) for a in range(1, 8)]
    counts += [chosen[72 + a - 8:72 + a - 7] for a in range(8, 16)]
    c1 = jnp.zeros((N_KEYS, lanes), F32)
    for a in range(PEER_TOPK):
        c1 = jnp.where(rank1 == a, counts[a], c1)

    ranked = (jnp.sum((rank1 < PEER_TOPK).astype(F32), axis=0, keepdims=True)
              + jnp.sum((rank2 < PEER_TOPK).astype(F32), axis=0, keepdims=True)
              + jnp.sum(chosen, axis=0, keepdims=True))
    low = jnp.minimum(jnp.min(s1, axis=0, keepdims=True), jnp.min(s2, axis=0, keepdims=True))
    redo = jnp.logical_or(ranked != 3.0 * PEER_TOPK, jnp.logical_not(low > -0.25 * _REMOVED)).astype(F32)
    e1 = jnp.exp(s1 - top1[0:1])
    e2 = jnp.exp(s2 - top2[0:1]) / z
    return rank2, e2, c1, e1, redo


def _scores_kernel(xnt_ref, wq_ref, keys_ref, s_ref):
    qq = jnp.dot(wq_ref[...], xnt_ref[...], preferred_element_type=F32).astype(BF16)
    for h in range(PEER_HEADS):
        for c in range(2):
            row = (2 * h + c) * N_KEYS
            s_ref[2 * h + c] = jnp.dot(keys_ref[c, h], qq[row:row + N_KEYS], preferred_element_type=F32)


def _peer_scores(xnt, w_pq_t, keys):
    d, n = xnt.shape
    tb = min(512, n)
    return pl.pallas_call(
        _scores_kernel,
        grid=(n // tb,),
        in_specs=[
            pl.BlockSpec((d, tb), lambda i: (0, i)),
            pl.BlockSpec(w_pq_t.shape, lambda i: (0, 0)),
            pl.BlockSpec(keys.shape, lambda i: (0, 0, 0, 0)),
        ],
        out_specs=pl.BlockSpec((2 * PEER_HEADS, N_KEYS, tb), lambda i: (0, 0, i)),
        out_shape=jax.ShapeDtypeStruct((2 * PEER_HEADS, N_KEYS, n), F32),
        compiler_params=_cparams(("parallel",)),
        name="peer_scores",
    )(xnt, w_pq_t, keys)


def _route_kernel(s_ref, r2_ref, e2_ref, c1_ref, e1_ref):
    def emit(exact):
        rank2, e2, c1, e1, redo = _route_lanes(s_ref[0], s_ref[1], exact)
        r2_ref[...] = rank2.astype(BF16)
        e2_ref[...] = e2.astype(BF16)
        c1_ref[...] = c1
        e1_ref[...] = e1
        return redo

    redo = emit(exact=False)

    @pl.when(jnp.max(redo) > 0.0)
    def _():
        emit(exact=True)


def _peer_route(scores):
    n = scores.shape[2]
    tb = min(TB_ROUTE, n)
    out = jax.ShapeDtypeStruct((PEER_HEADS, N_KEYS, n), F32)
    out16 = jax.ShapeDtypeStruct((PEER_HEADS, N_KEYS, n), BF16)
    ospec = pl.BlockSpec((None, N_KEYS, tb), lambda i, h: (h, 0, i))
    return pl.pallas_call(
        _route_kernel,
        grid=(n // tb, PEER_HEADS),
        in_specs=[pl.BlockSpec((2, N_KEYS, tb), lambda i, h: (h, 0, i))],
        out_specs=[ospec, ospec, ospec, ospec],
        out_shape=[out16, out16, out, out],
        compiler_params=_cparams(("parallel", "parallel")),
        name="peer_route",
    )(scores)


def _gelu(x):
    return 0.5 * x * (1.0 + lax.erf(x * (0.5 ** 0.5)))


def _peer_kernel(xnt_ref, u_ref, vt_ref, r2_ref, e2_ref, c1_ref, e1_ref, y_ref, a_ref, coef_ref, *, eb, ck, lc):
    e = pl.program_id(1)
    tb = xnt_ref.shape[1]
    per = ck // N_KEYS

    @pl.when(e == 0)
    def _():
        y_ref[...] = jnp.zeros_like(y_ref)

    nck = eb // ck

    def pre_act(k):
        a = jnp.dot(u_ref[k * ck:(k + 1) * ck, :], xnt_ref[...], preferred_element_type=F32)
        a_ref[k] = a
        return a

    a_next = pre_act(0)
    for k in range(nck):
        zero = None
        if k + 1 < nck:
            a_next = pre_act(k + 1)
            bits = pltpu.bitcast(a_next[0:1, tb - lc:], jnp.uint32)
            bits = lax.shift_right_logical(lax.shift_right_logical(bits, jnp.uint32(16)), jnp.uint32(16))
            zero = pltpu.bitcast(bits, F32)
        for ii in range(per):
            i = k * per + ii
            rows = slice(ii * N_KEYS, (ii + 1) * N_KEYS)
            for l0 in range(0, tb, lc):
                cols = slice(l0, l0 + lc)
                gate = jnp.zeros((N_KEYS, lc), BF16)
                for h in range(PEER_HEADS):
                    c1 = jnp.broadcast_to(c1_ref[h, i:i + 1, cols].astype(BF16), (N_KEYS, lc))
                    e1 = jnp.broadcast_to(e1_ref[h, i:i + 1, cols].astype(BF16), (N_KEYS, lc))
                    zero16 = jnp.zeros((N_KEYS, lc), BF16)
                    gate = gate + jnp.where(r2_ref[h, :, cols] < c1, e2_ref[h, :, cols], zero16) * e1
                act = gate.astype(F32) * _gelu(a_ref[k, rows, cols])
                if zero is not None and ii == per - 1 and l0 + lc == tb:
                    act = act + zero
                coef_ref[k, rows, cols] = act.astype(BF16)
        y_ref[...] += jnp.dot(vt_ref[:, k * ck:(k + 1) * ck], coef_ref[k], preferred_element_type=F32)


def _peer_sweep(xnt, u, vt, r2, e2, c1, e1):
    d, n = xnt.shape
    tb = min(TB_PEER, n)
    eb = EB_PEER
    ck = CK_PEER
    lc = min(LC_PEER, tb)
    rspec = pl.BlockSpec((PEER_HEADS, N_KEYS, tb), lambda t, e: (0, 0, t))
    ispec = pl.BlockSpec((PEER_HEADS, eb // N_KEYS, tb), lambda t, e: (0, e, t))
    return pl.pallas_call(
        functools.partial(_peer_kernel, eb=eb, ck=ck, lc=lc),
        grid=(n // tb, N_EXPERTS // eb),
        in_specs=[
            pl.BlockSpec((d, tb), lambda t, e: (0, t)),
            pl.BlockSpec((eb, d), lambda t, e: (e, 0)),
            pl.BlockSpec((d, eb), lambda t, e: (0, e)),
            rspec, rspec, ispec, ispec,
        ],
        out_specs=pl.BlockSpec((d, tb), lambda t, e: (0, t)),
        out_shape=jax.ShapeDtypeStruct((d, n), F32),
        scratch_shapes=[pltpu.VMEM((eb // ck, ck, tb), F32), pltpu.VMEM((eb // ck, ck, tb), BF16)],
        compiler_params=_cparams(("parallel", "arbitrary")),
        name="peer_sweep",
    )(xnt, u, vt, r2, e2, c1, e1)


def _add_t_kernel(x_ref, yt_ref, g_ref, o_ref, *, final):
    x = x_ref[...] + yt_ref[...].T
    if final:
        x = _rms(x, g_ref[...])
    o_ref[...] = x


def _add_transposed(x1, yt, g, final):
    n, d = x1.shape
    tm = min(256, n)
    return pl.pallas_call(
        functools.partial(_add_t_kernel, final=final),
        grid=(n // tm,),
        in_specs=[
            pl.BlockSpec((tm, d), lambda i: (i, 0)),
            pl.BlockSpec((d, tm), lambda i: (0, i)),
            pl.BlockSpec((1, d), lambda i: (0, 0)),
        ],
        out_specs=pl.BlockSpec((tm, d), lambda i: (i, 0)),
        out_shape=jax.ShapeDtypeStruct((n, d), F32),
        compiler_params=_cparams(("parallel",)),
        name="peer_residual",
    )(x1, yt, g.reshape(1, d))


def _trunk(x, mem, k_cache, v_cache, s_cache, mk_cache, mv_cache, p):
    prompt = mem is not None
    bsz, t, d = x.shape
    n = bsz * t
    depth = p["g_norm1"].shape[0]
    x2 = x.reshape(n, d)
    new_k, new_v, new_s, new_mk, new_mv = [], [], [], [], []
    for i in range(depth):
        j = i // 2
        attn = i % 2 == 0
        w_in = p["w_in_a"][j] if attn else p["w_in_b"][j]
        w_out = p["w_out_a"][j] if attn else p["w_out_b"][j]
        z = _norm_matmul(x2, p["g_norm1"][i], w_in)
        z3 = z.reshape(bsz, t, z.shape[1])
        mq_blk = (z.shape[1] - MEM_W) // HEAD_DIM

        if prompt:
            mkv = _norm_matmul(mem.reshape(bsz * N_MEM, d), p["g_mem"][i], p["w_mem_kv"][i])
            new_mk.append(mkv[:, :MEM_W].reshape(bsz, N_MEM, N_MEM_HEADS, HEAD_DIM))
            new_mv.append(mkv[:, MEM_W:].reshape(bsz, N_MEM, N_MEM_HEADS, HEAD_DIM))
            mkv3 = mkv.reshape(bsz, N_MEM, 2 * MEM_W)
            o_mem = _mem_attention(z3, mq_blk, mkv3, 0, mkv3, N_MEM_HEADS)
        else:
            o_mem = _mem_attention(z3, mq_blk, mk_cache[i].reshape(bsz, N_MEM, MEM_W), 0,
                                   mv_cache[i].reshape(bsz, N_MEM, MEM_W), 0)

        if attn:
            k_new = z3[:, :, MIX_W:2 * MIX_W]
            v_new = z3[:, :, 2 * MIX_W:3 * MIX_W]
            if prompt:
                bias = _band_bias(p["rel_bias"][j], CHUNK, WINDOW + CHUNK, WINDOW)
                kv_pad = jnp.pad(z3[:, :, MIX_W:3 * MIX_W], ((0, 0), (WINDOW, 0), (0, 0)))
                o_mix = _band_attention_prompt(z3, kv_pad, bias)
                keep = min(WINDOW, t)
                k_new, v_new = k_new[:, t - keep:], v_new[:, t - keep:]
            else:
                p_len = k_cache.shape[2]
                bias = _band_bias(p["rel_bias"][j], t, p_len + t, p_len)
                o_mix = _band_attention_sample(z3, k_cache[j].reshape(bsz, p_len, MIX_W),
                                               v_cache[j].reshape(bsz, p_len, MIX_W), bias)
            new_k.append(k_new.reshape(bsz, -1, N_MIX_HEADS, HEAD_DIM))
            new_v.append(v_new.reshape(bsz, -1, N_MIX_HEADS, HEAD_DIM))
        else:
            o_mix, s_new = _hgrn(z3, p["lb_param"], p["g_hgrn"][j], None if prompt else s_cache[j], i)
            new_s.append(s_new)

        x1, xnt = _out_proj(o_mix.reshape(n, MIX_W), o_mem.reshape(n, MEM_W), w_out, x2, p["g_norm2"][i])
        r2, e2, c1, e1 = _peer_route(_peer_scores(xnt, p["w_pq_t"][i], p["peer_keys"][i]))
        yt = _peer_sweep(xnt, p["peer_u"][i], p["peer_v_t"][i], r2, e2, c1, e1)
        x2 = _add_transposed(x1, yt, p["g_final"], final=(i == depth - 1))

    y = x2.reshape(bsz, t, d)
    mk_out = jnp.stack(new_mk) if new_mk else None
    mv_out = jnp.stack(new_mv) if new_mv else None
    return y, jnp.stack(new_k), jnp.stack(new_v), jnp.stack(new_s), mk_out, mv_out


def kernel(x_prompt, x_sample, cache_attn_k, cache_attn_v, state_hgrn, cache_mem_k, cache_mem_v, mem_prompt,
           g_norm1, g_norm2, g_mem, g_final, w_in_a, w_out_a, rel_bias, w_in_b, w_out_b, lb_param, g_hgrn,
           w_mem_kv, w_pq, peer_keys, peer_u, peer_v):
    p = dict(
        g_norm1=g_norm1, g_norm2=g_norm2, g_mem=g_mem, g_final=g_final, rel_bias=rel_bias,
        lb_param=lb_param, g_hgrn=g_hgrn,
        w_in_a=w_in_a.astype(BF16), w_out_a=w_out_a.astype(BF16),
        w_in_b=w_in_b.astype(BF16), w_out_b=w_out_b.astype(BF16),
        w_mem_kv=w_mem_kv.astype(BF16),
        w_pq_t=jnp.swapaxes(w_pq, 1, 2).astype(BF16),
        peer_keys=peer_keys.astype(BF16),
        peer_u=peer_u.astype(BF16),
        peer_v_t=jnp.swapaxes(peer_v, 1, 2).astype(BF16),
    )
    y_p, k_p, v_p, s_p, mk_p, mv_p = _trunk(x_prompt, mem_prompt, None, None, None, None, None, p)
    y_s, k_s, v_s, s_s, _, _ = _trunk(x_sample, None, cache_attn_k, cache_attn_v, state_hgrn,
                                      cache_mem_k, cache_mem_v, p)
    return (y_p, y_s, k_p, v_p, k_s, v_s, s_p, s_s, mk_p, mv_p)
```

```python
import functools
import math

import jax
import jax.numpy as jnp
from jax import lax
from jax.experimental import pallas as pl
from jax.experimental.pallas import tpu as pltpu

F32 = jnp.float32
BF16 = jnp.bfloat16

D_MODEL = 2048
PAST_LEN = 1024
CHUNK = 64
LEFT_CHUNKS = 8
WINDOW = LEFT_CHUNKS * CHUNK
HEAD_DIM = 128
N_MIX_HEADS = 12
N_MEM_HEADS = 4
MIX_W = N_MIX_HEADS * HEAD_DIM
MEM_W = N_MEM_HEADS * HEAD_DIM
N_MEM = 256
REL_CLIP = 256
PEER_HEADS = 8
N_KEYS = 128
N_EXPERTS = N_KEYS * N_KEYS
PEER_TOPK = 16
EPS = 1e-6
NEG_INF = -1e30

LANES = 128
SUBLANES = 8
SUB = SUBLANES
VMEM_LIMIT = 56 * 1024 * 1024

TM_PROJ = 1024
TN_PROJ = 512
TB_ROUTE = 256
TB_PEER = 512
EB_PEER = 1024
CK_PEER = 256
LC_PEER = 256
HB_MIX = 4
QB_ATTN = 256
HB_HGRN = 12


def _cparams(sem, flags=None):
    return pltpu.CompilerParams(dimension_semantics=sem, vmem_limit_bytes=VMEM_LIMIT, flags=flags)


def _rms(x, g):
    return x * lax.rsqrt(jnp.mean(x * x, axis=-1, keepdims=True) + EPS) * g


def _norm_matmul_kernel(x_ref, g_ref, w_ref, o_ref, hn_ref, *, blocks_per_seq, pad_blocks):
    live = lax.rem(pl.program_id(0), blocks_per_seq) >= pad_blocks

    @pl.when(jnp.logical_and(live, pl.program_id(1) == 0))
    def _():
        hn_ref[...] = _rms(x_ref[...], g_ref[...]).astype(BF16)

    @pl.when(live)
    def _():
        o_ref[...] = jnp.dot(hn_ref[...], w_ref[...], preferred_element_type=F32)

    @pl.when(jnp.logical_not(live))
    def _():
        o_ref[...] = jnp.zeros_like(o_ref)


def _norm_matmul(x, g, w, layer, seq=None, pad=0):
    m, d = x.shape
    n = w.shape[2]
    tm = min(TM_PROJ, m) if pad == 0 else math.gcd(pad, seq)
    tn = min(TN_PROJ, n)
    seq = m if pad == 0 else seq
    pad_blocks = pad // tm
    data_blocks = seq // tm
    blocks_per_seq = data_blocks + pad_blocks
    n_seq = m // seq

    def x_map(i, j):
        return ((i // blocks_per_seq) * data_blocks + jnp.maximum(lax.rem(i, blocks_per_seq) - pad_blocks, 0), 0)

    return pl.pallas_call(
        functools.partial(_norm_matmul_kernel, blocks_per_seq=blocks_per_seq, pad_blocks=pad_blocks),
        grid=(n_seq * blocks_per_seq, n // tn),
        in_specs=[
            pl.BlockSpec((tm, d), x_map),
            pl.BlockSpec((1, d), lambda i, j: (0, 0)),
            pl.BlockSpec((None, d, tn), lambda i, j: (layer, 0, j)),
        ],
        out_specs=pl.BlockSpec((tm, tn), lambda i, j: (i, j)),
        out_shape=jax.ShapeDtypeStruct((n_seq * blocks_per_seq * tm, n), F32),
        scratch_shapes=[pltpu.VMEM((tm, d), BF16)],
        compiler_params=_cparams(("parallel", "arbitrary")),
        name="norm_matmul",
    )(x, g.reshape(1, d), w)


def _bias_kernel(tab_ref, o_ref, *, tq, tk, off):
    h = pl.program_id(0)
    qi = lax.broadcasted_iota(jnp.int32, (tq, tk), 0)
    ki = lax.broadcasted_iota(jnp.int32, (tq, tk), 1)
    rel = jnp.clip(qi - ki + off, -REL_CLIP, REL_CLIP) + REL_CLIP
    lo = max(min(-(tk - 1) + off, REL_CLIP), -REL_CLIP) + REL_CLIP
    hi = max(min((tq - 1) + off, REL_CLIP), -REL_CLIP) + REL_CLIP

    def body(r, acc):
        return jnp.where(rel == r, tab_ref[h, r], acc)

    o_ref[0] = lax.fori_loop(lo, hi + 1, body, jnp.zeros((tq, tk), F32))


def _band_bias(table, tq, tk, off):
    nh = table.shape[1]
    return pl.pallas_call(
        functools.partial(_bias_kernel, tq=tq, tk=tk, off=off),
        grid=(nh,),
        in_specs=[pl.BlockSpec(memory_space=pltpu.SMEM)],
        out_specs=pl.BlockSpec((1, tq, tk), lambda h: (h, 0, 0)),
        out_shape=jax.ShapeDtypeStruct((nh, tq, tk), F32),
        compiler_params=_cparams(("arbitrary",)),
        name="band_bias",
    )(table.T)


def _softmax_rows(s):
    m = jnp.max(s, axis=-1, keepdims=True)
    p = jnp.exp(s - m)
    return p / jnp.sum(p, axis=-1, keepdims=True)


def _nt_dot(a, b):
    return lax.dot_general(a, b, (((1,), (1,)), ((), ())), preferred_element_type=F32)


def _tn_dot(a, b):
    return lax.dot_general(a, b, (((0,), (0,)), ((), ())), preferred_element_type=F32)


def _band_prompt_kernel(q_ref, k_ref, v_ref, bias_ref, o_ref, *, hb, qb):
    c = pl.program_id(2)
    start = pl.multiple_of(c * qb, qb)
    tk = WINDOW + qb
    k_pos = lax.broadcasted_iota(jnp.int32, (qb, tk), 1) + (c * qb - WINDOW)
    live = k_pos >= 0
    scale = HEAD_DIM ** -0.5
    for h in range(hb):
        sl = slice(h * HEAD_DIM, (h + 1) * HEAD_DIM)
        q = q_ref[:, sl].astype(BF16)
        kw = k_ref[pl.ds(start, tk), sl].astype(BF16)
        vw = v_ref[pl.ds(start, tk), sl].astype(BF16)
        s = _nt_dot(q, kw) * scale + bias_ref[h]
        s = jnp.where(live, s, NEG_INF)
        p = _softmax_rows(s).astype(BF16)
        o_ref[:, sl] = jnp.dot(p, vw, preferred_element_type=F32).astype(BF16)


def _band_attention_prompt(zp3, bias, hb=HB_MIX):
    b, sp, _ = zp3.shape
    s = sp - WINDOW
    qb = min(QB_ATTN, s)
    nqc = qb // CHUNK
    ng = N_MIX_HEADS // hb
    w = hb * HEAD_DIM
    tk = WINDOW + qb
    bias_blk = jnp.concatenate(
        [jnp.pad(bias, ((0, 0), (0, 0), (j * CHUNK, (nqc - 1 - j) * CHUNK)), constant_values=NEG_INF)
         for j in range(nqc)], axis=1)
    return pl.pallas_call(
        functools.partial(_band_prompt_kernel, hb=hb, qb=qb),
        grid=(b, ng, s // qb),
        in_specs=[
            pl.BlockSpec((None, qb, w), lambda i, g, c: (i, c + WINDOW // qb, g)),
            pl.BlockSpec((None, sp, w), lambda i, g, c: (i, 0, ng + g)),
            pl.BlockSpec((None, sp, w), lambda i, g, c: (i, 0, 2 * ng + g)),
            pl.BlockSpec((hb, qb, tk), lambda i, g, c: (g, 0, 0)),
        ],
        out_specs=pl.BlockSpec((None, qb, w), lambda i, g, c: (i, c, g)),
        out_shape=jax.ShapeDtypeStruct((b, s, MIX_W), BF16),
        compiler_params=_cparams(("parallel", "parallel", "arbitrary")),
        name="band_attn_prompt",
    )(zp3, zp3, zp3, bias_blk)


def _band_sample_kernel(*refs, hb, p_len):
    q_ref, kn_ref, vn_ref = refs[:3]
    kc_refs = refs[3:3 + hb]
    vc_refs = refs[3 + hb:3 + 2 * hb]
    bias_ref, o_ref = refs[3 + 2 * hb:]
    scale = HEAD_DIM ** -0.5
    for h in range(hb):
        sl = slice(h * HEAD_DIM, (h + 1) * HEAD_DIM)
        q = q_ref[:, sl].astype(BF16)
        s_c = _nt_dot(q, kc_refs[h][...].astype(BF16)) * scale + bias_ref[h, :, :p_len]
        s_n = _nt_dot(q, kn_ref[:, sl].astype(BF16)) * scale + bias_ref[h, :, p_len:]
        m = jnp.maximum(jnp.max(s_c, axis=-1, keepdims=True), jnp.max(s_n, axis=-1, keepdims=True))
        e_c = jnp.exp(s_c - m)
        e_n = jnp.exp(s_n - m)
        l = jnp.sum(e_c, axis=-1, keepdims=True) + jnp.sum(e_n, axis=-1, keepdims=True)
        o = jnp.dot((e_c / l).astype(BF16), vc_refs[h][...].astype(BF16), preferred_element_type=F32)
        o = o + jnp.dot((e_n / l).astype(BF16), vn_ref[:, sl].astype(BF16), preferred_element_type=F32)
        o_ref[:, sl] = o.astype(BF16)


def _band_attention_sample(z3, k_cache, v_cache, layer, bias, hb=N_MIX_HEADS):
    b, t, _ = z3.shape
    p_len = k_cache.shape[2]
    ng = N_MIX_HEADS // hb
    w = hb * HEAD_DIM
    k_cache = jnp.transpose(k_cache, (0, 1, 3, 2, 4))
    v_cache = jnp.transpose(v_cache, (0, 1, 3, 2, 4))

    def cache_spec(h):
        return pl.BlockSpec((None, None, None, p_len, HEAD_DIM), lambda i, g, h=h: (layer, i, g * hb + h, 0, 0))

    cache_specs = [cache_spec(h) for h in range(hb)]
    return pl.pallas_call(
        functools.partial(_band_sample_kernel, hb=hb, p_len=p_len),
        grid=(b, ng),
        in_specs=[
            pl.BlockSpec((None, t, w), lambda i, g: (i, 0, g)),
            pl.BlockSpec((None, t, w), lambda i, g: (i, 0, ng + g)),
            pl.BlockSpec((None, t, w), lambda i, g: (i, 0, 2 * ng + g)),
            *cache_specs, *cache_specs,
            pl.BlockSpec((hb, t, p_len + t), lambda i, g: (g, 0, 0)),
        ],
        out_specs=pl.BlockSpec((None, t, w), lambda i, g: (i, 0, g)),
        out_shape=jax.ShapeDtypeStruct((b, t, MIX_W), BF16),
        compiler_params=_cparams(("parallel", "parallel")),
        name="band_attn_sample",
    )(z3, z3, z3, *([k_cache] * hb), *([v_cache] * hb), bias)


def _mem_attn_kernel(q_ref, k_ref, v_ref, o_ref, *, head_major):
    for h in range(N_MEM_HEADS):
        sl = slice(h * HEAD_DIM, (h + 1) * HEAD_DIM)
        k = k_ref[h] if head_major else k_ref[:, sl]
        v = v_ref[h] if head_major else v_ref[:, sl]
        s = _nt_dot(q_ref[:, sl].astype(BF16), k.astype(BF16)) * (HEAD_DIM ** -0.5)
        p = _softmax_rows(s).astype(BF16)
        o_ref[:, sl] = jnp.dot(p, v.astype(BF16), preferred_element_type=F32).astype(BF16)


def _mem_attention(z3, q_blk, row0, t, karr, kspec, varr, vspec, head_major):
    b = z3.shape[0]
    tq = min(t, 512)
    return pl.pallas_call(
        functools.partial(_mem_attn_kernel, head_major=head_major),
        grid=(b, t // tq),
        in_specs=[
            pl.BlockSpec((None, tq, MEM_W), lambda i, j: (i, j + row0 // tq, q_blk)),
            kspec, vspec,
        ],
        out_specs=pl.BlockSpec((None, tq, MEM_W), lambda i, j: (i, j, 0)),
        out_shape=jax.ShapeDtypeStruct((b, t, MEM_W), BF16),
        compiler_params=_cparams(("parallel", "arbitrary")),
        name="mem_attn",
    )(z3, karr, varr)


def _log_sigmoid(x):
    return jnp.minimum(x, 0.0) - jnp.log1p(jnp.exp(-jnp.abs(x)))


def _logaddexp(a, b):
    hi = jnp.maximum(a, b)
    lo = jnp.minimum(a, b)
    return jnp.where(hi == -jnp.inf, hi, hi + jnp.log1p(jnp.exp(lo - hi)))


def _hgrn_kernel(*refs, hb, t, layer, has_state):
    if has_state:
        q_ref, f_ref, i_ref, g_ref, lbp_ref, gh_ref, s0_ref, o_ref, sout_ref, st_ref = refs
    else:
        q_ref, f_ref, i_ref, g_ref, lbp_ref, gh_ref, o_ref, sout_ref, st_ref = refs
    c = pl.program_id(2)
    nsub = t // SUB

    @pl.when(c == 0)
    def _():
        for h in range(hb):
            if has_state:
                st_ref[h] = s0_ref[h].T
            else:
                st_ref[h] = jnp.zeros((HEAD_DIM, HEAD_DIM), F32)

    row = lax.broadcasted_iota(jnp.int32, (t, t), 0)
    col = lax.broadcasted_iota(jnp.int32, (t, t), 1)
    rowd = lax.broadcasted_iota(jnp.int32, (t, HEAD_DIM), 0)

    for h in range(hb):
        sl = slice(h * HEAD_DIM, (h + 1) * HEAD_DIM)
        q = q_ref[:, sl]
        ff = f_ref[:, sl]
        v = i_ref[:, sl]
        g = g_ref[:, sl]

        lbp = lbp_ref[:, sl]
        e = jnp.exp(lbp - jnp.max(lbp, axis=0, keepdims=True))
        soft = e / jnp.sum(e, axis=0, keepdims=True)
        acc = soft[0:1]
        for r in range(1, layer + 1):
            acc = acc + soft[r:r + 1]
        lb = acc - soft[0:1]

        log_f = _logaddexp(jnp.log(lb), jnp.log1p(-lb) + _log_sigmoid(ff))
        k = (1.0 - lb) * jax.nn.sigmoid(-ff)

        b = log_f
        sh = 1
        while sh < t:
            b = b + jnp.where(rowd >= sh, pltpu.roll(b, sh, 0), 0.0)
            sh *= 2
        b_end = b[t - 1:t]

        st = st_ref[h]
        o = _nt_dot((q * jnp.exp(b)).astype(BF16), st.astype(BF16))

        sc = jnp.zeros((t, t), F32)
        for dl in range(SUB):
            if dl == 0:
                prod = q * k
            else:
                prod = q * pltpu.roll(k, dl, 0) * jnp.exp(jnp.minimum(b - pltpu.roll(b, dl, 0), 0.0))
            r = jnp.sum(prod, axis=1, keepdims=True)
            hit = jnp.logical_and(row - col == dl, jnp.bitwise_and(row, SUB - 1) >= dl)
            sc = jnp.where(hit, r, sc)

        q_parts = []
        k_rows = []
        for i in range(nsub):
            b_ref = b[i * SUB + SUB - 1:i * SUB + SUB]
            q_parts.append((q * jnp.exp(jnp.minimum(b - b_ref, 0.0))).astype(BF16))
            kh = k[i * SUB:(i + 1) * SUB] * jnp.exp(jnp.minimum(b_ref - b[i * SUB:(i + 1) * SUB], 0.0))
            zero = jnp.zeros_like(kh)
            k_rows.append(jnp.concatenate([kh if j == i else zero for j in range(nsub)], axis=1))
        q_cat = jnp.concatenate(q_parts, axis=1)
        k_bd = jnp.concatenate(k_rows, axis=0).astype(BF16)
        sc_far = _nt_dot(q_cat, k_bd)
        far = jnp.right_shift(col, 3) < jnp.right_shift(row, 3)
        scores = jnp.where(far, sc_far, sc)

        o = o + jnp.dot(scores.astype(BF16), v.astype(BF16), preferred_element_type=F32)

        k_dec = (k * jnp.exp(jnp.minimum(b_end - b, 0.0))).astype(BF16)
        st_new = st * jnp.exp(b_end) + _tn_dot(v.astype(BF16), k_dec)
        st_ref[h] = st_new

        o = _rms(o, gh_ref[:, sl])
        o = o * (g * jax.nn.sigmoid(g))
        o_ref[:, sl] = o.astype(BF16)

    @pl.when(c == pl.num_programs(2) - 1)
    def _():
        for h in range(hb):
            sout_ref[h] = st_ref[h].T


def _hgrn(z3, lb_param, g_head, state, state_idx, layer, hb=HB_HGRN):
    assert SUB == 8
    b, s, _ = z3.shape
    t = min(CHUNK, s)
    nc = s // t
    ng = N_MIX_HEADS // hb
    w = hb * HEAD_DIM
    depth = lb_param.shape[0]
    has_state = state is not None

    def zspec(seg):
        return pl.BlockSpec((None, t, w), lambda i, g, c, seg=seg: (i, c, seg * ng + g))

    in_specs = [zspec(0), zspec(1), zspec(2), zspec(3),
                pl.BlockSpec((depth, w), lambda i, g, c: (0, g)),
                pl.BlockSpec((1, w), lambda i, g, c: (0, g))]
    args = [z3, z3, z3, z3, lb_param, g_head.reshape(1, MIX_W)]
    if has_state:
        in_specs.append(pl.BlockSpec((None, None, hb, HEAD_DIM, HEAD_DIM),
                                     lambda i, g, c: (state_idx, i, g, 0, 0)))
        args.append(state)
    return pl.pallas_call(
        functools.partial(_hgrn_kernel, hb=hb, t=t, layer=layer, has_state=has_state),
        grid=(b, ng, nc),
        in_specs=in_specs,
        out_specs=[
            pl.BlockSpec((None, t, w), lambda i, g, c: (i, c, g)),
            pl.BlockSpec((None, hb, HEAD_DIM, HEAD_DIM), lambda i, g, c: (i, g, 0, 0)),
        ],
        out_shape=[
            jax.ShapeDtypeStruct((b, s, MIX_W), BF16),
            jax.ShapeDtypeStruct((b, N_MIX_HEADS, HEAD_DIM, HEAD_DIM), F32),
        ],
        scratch_shapes=[pltpu.VMEM((hb, HEAD_DIM, HEAD_DIM), F32)],
        compiler_params=_cparams(("parallel", "parallel", "arbitrary")),
        name="hgrn2",
    )(*args)


def _out_proj_kernel(om_ref, oe_ref, wm_ref, we_ref, x_ref, g_ref, x1_ref, xnt_ref):
    x1 = x_ref[...] + jnp.dot(om_ref[...], wm_ref[...], preferred_element_type=F32)
    x1 = x1 + jnp.dot(oe_ref[...], we_ref[...], preferred_element_type=F32)
    x1_ref[...] = x1
    xnt_ref[...] = _rms(x1, g_ref[...]).T.astype(BF16)


def _out_proj(o_mix, o_mem, w_out, layer, x, g2):
    n, d = x.shape
    tm = min(256, n)
    return pl.pallas_call(
        _out_proj_kernel,
        grid=(n // tm,),
        in_specs=[
            pl.BlockSpec((tm, MIX_W), lambda i: (i, 0)),
            pl.BlockSpec((tm, MEM_W), lambda i: (i, 0)),
            pl.BlockSpec((None, MIX_W, d), lambda i: (layer, 0, 0)),
            pl.BlockSpec((None, MEM_W, d), lambda i: (layer, MIX_W // MEM_W, 0)),
            pl.BlockSpec((tm, d), lambda i: (i, 0)),
            pl.BlockSpec((1, d), lambda i: (0, 0)),
        ],
        out_specs=[
            pl.BlockSpec((tm, d), lambda i: (i, 0)),
            pl.BlockSpec((d, tm), lambda i: (0, i)),
        ],
        out_shape=[jax.ShapeDtypeStruct((n, d), F32), jax.ShapeDtypeStruct((d, n), BF16)],
        compiler_params=_cparams(("parallel",)),
        name="out_proj",
    )(o_mix, o_mem, w_out, w_out, x, g2.reshape(1, d))


_REMOVED = 2.0 ** 100


def _extract_exact(vals, order, rounds):
    rank = jnp.full(vals.shape, float(PEER_TOPK), F32)
    tops = []
    for r in range(rounds):
        m = jnp.max(vals, axis=0, keepdims=True)
        pick = jnp.min(jnp.where(vals == m, order, 1e9), axis=0, keepdims=True)
        sel = order == pick
        rank = jnp.where(sel, float(r), rank)
        vals = jnp.where(sel, -jnp.inf, vals)
        tops.append(m)
    return jnp.concatenate(tops, axis=0), rank


def _extract_fast(vals, rounds):
    tops = []
    for r in range(rounds):
        m = jnp.max(vals, axis=0, keepdims=True)
        vals = jnp.where(vals == m, -_REMOVED * (r + 1), vals)
        tops.append(m)
    rank = jnp.where(vals <= -_REMOVED, vals * (-1.0 / _REMOVED) - 1.0, float(PEER_TOPK))
    return jnp.concatenate(tops, axis=0), rank


def _route_lanes(s1, s2, exact):
    lanes = s1.shape[1]
    if exact:
        key_id = lax.broadcasted_iota(jnp.int32, (N_KEYS, lanes), 0).astype(F32)
        top1, rank1 = _extract_exact(s1, key_id, PEER_TOPK)
        top2, rank2 = _extract_exact(s2, key_id, PEER_TOPK)
    else:
        top1, rank1 = _extract_fast(s1, PEER_TOPK)
        top2, rank2 = _extract_fast(s2, PEER_TOPK)

    blk_a = top1[0:1] + top2
    blk_b = [top1[a:a + 1] + top2[0:8] for a in range(1, 8)]
    blk_c = top1[8:16] + top2[0:1]
    cand = jnp.concatenate([blk_a] + blk_b + [blk_c], axis=0)
    nrow = cand.shape[0]
    r = lax.broadcasted_iota(jnp.int32, (nrow, lanes), 0)
    a_mid = 1 + jnp.right_shift(r - 16, 3)
    b_mid = jnp.bitwise_and(r - 16, 7)
    valid = jnp.logical_or(jnp.logical_or(r < 16, r >= 72), (a_mid + 1) * (b_mid + 1) <= PEER_TOPK)
    cand = jnp.where(valid, cand, -jnp.inf)
    if exact:
        cand_id = jnp.where(r < 16, r, jnp.where(r < 72, a_mid * PEER_TOPK + b_mid, (r - 64) * PEER_TOPK))
        best, crank = _extract_exact(cand, cand_id.astype(F32), PEER_TOPK)
    else:
        best, crank = _extract_fast(cand, PEER_TOPK)
    chosen = jnp.logical_and(valid, crank < PEER_TOPK).astype(F32)
    z = jnp.sum(jnp.exp(best - best[0:1]), axis=0, keepdims=True)

    counts = [jnp.sum(chosen[0:16], axis=0, keepdims=True)]
    counts += [jnp.sum(chosen[16 + 8 * (a - 1):16 + 8 * a], axis=0, keepdims=True) for a in range(1, 8)]
    counts += [chosen[72 + a - 8:72 + a - 7] for a in range(8, 16)]
    c1 = jnp.zeros((N_KEYS, lanes), F32)
    for a in range(PEER_TOPK):
        c1 = jnp.where(rank1 == a, counts[a], c1)

    ranked = (jnp.sum((rank1 < PEER_TOPK).astype(F32), axis=0, keepdims=True)
              + jnp.sum((rank2 < PEER_TOPK).astype(F32), axis=0, keepdims=True)
              + jnp.sum(chosen, axis=0, keepdims=True))
    low = jnp.minimum(jnp.min(s1, axis=0, keepdims=True), jnp.min(s2, axis=0, keepdims=True))
    redo = jnp.logical_or(ranked != 3.0 * PEER_TOPK, jnp.logical_not(low > -0.25 * _REMOVED)).astype(F32)
    e1 = jnp.exp(s1 - top1[0:1])
    e2 = jnp.exp(s2 - top2[0:1]) / z
    return rank2, e2, c1, e1, redo


def _scores_kernel(xnt_ref, wq_ref, keys_ref, s_ref):
    qq = jnp.dot(wq_ref[...], xnt_ref[...], preferred_element_type=F32).astype(BF16)
    for h in range(PEER_HEADS):
        for c in range(2):
            row = (2 * h + c) * N_KEYS
            s_ref[2 * h + c] = jnp.dot(keys_ref[c, h], qq[row:row + N_KEYS], preferred_element_type=F32)


def _peer_scores(xnt, w_pq_t, keys, layer):
    d, n = xnt.shape
    tb = min(512, n)
    return pl.pallas_call(
        _scores_kernel,
        grid=(n // tb,),
        in_specs=[
            pl.BlockSpec((d, tb), lambda i: (0, i)),
            pl.BlockSpec((None,) + w_pq_t.shape[1:], lambda i: (layer, 0, 0)),
            pl.BlockSpec((None,) + keys.shape[1:], lambda i: (layer, 0, 0, 0, 0)),
        ],
        out_specs=pl.BlockSpec((2 * PEER_HEADS, N_KEYS, tb), lambda i: (0, 0, i)),
        out_shape=jax.ShapeDtypeStruct((2 * PEER_HEADS, N_KEYS, n), F32),
        compiler_params=_cparams(("parallel",)),
        name="peer_scores",
    )(xnt, w_pq_t, keys)


def _route_kernel(s_ref, r2_ref, e2_ref, c1_ref, e1_ref):
    def emit(exact):
        rank2, e2, c1, e1, redo = _route_lanes(s_ref[0], s_ref[1], exact)
        r2_ref[...] = rank2.astype(BF16)
        e2_ref[...] = e2.astype(BF16)
        c1_ref[...] = c1
        e1_ref[...] = e1
        return redo

    redo = emit(exact=False)

    @pl.when(jnp.max(redo) > 0.0)
    def _():
        emit(exact=True)


def _peer_route(scores):
    n = scores.shape[2]
    tb = min(TB_ROUTE, n)
    out = jax.ShapeDtypeStruct((PEER_HEADS, N_KEYS, n), F32)
    out16 = jax.ShapeDtypeStruct((PEER_HEADS, N_KEYS, n), BF16)
    ospec = pl.BlockSpec((None, N_KEYS, tb), lambda i, h: (h, 0, i))
    return pl.pallas_call(
        _route_kernel,
        grid=(n // tb, PEER_HEADS),
        in_specs=[pl.BlockSpec((2, N_KEYS, tb), lambda i, h: (h, 0, i))],
        out_specs=[ospec, ospec, ospec, ospec],
        out_shape=[out16, out16, out, out],
        compiler_params=_cparams(("parallel", "parallel")),
        name="peer_route",
    )(scores)


def _gelu(x):
    return 0.5 * x * (1.0 + lax.erf(x * (0.5 ** 0.5)))


def _peer_kernel(xnt_ref, u_ref, vt_ref, r2_ref, e2_ref, c1_ref, e1_ref, y_ref, a_ref, coef_ref, *, eb, ck, lc):
    e = pl.program_id(1)
    tb = xnt_ref.shape[1]
    per = ck // N_KEYS

    @pl.when(e == 0)
    def _():
        y_ref[...] = jnp.zeros_like(y_ref)

    nck = eb // ck

    def pre_act(k):
        a = jnp.dot(u_ref[k * ck:(k + 1) * ck, :], xnt_ref[...], preferred_element_type=F32)
        a_ref[k] = a
        return a

    a_next = pre_act(0)
    for k in range(nck):
        zero = None
        if k + 1 < nck:
            a_next = pre_act(k + 1)
            bits = pltpu.bitcast(a_next[0:1, tb - lc:], jnp.uint32)
            bits = lax.shift_right_logical(lax.shift_right_logical(bits, jnp.uint32(16)), jnp.uint32(16))
            zero = pltpu.bitcast(bits, F32)
        for ii in range(per):
            i = k * per + ii
            rows = slice(ii * N_KEYS, (ii + 1) * N_KEYS)
            for l0 in range(0, tb, lc):
                cols = slice(l0, l0 + lc)
                gate = jnp.zeros((N_KEYS, lc), BF16)
                for h in range(PEER_HEADS):
                    c1 = jnp.broadcast_to(c1_ref[h, i:i + 1, cols].astype(BF16), (N_KEYS, lc))
                    e1 = jnp.broadcast_to(e1_ref[h, i:i + 1, cols].astype(BF16), (N_KEYS, lc))
                    zero16 = jnp.zeros((N_KEYS, lc), BF16)
                    gate = gate + jnp.where(r2_ref[h, :, cols] < c1, e2_ref[h, :, cols], zero16) * e1
                act = gate.astype(F32) * _gelu(a_ref[k, rows, cols])
                if zero is not None and ii == per - 1 and l0 + lc == tb:
                    act = act + zero
                coef_ref[k, rows, cols] = act.astype(BF16)
        y_ref[...] += jnp.dot(vt_ref[:, k * ck:(k + 1) * ck], coef_ref[k], preferred_element_type=F32)


def _peer_sweep(xnt, u, vt, layer, r2, e2, c1, e1):
    d, n = xnt.shape
    tb = min(TB_PEER, n)
    eb = EB_PEER
    ck = CK_PEER
    lc = min(LC_PEER, tb)
    rspec = pl.BlockSpec((PEER_HEADS, N_KEYS, tb), lambda t, e: (0, 0, t))
    ispec = pl.BlockSpec((PEER_HEADS, eb // N_KEYS, tb), lambda t, e: (0, e, t))
    return pl.pallas_call(
        functools.partial(_peer_kernel, eb=eb, ck=ck, lc=lc),
        grid=(n // tb, N_EXPERTS // eb),
        in_specs=[
            pl.BlockSpec((d, tb), lambda t, e: (0, t)),
            pl.BlockSpec((None, eb, d), lambda t, e: (layer, e, 0)),
            pl.BlockSpec((None, d, eb), lambda t, e: (layer, 0, e)),
            rspec, rspec, ispec, ispec,
        ],
        out_specs=pl.BlockSpec((d, tb), lambda t, e: (0, t)),
        out_shape=jax.ShapeDtypeStruct((d, n), F32),
        scratch_shapes=[pltpu.VMEM((eb // ck, ck, tb), F32), pltpu.VMEM((eb // ck, ck, tb), BF16)],
        compiler_params=_cparams(("parallel", "arbitrary")),
        name="peer_sweep",
    )(xnt, u, vt, r2, e2, c1, e1)


def _add_t_kernel(x_ref, yt_ref, g_ref, o_ref, *, final):
    x = x_ref[...] + yt_ref[...].T
    if final:
        x = _rms(x, g_ref[...])
    o_ref[...] = x


def _add_transposed(x1, yt, g, final):
    n, d = x1.shape
    tm = min(256, n)
    return pl.pallas_call(
        functools.partial(_add_t_kernel, final=final),
        grid=(n // tm,),
        in_specs=[
            pl.BlockSpec((tm, d), lambda i: (i, 0)),
            pl.BlockSpec((d, tm), lambda i: (0, i)),
            pl.BlockSpec((1, d), lambda i: (0, 0)),
        ],
        out_specs=pl.BlockSpec((tm, d), lambda i: (i, 0)),
        out_shape=jax.ShapeDtypeStruct((n, d), F32),
        compiler_params=_cparams(("parallel",)),
        name="peer_residual",
    )(x1, yt, g.reshape(1, d))


def _trunk(x, mem, k_cache, v_cache, s_cache, mk_cache, mv_cache, p):
    prompt = mem is not None
    bsz, t, d = x.shape
    n = bsz * t
    depth = p["g_norm1"].shape[0]
    x2 = x.reshape(n, d)
    new_k, new_v, new_s, new_mk, new_mv = [], [], [], [], []
    for i in range(depth):
        j = i // 2
        attn = i % 2 == 0
        w_in = p["w_in_a"] if attn else p["w_in_b"]
        w_out = p["w_out_a"] if attn else p["w_out_b"]
        pad = WINDOW if (attn and prompt) else 0
        z = _norm_matmul(x2, p["g_norm1"][i], w_in, j, seq=t, pad=pad)
        z3 = z.reshape(bsz, pad + t, z.shape[1])
        mq_blk = (z.shape[1] - MEM_W) // MEM_W

        if prompt:
            mkv = _norm_matmul(mem.reshape(bsz * N_MEM, d), p["g_mem"][i], p["w_mem_kv"], i)
            new_mk.append(mkv[:, :MEM_W].reshape(bsz, N_MEM, N_MEM_HEADS, HEAD_DIM))
            new_mv.append(mkv[:, MEM_W:].reshape(bsz, N_MEM, N_MEM_HEADS, HEAD_DIM))
            mkv3 = mkv.reshape(bsz, N_MEM, 2 * MEM_W)
            kspec = pl.BlockSpec((None, N_MEM, MEM_W), lambda b, q: (b, 0, 0))
            vspec = pl.BlockSpec((None, N_MEM, MEM_W), lambda b, q: (b, 0, 1))
            o_mem = _mem_attention(z3, mq_blk, pad, t, mkv3, kspec, mkv3, vspec, head_major=False)
        else:
            cspec = pl.BlockSpec((None, None, N_MEM_HEADS, N_MEM, HEAD_DIM), lambda b, q, i=i: (i, b, 0, 0, 0))
            o_mem = _mem_attention(z3, mq_blk, pad, t, jnp.transpose(mk_cache, (0, 1, 3, 2, 4)), cspec,
                                   jnp.transpose(mv_cache, (0, 1, 3, 2, 4)), cspec, head_major=True)

        if attn:
            k_new = z3[:, pad:, MIX_W:2 * MIX_W]
            v_new = z3[:, pad:, 2 * MIX_W:3 * MIX_W]
            if prompt:
                bias = _band_bias(p["rel_bias"][j], CHUNK, WINDOW + CHUNK, WINDOW)
                o_mix = _band_attention_prompt(z3, bias)
                keep = min(WINDOW, t)
                k_new, v_new = k_new[:, t - keep:], v_new[:, t - keep:]
            else:
                p_len = k_cache.shape[2]
                bias = _band_bias(p["rel_bias"][j], t, p_len + t, p_len)
                o_mix = _band_attention_sample(z3, k_cache, v_cache, j, bias)
            new_k.append(k_new.reshape(bsz, -1, N_MIX_HEADS, HEAD_DIM))
            new_v.append(v_new.reshape(bsz, -1, N_MIX_HEADS, HEAD_DIM))
        else:
            o_mix, s_new = _hgrn(z3, p["lb_param"], p["g_hgrn"][j], None if prompt else s_cache, j, i)
            new_s.append(s_new)

        x1, xnt = _out_proj(o_mix.reshape(n, MIX_W), o_mem.reshape(n, MEM_W), w_out, j, x2, p["g_norm2"][i])
        r2, e2, c1, e1 = _peer_route(_peer_scores(xnt, p["w_pq_t"], p["peer_keys"], i))
        yt = _peer_sweep(xnt, p["peer_u"], p["peer_v_t"], i, r2, e2, c1, e1)
        x2 = _add_transposed(x1, yt, p["g_final"], final=(i == depth - 1))

    y = x2.reshape(bsz, t, d)
    mk_out = jnp.stack(new_mk) if new_mk else None
    mv_out = jnp.stack(new_mv) if new_mv else None
    return y, jnp.stack(new_k), jnp.stack(new_v), jnp.stack(new_s), mk_out, mv_out


def kernel(x_prompt, x_sample, cache_attn_k, cache_attn_v, state_hgrn, cache_mem_k, cache_mem_v, mem_prompt,
           g_norm1, g_norm2, g_mem, g_final, w_in_a, w_out_a, rel_bias, w_in_b, w_out_b, lb_param, g_hgrn,
           w_mem_kv, w_pq, peer_keys, peer_u, peer_v):
    p = dict(
        g_norm1=g_norm1, g_norm2=g_norm2, g_mem=g_mem, g_final=g_final, rel_bias=rel_bias,
        lb_param=lb_param, g_hgrn=g_hgrn,
        w_in_a=w_in_a.astype(BF16), w_out_a=w_out_a.astype(BF16),
        w_in_b=w_in_b.astype(BF16), w_out_b=w_out_b.astype(BF16),
        w_mem_kv=w_mem_kv.astype(BF16),
        w_pq_t=jnp.swapaxes(w_pq, 1, 2).astype(BF16),
        peer_keys=peer_keys.astype(BF16),
        peer_u=peer_u.astype(BF16),
        peer_v_t=jnp.swapaxes(peer_v, 1, 2).astype(BF16),
    )
    y_p, k_p, v_p, s_p, mk_p, mv_p = _trunk(x_prompt, mem_prompt, None, None, None, None, None, p)
    y_s, k_s, v_s, s_s, _, _ = _trunk(x_sample, None, cache_attn_k, cache_attn_v, state_hgrn,
                                      cache_mem_k, cache_mem_v, p)
    return (y_p, y_s, k_p, v_p, k_s, v_s, s_p, s_s, mk_p, mv_p)
```

```python
import functools
import math

import jax
import jax.numpy as jnp
from jax import lax
from jax.experimental import pallas as pl
from jax.experimental.pallas import tpu as pltpu

F32 = jnp.float32
BF16 = jnp.bfloat16

D_MODEL = 2048
PAST_LEN = 1024
CHUNK = 64
LEFT_CHUNKS = 8
WINDOW = LEFT_CHUNKS * CHUNK
HEAD_DIM = 128
N_MIX_HEADS = 12
N_MEM_HEADS = 4
MIX_W = N_MIX_HEADS * HEAD_DIM
MEM_W = N_MEM_HEADS * HEAD_DIM
N_MEM = 256
REL_CLIP = 256
PEER_HEADS = 8
N_KEYS = 128
N_EXPERTS = N_KEYS * N_KEYS
PEER_TOPK = 16
EPS = 1e-6
NEG_INF = -1e30

LANES = 128
SUBLANES = 8
SUB = SUBLANES
VMEM_LIMIT = 56 * 1024 * 1024

TM_PROJ = 1024
TN_PROJ = 512
TB_ROUTE = 256
TB_PEER = 512
EB_PEER = 1024
CK_PEER = 256
LC_PEER = 256
HB_MIX = 4
QB_ATTN = 256
HB_HGRN = 12


def _cparams(sem, flags=None):
    return pltpu.CompilerParams(dimension_semantics=sem, vmem_limit_bytes=VMEM_LIMIT, flags=flags)


def _rms(x, g):
    return x * lax.rsqrt(jnp.mean(x * x, axis=-1, keepdims=True) + EPS) * g


def _norm_matmul_kernel(x_ref, g_ref, w_ref, o_ref, hn_ref, *, blocks_per_seq, pad_blocks):
    live = lax.rem(pl.program_id(0), blocks_per_seq) >= pad_blocks

    @pl.when(jnp.logical_and(live, pl.program_id(1) == 0))
    def _():
        hn_ref[...] = _rms(x_ref[...], g_ref[...]).astype(BF16)

    @pl.when(live)
    def _():
        o_ref[...] = jnp.dot(hn_ref[...], w_ref[...], preferred_element_type=F32)

    @pl.when(jnp.logical_not(live))
    def _():
        o_ref[...] = jnp.zeros_like(o_ref)


def _norm_matmul(x, g, w, layer, seq=None, pad=0):
    m, d = x.shape
    n = w.shape[2]
    tm = min(TM_PROJ, m) if pad == 0 else math.gcd(math.gcd(pad, seq), TM_PROJ)
    tn = min(TN_PROJ, n)
    seq = m if pad == 0 else seq
    pad_blocks = pad // tm
    data_blocks = seq // tm
    blocks_per_seq = data_blocks + pad_blocks
    n_seq = m // seq

    def x_map(i, j):
        return ((i // blocks_per_seq) * data_blocks + jnp.maximum(lax.rem(i, blocks_per_seq) - pad_blocks, 0), 0)

    return pl.pallas_call(
        functools.partial(_norm_matmul_kernel, blocks_per_seq=blocks_per_seq, pad_blocks=pad_blocks),
        grid=(n_seq * blocks_per_seq, n // tn),
        in_specs=[
            pl.BlockSpec((tm, d), x_map),
            pl.BlockSpec((1, d), lambda i, j: (0, 0)),
            pl.BlockSpec((None, d, tn), lambda i, j: (layer, 0, j)),
        ],
        out_specs=pl.BlockSpec((tm, tn), lambda i, j: (i, j)),
        out_shape=jax.ShapeDtypeStruct((n_seq * blocks_per_seq * tm, n), F32),
        scratch_shapes=[pltpu.VMEM((tm, d), BF16)],
        compiler_params=_cparams(("parallel", "arbitrary")),
        name="norm_matmul",
    )(x, g.reshape(1, d), w)


def _bias_kernel(tab_ref, o_ref, *, tq, tk, off, hg):
    h0 = pl.program_id(0) * hg
    clip = lambda d: max(min(d, REL_CLIP), -REL_CLIP) + REL_CLIP
    for c0 in range(0, tk, LANES):
        w = min(LANES, tk - c0)
        qi = lax.broadcasted_iota(jnp.int32, (tq, w), 0)
        ki = lax.broadcasted_iota(jnp.int32, (tq, w), 1) + c0
        rel = jnp.clip(qi - ki + off, -REL_CLIP, REL_CLIP) + REL_CLIP
        lo = clip(-(c0 + w - 1) + off)
        hi = clip((tq - 1) - c0 + off)

        def body(r, accs, rel=rel):
            hit = rel == r
            return tuple(jnp.where(hit, tab_ref[h0 + h, r], acc) for h, acc in enumerate(accs))

        accs = lax.fori_loop(lo, hi + 1, body, tuple(jnp.zeros((tq, w), F32) for _ in range(hg)))
        for h in range(hg):
            o_ref[h, :, c0:c0 + w] = accs[h]


def _band_bias(table, tq, tk, off, hg=4):
    nh = table.shape[1]
    return pl.pallas_call(
        functools.partial(_bias_kernel, tq=tq, tk=tk, off=off, hg=hg),
        grid=(nh // hg,),
        in_specs=[pl.BlockSpec(memory_space=pltpu.SMEM)],
        out_specs=pl.BlockSpec((hg, tq, tk), lambda g: (g, 0, 0)),
        out_shape=jax.ShapeDtypeStruct((nh, tq, tk), F32),
        compiler_params=_cparams(("arbitrary",)),
        name="band_bias",
    )(table.T)


def _softmax_rows(s):
    m = jnp.max(s, axis=-1, keepdims=True)
    p = jnp.exp(s - m)
    return p / jnp.sum(p, axis=-1, keepdims=True)


def _nt_dot(a, b):
    return lax.dot_general(a, b, (((1,), (1,)), ((), ())), preferred_element_type=F32)


def _tn_dot(a, b):
    return lax.dot_general(a, b, (((0,), (0,)), ((), ())), preferred_element_type=F32)


def _band_prompt_kernel(q_ref, k_ref, v_ref, bias_ref, o_ref, *, hb, qb, pad):
    c = pl.program_id(2)
    start = pl.multiple_of(c * qb + (pad - WINDOW), CHUNK)
    tk = WINDOW + qb
    k_pos = lax.broadcasted_iota(jnp.int32, (qb, tk), 1) + (c * qb - WINDOW)
    live = k_pos >= 0
    scale = HEAD_DIM ** -0.5
    for h in range(hb):
        sl = slice(h * HEAD_DIM, (h + 1) * HEAD_DIM)
        q = q_ref[:, sl].astype(BF16)
        kw = k_ref[pl.ds(start, tk), sl].astype(BF16)
        vw = v_ref[pl.ds(start, tk), sl].astype(BF16)
        s = _nt_dot(q, kw) * scale + bias_ref[h]
        s = jnp.where(live, s, NEG_INF)
        p = _softmax_rows(s).astype(BF16)
        o_ref[:, sl] = jnp.dot(p, vw, preferred_element_type=F32).astype(BF16)


def _band_attention_prompt(zp3, pad, bias, hb=HB_MIX):
    b, sp, _ = zp3.shape
    s = sp - pad
    qb = min(QB_ATTN, s)
    nqc = qb // CHUNK
    ng = N_MIX_HEADS // hb
    w = hb * HEAD_DIM
    tk = WINDOW + qb
    bias_blk = jnp.concatenate(
        [jnp.pad(bias, ((0, 0), (0, 0), (j * CHUNK, (nqc - 1 - j) * CHUNK)), constant_values=NEG_INF)
         for j in range(nqc)], axis=1)
    return pl.pallas_call(
        functools.partial(_band_prompt_kernel, hb=hb, qb=qb, pad=pad),
        grid=(b, ng, s // qb),
        in_specs=[
            pl.BlockSpec((None, qb, w), lambda i, g, c: (i, c + pad // qb, g)),
            pl.BlockSpec((None, sp, w), lambda i, g, c: (i, 0, ng + g)),
            pl.BlockSpec((None, sp, w), lambda i, g, c: (i, 0, 2 * ng + g)),
            pl.BlockSpec((hb, qb, tk), lambda i, g, c: (g, 0, 0)),
        ],
        out_specs=pl.BlockSpec((None, qb, w), lambda i, g, c: (i, c, g)),
        out_shape=jax.ShapeDtypeStruct((b, s, MIX_W), BF16),
        compiler_params=_cparams(("parallel", "parallel", "arbitrary")),
        name="band_attn_prompt",
    )(zp3, zp3, zp3, bias_blk)


def _band_sample_kernel(*refs, hb, p_len):
    q_ref, kn_ref, vn_ref = refs[:3]
    kc_refs = refs[3:3 + hb]
    vc_refs = refs[3 + hb:3 + 2 * hb]
    bias_ref, o_ref = refs[3 + 2 * hb:]
    scale = HEAD_DIM ** -0.5
    for h in range(hb):
        sl = slice(h * HEAD_DIM, (h + 1) * HEAD_DIM)
        q = q_ref[:, sl].astype(BF16)
        s_c = _nt_dot(q, kc_refs[h][...].astype(BF16)) * scale + bias_ref[h, :, :p_len]
        s_n = _nt_dot(q, kn_ref[:, sl].astype(BF16)) * scale + bias_ref[h, :, p_len:]
        m = jnp.maximum(jnp.max(s_c, axis=-1, keepdims=True), jnp.max(s_n, axis=-1, keepdims=True))
        e_c = jnp.exp(s_c - m)
        e_n = jnp.exp(s_n - m)
        l = jnp.sum(e_c, axis=-1, keepdims=True) + jnp.sum(e_n, axis=-1, keepdims=True)
        o = jnp.dot((e_c / l).astype(BF16), vc_refs[h][...].astype(BF16), preferred_element_type=F32)
        o = o + jnp.dot((e_n / l).astype(BF16), vn_ref[:, sl].astype(BF16), preferred_element_type=F32)
        o_ref[:, sl] = o.astype(BF16)


def _band_attention_sample(z3, k_cache, v_cache, layer, bias, hb=N_MIX_HEADS):
    b, t, _ = z3.shape
    p_len = k_cache.shape[2]
    ng = N_MIX_HEADS // hb
    w = hb * HEAD_DIM
    k_cache = jnp.transpose(k_cache, (0, 1, 3, 2, 4))
    v_cache = jnp.transpose(v_cache, (0, 1, 3, 2, 4))

    def cache_spec(h):
        return pl.BlockSpec((None, None, None, p_len, HEAD_DIM), lambda i, g, h=h: (layer, i, g * hb + h, 0, 0))

    cache_specs = [cache_spec(h) for h in range(hb)]
    return pl.pallas_call(
        functools.partial(_band_sample_kernel, hb=hb, p_len=p_len),
        grid=(b, ng),
        in_specs=[
            pl.BlockSpec((None, t, w), lambda i, g: (i, 0, g)),
            pl.BlockSpec((None, t, w), lambda i, g: (i, 0, ng + g)),
            pl.BlockSpec((None, t, w), lambda i, g: (i, 0, 2 * ng + g)),
            *cache_specs, *cache_specs,
            pl.BlockSpec((hb, t, p_len + t), lambda i, g: (g, 0, 0)),
        ],
        out_specs=pl.BlockSpec((None, t, w), lambda i, g: (i, 0, g)),
        out_shape=jax.ShapeDtypeStruct((b, t, MIX_W), BF16),
        compiler_params=_cparams(("parallel", "parallel")),
        name="band_attn_sample",
    )(z3, z3, z3, *([k_cache] * hb), *([v_cache] * hb), bias)


def _mem_attn_kernel(q_ref, k_ref, v_ref, o_ref, *, head_major):
    for h in range(N_MEM_HEADS):
        sl = slice(h * HEAD_DIM, (h + 1) * HEAD_DIM)
        k = k_ref[h] if head_major else k_ref[:, sl]
        v = v_ref[h] if head_major else v_ref[:, sl]
        s = _nt_dot(q_ref[:, sl].astype(BF16), k.astype(BF16)) * (HEAD_DIM ** -0.5)
        p = _softmax_rows(s).astype(BF16)
        o_ref[:, sl] = jnp.dot(p, v.astype(BF16), preferred_element_type=F32).astype(BF16)


def _mem_attention(z3, q_blk, row0, t, karr, kspec, varr, vspec, head_major):
    b = z3.shape[0]
    tq = min(t, 512)
    return pl.pallas_call(
        functools.partial(_mem_attn_kernel, head_major=head_major),
        grid=(b, t // tq),
        in_specs=[
            pl.BlockSpec((None, tq, MEM_W), lambda i, j: (i, j + row0 // tq, q_blk)),
            kspec, vspec,
        ],
        out_specs=pl.BlockSpec((None, tq, MEM_W), lambda i, j: (i, j, 0)),
        out_shape=jax.ShapeDtypeStruct((b, t, MEM_W), BF16),
        compiler_params=_cparams(("parallel", "arbitrary")),
        name="mem_attn",
    )(z3, karr, varr)


def _log_sigmoid(x):
    return jnp.minimum(x, 0.0) - jnp.log1p(jnp.exp(-jnp.abs(x)))


def _logaddexp(a, b):
    hi = jnp.maximum(a, b)
    lo = jnp.minimum(a, b)
    return jnp.where(hi == -jnp.inf, hi, hi + jnp.log1p(jnp.exp(lo - hi)))


def _hgrn_kernel(*refs, hb, t, layer, has_state):
    if has_state:
        q_ref, f_ref, i_ref, g_ref, lbp_ref, gh_ref, s0_ref, o_ref, sout_ref, st_ref = refs
    else:
        q_ref, f_ref, i_ref, g_ref, lbp_ref, gh_ref, o_ref, sout_ref, st_ref = refs
    c = pl.program_id(2)
    nsub = t // SUB

    @pl.when(c == 0)
    def _():
        for h in range(hb):
            if has_state:
                st_ref[h] = s0_ref[h].T
            else:
                st_ref[h] = jnp.zeros((HEAD_DIM, HEAD_DIM), F32)

    row = lax.broadcasted_iota(jnp.int32, (t, t), 0)
    col = lax.broadcasted_iota(jnp.int32, (t, t), 1)
    rowd = lax.broadcasted_iota(jnp.int32, (t, HEAD_DIM), 0)

    pending = []
    for h in range(hb):
        sl = slice(h * HEAD_DIM, (h + 1) * HEAD_DIM)
        q = q_ref[:, sl]
        ff = f_ref[:, sl]
        v = i_ref[:, sl]

        lbp = lbp_ref[:, sl]
        e = jnp.exp(lbp - jnp.max(lbp, axis=0, keepdims=True))
        soft = e / jnp.sum(e, axis=0, keepdims=True)
        acc = soft[0:1]
        for r in range(1, layer + 1):
            acc = acc + soft[r:r + 1]
        lb = acc - soft[0:1]

        log_f = _logaddexp(jnp.log(lb), jnp.log1p(-lb) + _log_sigmoid(ff))
        k = (1.0 - lb) * jax.nn.sigmoid(-ff)

        b = log_f
        sh = 1
        while sh < t:
            b = b + jnp.where(rowd >= sh, pltpu.roll(b, sh, 0), 0.0)
            sh *= 2
        b_end = b[t - 1:t]

        st = st_ref[h]
        o = _nt_dot((q * jnp.exp(b)).astype(BF16), st.astype(BF16))

        sc = jnp.zeros((t, t), F32)
        for dl in range(SUB):
            if dl == 0:
                prod = q * k
            else:
                prod = q * pltpu.roll(k, dl, 0) * jnp.exp(jnp.minimum(b - pltpu.roll(b, dl, 0), 0.0))
            r = jnp.sum(prod, axis=1, keepdims=True)
            hit = jnp.logical_and(row - col == dl, jnp.bitwise_and(row, SUB - 1) >= dl)
            sc = jnp.where(hit, r, sc)

        q_parts = []
        k_rows = []
        for i in range(nsub):
            b_ref = b[i * SUB + SUB - 1:i * SUB + SUB]
            q_parts.append((q * jnp.exp(jnp.minimum(b - b_ref, 0.0))).astype(BF16))
            kh = k[i * SUB:(i + 1) * SUB] * jnp.exp(jnp.minimum(b_ref - b[i * SUB:(i + 1) * SUB], 0.0))
            zero = jnp.zeros_like(kh)
            k_rows.append(jnp.concatenate([kh if j == i else zero for j in range(nsub)], axis=1))
        q_cat = jnp.concatenate(q_parts, axis=1)
        k_bd = jnp.concatenate(k_rows, axis=0).astype(BF16)
        sc_far = _nt_dot(q_cat, k_bd)
        far = jnp.right_shift(col, 3) < jnp.right_shift(row, 3)
        scores = jnp.where(far, sc_far, sc)

        k_dec = (k * jnp.exp(jnp.minimum(b_end - b, 0.0))).astype(BF16)
        st_ref[h] = st * jnp.exp(b_end) + _tn_dot(v.astype(BF16), k_dec)
        pending.append((o, scores.astype(BF16)))

    for h in range(hb):
        sl = slice(h * HEAD_DIM, (h + 1) * HEAD_DIM)
        o, scores = pending[h]
        g = g_ref[:, sl]
        o = o + jnp.dot(scores, i_ref[:, sl].astype(BF16), preferred_element_type=F32)
        o = _rms(o, gh_ref[:, sl])
        o = o * (g * jax.nn.sigmoid(g))
        o_ref[:, sl] = o.astype(BF16)

    @pl.when(c == pl.num_programs(2) - 1)
    def _():
        for h in range(hb):
            sout_ref[h] = st_ref[h].T


def _hgrn(z3, lb_param, g_head, state, state_idx, layer, hb=HB_HGRN):
    assert SUB == 8
    b, s, _ = z3.shape
    t = min(CHUNK, s)
    nc = s // t
    ng = N_MIX_HEADS // hb
    w = hb * HEAD_DIM
    depth = lb_param.shape[0]
    has_state = state is not None

    def zspec(seg):
        return pl.BlockSpec((None, t, w), lambda i, g, c, seg=seg: (i, c, seg * ng + g))

    in_specs = [zspec(0), zspec(1), zspec(2), zspec(3),
                pl.BlockSpec((depth, w), lambda i, g, c: (0, g)),
                pl.BlockSpec((1, w), lambda i, g, c: (0, g))]
    args = [z3, z3, z3, z3, lb_param, g_head.reshape(1, MIX_W)]
    if has_state:
        in_specs.append(pl.BlockSpec((None, None, hb, HEAD_DIM, HEAD_DIM),
                                     lambda i, g, c: (state_idx, i, g, 0, 0)))
        args.append(state)
    return pl.pallas_call(
        functools.partial(_hgrn_kernel, hb=hb, t=t, layer=layer, has_state=has_state),
        grid=(b, ng, nc),
        in_specs=in_specs,
        out_specs=[
            pl.BlockSpec((None, t, w), lambda i, g, c: (i, c, g)),
            pl.BlockSpec((None, hb, HEAD_DIM, HEAD_DIM), lambda i, g, c: (i, g, 0, 0)),
        ],
        out_shape=[
            jax.ShapeDtypeStruct((b, s, MIX_W), BF16),
            jax.ShapeDtypeStruct((b, N_MIX_HEADS, HEAD_DIM, HEAD_DIM), F32),
        ],
        scratch_shapes=[pltpu.VMEM((hb, HEAD_DIM, HEAD_DIM), F32)],
        compiler_params=_cparams(("parallel", "parallel", "arbitrary")),
        name="hgrn2",
    )(*args)


def _out_proj_kernel(om_ref, oe_ref, wm_ref, we_ref, x_ref, g_ref, x1_ref, xnt_ref):
    x1 = x_ref[...] + jnp.dot(om_ref[...], wm_ref[...], preferred_element_type=F32)
    x1 = x1 + jnp.dot(oe_ref[...], we_ref[...], preferred_element_type=F32)
    x1_ref[...] = x1
    xnt_ref[...] = _rms(x1, g_ref[...]).T.astype(BF16)


def _out_proj(o_mix, o_mem, w_out, layer, x, g2):
    n, d = x.shape
    tm = min(256, n)
    return pl.pallas_call(
        _out_proj_kernel,
        grid=(n // tm,),
        in_specs=[
            pl.BlockSpec((tm, MIX_W), lambda i: (i, 0)),
            pl.BlockSpec((tm, MEM_W), lambda i: (i, 0)),
            pl.BlockSpec((None, MIX_W, d), lambda i: (layer, 0, 0)),
            pl.BlockSpec((None, MEM_W, d), lambda i: (layer, MIX_W // MEM_W, 0)),
            pl.BlockSpec((tm, d), lambda i: (i, 0)),
            pl.BlockSpec((1, d), lambda i: (0, 0)),
        ],
        out_specs=[
            pl.BlockSpec((tm, d), lambda i: (i, 0)),
            pl.BlockSpec((d, tm), lambda i: (0, i)),
        ],
        out_shape=[jax.ShapeDtypeStruct((n, d), F32), jax.ShapeDtypeStruct((d, n), BF16)],
        compiler_params=_cparams(("parallel",)),
        name="out_proj",
    )(o_mix, o_mem, w_out, w_out, x, g2.reshape(1, d))


_REMOVED = 2.0 ** 100


def _extract_exact(vals, order, rounds):
    rank = jnp.full(vals.shape, float(PEER_TOPK), F32)
    tops = []
    for r in range(rounds):
        m = jnp.max(vals, axis=0, keepdims=True)
        pick = jnp.min(jnp.where(vals == m, order, 1e9), axis=0, keepdims=True)
        sel = order == pick
        rank = jnp.where(sel, float(r), rank)
        vals = jnp.where(sel, -jnp.inf, vals)
        tops.append(m)
    return jnp.concatenate(tops, axis=0), rank


def _extract_fast(vals, rounds):
    tops = []
    for r in range(rounds):
        m = jnp.max(vals, axis=0, keepdims=True)
        vals = jnp.where(vals == m, -_REMOVED * (r + 1), vals)
        tops.append(m)
    rank = jnp.where(vals <= -_REMOVED, vals * (-1.0 / _REMOVED) - 1.0, float(PEER_TOPK))
    return jnp.concatenate(tops, axis=0), rank


def _route_lanes(s1, s2, exact):
    lanes = s1.shape[1]
    if exact:
        key_id = lax.broadcasted_iota(jnp.int32, (N_KEYS, lanes), 0).astype(F32)
        top1, rank1 = _extract_exact(s1, key_id, PEER_TOPK)
        top2, rank2 = _extract_exact(s2, key_id, PEER_TOPK)
    else:
        top1, rank1 = _extract_fast(s1, PEER_TOPK)
        top2, rank2 = _extract_fast(s2, PEER_TOPK)

    blk_a = top1[0:1] + top2
    blk_b = [top1[a:a + 1] + top2[0:8] for a in range(1, 8)]
    blk_c = top1[8:16] + top2[0:1]
    cand = jnp.concatenate([blk_a] + blk_b + [blk_c], axis=0)
    nrow = cand.shape[0]
    r = lax.broadcasted_iota(jnp.int32, (nrow, lanes), 0)
    a_mid = 1 + jnp.right_shift(r - 16, 3)
    b_mid = jnp.bitwise_and(r - 16, 7)
    valid = jnp.logical_or(jnp.logical_or(r < 16, r >= 72), (a_mid + 1) * (b_mid + 1) <= PEER_TOPK)
    cand = jnp.where(valid, cand, -jnp.inf)
    if exact:
        cand_id = jnp.where(r < 16, r, jnp.where(r < 72, a_mid * PEER_TOPK + b_mid, (r - 64) * PEER_TOPK))
        best, crank = _extract_exact(cand, cand_id.astype(F32), PEER_TOPK)
    else:
        best, crank = _extract_fast(cand, PEER_TOPK)
    chosen = jnp.logical_and(valid, crank < PEER_TOPK).astype(F32)
    z = jnp.sum(jnp.exp(best - best[0:1]), axis=0, keepdims=True)

    counts = [jnp.sum(chosen[0:16], axis=0, keepdims=True)]
    counts += [jnp.sum(chosen[16 + 8 * (a - 1):16 + 8 * a], axis=0, keepdims=True) for a in range(1, 8)]
    counts += [chosen[72 + a - 8:72 + a - 7] for a in range(8, 16)]
    c1 = jnp.zeros((N_KEYS, lanes), F32)
    for a in range(PEER_TOPK):
        c1 = jnp.where(rank1 == a, counts[a], c1)

    ranked = (jnp.sum((rank1 < PEER_TOPK).astype(F32), axis=0, keepdims=True)
              + jnp.sum((rank2 < PEER_TOPK).astype(F32), axis=0, keepdims=True)
              + jnp.sum(chosen, axis=0, keepdims=True))
    low = jnp.minimum(jnp.min(s1, axis=0, keepdims=True), jnp.min(s2, axis=0, keepdims=True))
    redo = jnp.logical_or(ranked != 3.0 * PEER_TOPK, jnp.logical_not(low > -0.25 * _REMOVED)).astype(F32)
    e1 = jnp.exp(s1 - top1[0:1])
    e2 = jnp.exp(s2 - top2[0:1]) / z
    return rank2, e2, c1, e1, redo


def _scores_kernel(xnt_ref, wq_ref, keys_ref, s_ref):
    qq = jnp.dot(wq_ref[...], xnt_ref[...], preferred_element_type=F32).astype(BF16)
    for h in range(PEER_HEADS):
        for c in range(2):
            row = (2 * h + c) * N_KEYS
            s_ref[2 * h + c] = jnp.dot(keys_ref[c, h], qq[row:row + N_KEYS], preferred_element_type=F32)


def _peer_scores(xnt, w_pq_t, keys, layer):
    d, n = xnt.shape
    tb = min(512, n)
    return pl.pallas_call(
        _scores_kernel,
        grid=(n // tb,),
        in_specs=[
            pl.BlockSpec((d, tb), lambda i: (0, i)),
            pl.BlockSpec((None,) + w_pq_t.shape[1:], lambda i: (layer, 0, 0)),
            pl.BlockSpec((None,) + keys.shape[1:], lambda i: (layer, 0, 0, 0, 0)),
        ],
        out_specs=pl.BlockSpec((2 * PEER_HEADS, N_KEYS, tb), lambda i: (0, 0, i)),
        out_shape=jax.ShapeDtypeStruct((2 * PEER_HEADS, N_KEYS, n), F32),
        compiler_params=_cparams(("parallel",)),
        name="peer_scores",
    )(xnt, w_pq_t, keys)


def _route_kernel(s_ref, r2_ref, e2_ref, c1_ref, e1_ref):
    def emit(exact):
        rank2, e2, c1, e1, redo = _route_lanes(s_ref[0], s_ref[1], exact)
        r2_ref[...] = rank2.astype(BF16)
        e2_ref[...] = e2.astype(BF16)
        c1_ref[...] = c1
        e1_ref[...] = e1
        return redo

    redo = emit(exact=False)

    @pl.when(jnp.max(redo) > 0.0)
    def _():
        emit(exact=True)


def _peer_route(scores):
    n = scores.shape[2]
    tb = min(TB_ROUTE, n)
    out = jax.ShapeDtypeStruct((PEER_HEADS, N_KEYS, n), F32)
    out16 = jax.ShapeDtypeStruct((PEER_HEADS, N_KEYS, n), BF16)
    ospec = pl.BlockSpec((None, N_KEYS, tb), lambda i, h: (h, 0, i))
    return pl.pallas_call(
        _route_kernel,
        grid=(n // tb, PEER_HEADS),
        in_specs=[pl.BlockSpec((2, N_KEYS, tb), lambda i, h: (h, 0, i))],
        out_specs=[ospec, ospec, ospec, ospec],
        out_shape=[out16, out16, out, out],
        compiler_params=_cparams(("parallel", "parallel")),
        name="peer_route",
    )(scores)


def _gelu(x):
    return 0.5 * x * (1.0 + lax.erf(x * (0.5 ** 0.5)))


def _peer_kernel(xnt_ref, u_ref, un_ref, vt_ref, r2_ref, e2_ref, c1_ref, e1_ref, y_ref, a_ref, coef_ref, *,
                 eb, ck, lc):
    e = pl.program_id(1)
    tb = xnt_ref.shape[1]
    per = ck // N_KEYS
    nck = eb // ck

    @pl.when(e == 0)
    def _():
        y_ref[...] = jnp.zeros_like(y_ref)
        a_ref[0] = jnp.dot(u_ref[0:ck, :], xnt_ref[...], preferred_element_type=F32)

    for k in range(nck):
        if k + 1 < nck:
            a_next = jnp.dot(u_ref[(k + 1) * ck:(k + 2) * ck, :], xnt_ref[...], preferred_element_type=F32)
            a_ref[k + 1] = a_next
        else:
            a_next = jnp.dot(un_ref[...], xnt_ref[...], preferred_element_type=F32)
        bits = pltpu.bitcast(a_next[0:1, tb - lc:], jnp.uint32)
        bits = lax.shift_right_logical(lax.shift_right_logical(bits, jnp.uint32(16)), jnp.uint32(16))
        zero = pltpu.bitcast(bits, F32)
        for ii in range(per):
            i = k * per + ii
            rows = slice(ii * N_KEYS, (ii + 1) * N_KEYS)
            for l0 in range(0, tb, lc):
                cols = slice(l0, l0 + lc)
                gate = jnp.zeros((N_KEYS, lc), BF16)
                for h in range(PEER_HEADS):
                    c1 = jnp.broadcast_to(c1_ref[h, i:i + 1, cols].astype(BF16), (N_KEYS, lc))
                    e1 = jnp.broadcast_to(e1_ref[h, i:i + 1, cols].astype(BF16), (N_KEYS, lc))
                    zero16 = jnp.zeros((N_KEYS, lc), BF16)
                    gate = gate + jnp.where(r2_ref[h, :, cols] < c1, e2_ref[h, :, cols], zero16) * e1
                act = gate.astype(F32) * _gelu(a_ref[k, rows, cols])
                if ii == per - 1 and l0 + lc == tb:
                    act = act + zero
                coef_ref[k, rows, cols] = act.astype(BF16)
        if k + 1 == nck:
            a_ref[0] = a_next
        y_ref[...] += jnp.dot(vt_ref[:, k * ck:(k + 1) * ck], coef_ref[k], preferred_element_type=F32)


def _peer_sweep(xnt, u, vt, layer, r2, e2, c1, e1):
    d, n = xnt.shape
    tb = min(TB_PEER, n)
    eb = EB_PEER
    ck = CK_PEER
    lc = min(LC_PEER, tb)
    rspec = pl.BlockSpec((PEER_HEADS, N_KEYS, tb), lambda t, e: (0, 0, t))
    ispec = pl.BlockSpec((PEER_HEADS, eb // N_KEYS, tb), lambda t, e: (0, e, t))
    last_ck = N_EXPERTS // ck - 1
    return pl.pallas_call(
        functools.partial(_peer_kernel, eb=eb, ck=ck, lc=lc),
        grid=(n // tb, N_EXPERTS // eb),
        in_specs=[
            pl.BlockSpec((d, tb), lambda t, e: (0, t)),
            pl.BlockSpec((None, eb, d), lambda t, e: (layer, e, 0)),
            pl.BlockSpec((None, ck, d), lambda t, e: (layer, jnp.minimum((e + 1) * (eb // ck), last_ck), 0)),
            pl.BlockSpec((None, d, eb), lambda t, e: (layer, 0, e)),
            rspec, rspec, ispec, ispec,
        ],
        out_specs=pl.BlockSpec((d, tb), lambda t, e: (0, t)),
        out_shape=jax.ShapeDtypeStruct((d, n), F32),
        scratch_shapes=[pltpu.VMEM((eb // ck, ck, tb), F32), pltpu.VMEM((eb // ck, ck, tb), BF16)],
        compiler_params=_cparams(("parallel", "arbitrary")),
        name="peer_sweep",
    )(xnt, u, u, vt, r2, e2, c1, e1)


def _add_t_kernel(x_ref, yt_ref, g_ref, o_ref, *, final):
    x = x_ref[...] + yt_ref[...].T
    if final:
        x = _rms(x, g_ref[...])
    o_ref[...] = x


def _add_transposed(x1, yt, g, final):
    n, d = x1.shape
    tm = min(256, n)
    return pl.pallas_call(
        functools.partial(_add_t_kernel, final=final),
        grid=(n // tm,),
        in_specs=[
            pl.BlockSpec((tm, d), lambda i: (i, 0)),
            pl.BlockSpec((d, tm), lambda i: (0, i)),
            pl.BlockSpec((1, d), lambda i: (0, 0)),
        ],
        out_specs=pl.BlockSpec((tm, d), lambda i: (i, 0)),
        out_shape=jax.ShapeDtypeStruct((n, d), F32),
        compiler_params=_cparams(("parallel",)),
        name="peer_residual",
    )(x1, yt, g.reshape(1, d))


def _trunk(x, mem, k_cache, v_cache, s_cache, mk_cache, mv_cache, p):
    prompt = mem is not None
    bsz, t, d = x.shape
    n = bsz * t
    depth = p["g_norm1"].shape[0]
    x2 = x.reshape(n, d)
    new_k, new_v, new_s, new_mk, new_mv = [], [], [], [], []
    for i in range(depth):
        j = i // 2
        attn = i % 2 == 0
        w_in = p["w_in_a"] if attn else p["w_in_b"]
        w_out = p["w_out_a"] if attn else p["w_out_b"]
        tm = min(TM_PROJ, t)
        pad = -(-WINDOW // tm) * tm if (attn and prompt) else 0
        z = _norm_matmul(x2, p["g_norm1"][i], w_in, j, seq=t, pad=pad)
        z3 = z.reshape(bsz, pad + t, z.shape[1])
        mq_blk = (z.shape[1] - MEM_W) // MEM_W

        if prompt:
            mkv = _norm_matmul(mem.reshape(bsz * N_MEM, d), p["g_mem"][i], p["w_mem_kv"], i)
            new_mk.append(mkv[:, :MEM_W].reshape(bsz, N_MEM, N_MEM_HEADS, HEAD_DIM))
            new_mv.append(mkv[:, MEM_W:].reshape(bsz, N_MEM, N_MEM_HEADS, HEAD_DIM))
            mkv3 = mkv.reshape(bsz, N_MEM, 2 * MEM_W)
            kspec = pl.BlockSpec((None, N_MEM, MEM_W), lambda b, q: (b, 0, 0))
            vspec = pl.BlockSpec((None, N_MEM, MEM_W), lambda b, q: (b, 0, 1))
            o_mem = _mem_attention(z3, mq_blk, pad, t, mkv3, kspec, mkv3, vspec, head_major=False)
        else:
            cspec = pl.BlockSpec((None, None, N_MEM_HEADS, N_MEM, HEAD_DIM), lambda b, q, i=i: (i, b, 0, 0, 0))
            o_mem = _mem_attention(z3, mq_blk, pad, t, jnp.transpose(mk_cache, (0, 1, 3, 2, 4)), cspec,
                                   jnp.transpose(mv_cache, (0, 1, 3, 2, 4)), cspec, head_major=True)

        if attn:
            k_new = z3[:, pad:, MIX_W:2 * MIX_W]
            v_new = z3[:, pad:, 2 * MIX_W:3 * MIX_W]
            if prompt:
                bias = _band_bias(p["rel_bias"][j], CHUNK, WINDOW + CHUNK, WINDOW)
                o_mix = _band_attention_prompt(z3, pad, bias)
                keep = min(WINDOW, t)
                k_new, v_new = k_new[:, t - keep:], v_new[:, t - keep:]
            else:
                p_len = k_cache.shape[2]
                bias = _band_bias(p["rel_bias"][j], t, p_len + t, p_len)
                o_mix = _band_attention_sample(z3, k_cache, v_cache, j, bias)
            new_k.append(k_new.reshape(bsz, -1, N_MIX_HEADS, HEAD_DIM))
            new_v.append(v_new.reshape(bsz, -1, N_MIX_HEADS, HEAD_DIM))
        else:
            o_mix, s_new = _hgrn(z3, p["lb_param"], p["g_hgrn"][j], None if prompt else s_cache, j, i)
            new_s.append(s_new)

        x1, xnt = _out_proj(o_mix.reshape(n, MIX_W), o_mem.reshape(n, MEM_W), w_out, j, x2, p["g_norm2"][i])
        r2, e2, c1, e1 = _peer_route(_peer_scores(xnt, p["w_pq_t"], p["peer_keys"], i))
        yt = _peer_sweep(xnt, p["peer_u"], p["peer_v_t"], i, r2, e2, c1, e1)
        x2 = _add_transposed(x1, yt, p["g_final"], final=(i == depth - 1))

    y = x2.reshape(bsz, t, d)
    mk_out = jnp.stack(new_mk) if new_mk else None
    mv_out = jnp.stack(new_mv) if new_mv else None
    return y, jnp.stack(new_k), jnp.stack(new_v), jnp.stack(new_s), mk_out, mv_out


def kernel(x_prompt, x_sample, cache_attn_k, cache_attn_v, state_hgrn, cache_mem_k, cache_mem_v, mem_prompt,
           g_norm1, g_norm2, g_mem, g_final, w_in_a, w_out_a, rel_bias, w_in_b, w_out_b, lb_param, g_hgrn,
           w_mem_kv, w_pq, peer_keys, peer_u, peer_v):
    p = dict(
        g_norm1=g_norm1, g_norm2=g_norm2, g_mem=g_mem, g_final=g_final, rel_bias=rel_bias,
        lb_param=lb_param, g_hgrn=g_hgrn,
        w_in_a=w_in_a.astype(BF16), w_out_a=w_out_a.astype(BF16),
        w_in_b=w_in_b.astype(BF16), w_out_b=w_out_b.astype(BF16),
        w_mem_kv=w_mem_kv.astype(BF16),
        w_pq_t=jnp.swapaxes(w_pq, 1, 2).astype(BF16),
        peer_keys=peer_keys.astype(BF16),
        peer_u=peer_u.astype(BF16),
        peer_v_t=jnp.swapaxes(peer_v, 1, 2).astype(BF16),
    )
    y_p, k_p, v_p, s_p, mk_p, mv_p = _trunk(x_prompt, mem_prompt, None, None, None, None, None, p)
    y_s, k_s, v_s, s_s, _, _ = _trunk(x_sample, None, cache_attn_k, cache_attn_v, state_hgrn,
                                      cache_mem_k, cache_mem_v, p)
    return (y_p, y_s, k_p, v_p, k_s, v_s, s_p, s_s, mk_p, mv_p)
```

```python
import functools
import math

import jax
import jax.numpy as jnp
from jax import lax
from jax.experimental import pallas as pl
from jax.experimental.pallas import tpu as pltpu

F32 = jnp.float32
BF16 = jnp.bfloat16

D_MODEL = 2048
PAST_LEN = 1024
CHUNK = 64
LEFT_CHUNKS = 8
WINDOW = LEFT_CHUNKS * CHUNK
HEAD_DIM = 128
N_MIX_HEADS = 12
N_MEM_HEADS = 4
MIX_W = N_MIX_HEADS * HEAD_DIM
MEM_W = N_MEM_HEADS * HEAD_DIM
N_MEM = 256
REL_CLIP = 256
PEER_HEADS = 8
N_KEYS = 128
N_EXPERTS = N_KEYS * N_KEYS
PEER_TOPK = 16
EPS = 1e-6
NEG_INF = -1e30

LANES = 128
SUBLANES = 8
SUB = SUBLANES
VMEM_LIMIT = 56 * 1024 * 1024

TM_PROJ = 1024
TN_PROJ = 1664
TB_ROUTE = 256
TB_PEER = 512
EB_PEER = 1024
CK_PEER = 256
AHEAD_PEER = 1
LC_PEER = 256
HB_MIX = 4
QB_ATTN = 256
HB_HGRN = 12


def _cparams(sem, flags=None):
    return pltpu.CompilerParams(dimension_semantics=sem, vmem_limit_bytes=VMEM_LIMIT, flags=flags)


def _rms(x, g):
    return x * lax.rsqrt(jnp.mean(x * x, axis=-1, keepdims=True) + EPS) * g


def _norm_matmul_kernel(x_ref, g_ref, w_ref, o_ref, hn_ref, *, blocks_per_seq, pad_blocks):
    live = lax.rem(pl.program_id(0), blocks_per_seq) >= pad_blocks

    @pl.when(jnp.logical_and(live, pl.program_id(1) == 0))
    def _():
        hn_ref[...] = _rms(x_ref[...], g_ref[...]).astype(BF16)

    @pl.when(live)
    def _():
        o_ref[...] = jnp.dot(hn_ref[...], w_ref[...], preferred_element_type=F32)

    @pl.when(jnp.logical_not(live))
    def _():
        o_ref[...] = jnp.zeros_like(o_ref)


def _norm_matmul(x, g, w, layer, seq=None, pad=0):
    m, d = x.shape
    n = w.shape[2]
    tm = min(TM_PROJ, m) if pad == 0 else math.gcd(math.gcd(pad, seq), TM_PROJ)
    tn = n // -(-n // TN_PROJ)
    assert n % tn == 0 and tn % LANES == 0
    seq = m if pad == 0 else seq
    pad_blocks = pad // tm
    data_blocks = seq // tm
    blocks_per_seq = data_blocks + pad_blocks
    n_seq = m // seq

    def x_map(i, j):
        return ((i // blocks_per_seq) * data_blocks + jnp.maximum(lax.rem(i, blocks_per_seq) - pad_blocks, 0), 0)

    return pl.pallas_call(
        functools.partial(_norm_matmul_kernel, blocks_per_seq=blocks_per_seq, pad_blocks=pad_blocks),
        grid=(n_seq * blocks_per_seq, n // tn),
        in_specs=[
            pl.BlockSpec((tm, d), x_map),
            pl.BlockSpec((1, d), lambda i, j: (0, 0)),
            pl.BlockSpec((None, d, tn), lambda i, j: (layer, 0, j)),
        ],
        out_specs=pl.BlockSpec((tm, tn), lambda i, j: (i, j)),
        out_shape=jax.ShapeDtypeStruct((n_seq * blocks_per_seq * tm, n), F32),
        scratch_shapes=[pltpu.VMEM((tm, d), BF16)],
        compiler_params=_cparams(("parallel", "arbitrary")),
        name="norm_matmul",
    )(x, g.reshape(1, d), w)


def _bias_kernel(tab_ref, o_ref, *, tq, tk, off, hg):
    h0 = pl.program_id(0) * hg
    clip = lambda d: max(min(d, REL_CLIP), -REL_CLIP) + REL_CLIP
    for c0 in range(0, tk, LANES):
        w = min(LANES, tk - c0)
        qi = lax.broadcasted_iota(jnp.int32, (tq, w), 0)
        ki = lax.broadcasted_iota(jnp.int32, (tq, w), 1) + c0
        rel = jnp.clip(qi - ki + off, -REL_CLIP, REL_CLIP) + REL_CLIP
        lo = clip(-(c0 + w - 1) + off)
        hi = clip((tq - 1) - c0 + off)

        def body(r, accs, rel=rel):
            hit = rel == r
            return tuple(jnp.where(hit, tab_ref[h0 + h, r], acc) for h, acc in enumerate(accs))

        accs = lax.fori_loop(lo, hi + 1, body, tuple(jnp.zeros((tq, w), F32) for _ in range(hg)))
        for h in range(hg):
            o_ref[h, :, c0:c0 + w] = accs[h]


def _band_bias(table, tq, tk, off, hg=4):
    nh = table.shape[1]
    return pl.pallas_call(
        functools.partial(_bias_kernel, tq=tq, tk=tk, off=off, hg=hg),
        grid=(nh // hg,),
        in_specs=[pl.BlockSpec(memory_space=pltpu.SMEM)],
        out_specs=pl.BlockSpec((hg, tq, tk), lambda g: (g, 0, 0)),
        out_shape=jax.ShapeDtypeStruct((nh, tq, tk), F32),
        compiler_params=_cparams(("arbitrary",)),
        name="band_bias",
    )(table.T)


def _softmax_rows(s):
    m = jnp.max(s, axis=-1, keepdims=True)
    p = jnp.exp(s - m)
    return p / jnp.sum(p, axis=-1, keepdims=True)


def _nt_dot(a, b):
    return lax.dot_general(a, b, (((1,), (1,)), ((), ())), preferred_element_type=F32)


def _tn_dot(a, b):
    return lax.dot_general(a, b, (((0,), (0,)), ((), ())), preferred_element_type=F32)


def _band_prompt_kernel(*refs, hb, qb, nwb):
    q_ref = refs[0]
    k_refs = refs[1:1 + nwb]
    v_refs = refs[1 + nwb:1 + 2 * nwb]
    bias_ref, o_ref = refs[1 + 2 * nwb:]
    c = pl.program_id(2)
    tk = WINDOW + qb
    k_pos = lax.broadcasted_iota(jnp.int32, (qb, tk), 1) + (c * qb - WINDOW)
    live = k_pos >= 0
    scale = HEAD_DIM ** -0.5
    for h in range(hb):
        sl = slice(h * HEAD_DIM, (h + 1) * HEAD_DIM)
        q = q_ref[:, sl].astype(BF16)
        kw = jnp.concatenate([r[:, sl].astype(BF16) for r in k_refs], axis=0)
        vw = jnp.concatenate([r[:, sl].astype(BF16) for r in v_refs], axis=0)
        s = _nt_dot(q, kw) * scale + bias_ref[h]
        s = jnp.where(live, s, NEG_INF)
        p = _softmax_rows(s).astype(BF16)
        o_ref[:, sl] = jnp.dot(p, vw, preferred_element_type=F32).astype(BF16)


def _band_attention_prompt(zp3, pad, bias, hb=N_MIX_HEADS):
    b, sp, _ = zp3.shape
    s = sp - pad
    qb = min(QB_ATTN, s)
    assert WINDOW % qb == 0 and (pad - WINDOW) % qb == 0
    nqc = qb // CHUNK
    nwb = WINDOW // qb + 1
    blk0 = (pad - WINDOW) // qb
    ng = N_MIX_HEADS // hb
    w = hb * HEAD_DIM
    tk = WINDOW + qb

    def win_spec(j, seg):
        return pl.BlockSpec((None, qb, w), lambda i, g, c, j=j, seg=seg: (i, c + blk0 + j, seg * ng + g))
    bias_blk = jnp.concatenate(
        [jnp.pad(bias, ((0, 0), (0, 0), (j * CHUNK, (nqc - 1 - j) * CHUNK)), constant_values=NEG_INF)
         for j in range(nqc)], axis=1)
    return pl.pallas_call(
        functools.partial(_band_prompt_kernel, hb=hb, qb=qb, nwb=nwb),
        grid=(b, ng, s // qb),
        in_specs=[
            win_spec(nwb - 1, 0),
            *[win_spec(j, 1) for j in range(nwb)],
            *[win_spec(j, 2) for j in range(nwb)],
            pl.BlockSpec((hb, qb, tk), lambda i, g, c: (g, 0, 0), pipeline_mode=pl.Buffered(1)),
        ],
        out_specs=pl.BlockSpec((None, qb, w), lambda i, g, c: (i, c, g)),
        out_shape=jax.ShapeDtypeStruct((b, s, MIX_W), BF16),
        compiler_params=_cparams(("parallel", "parallel", "arbitrary")),
        name="band_attn_prompt",
    )(*([zp3] * (1 + 2 * nwb)), bias_blk)


def _band_sample_kernel(*refs, hb, p_len):
    q_ref, kn_ref, vn_ref = refs[:3]
    kc_refs = refs[3:3 + hb]
    vc_refs = refs[3 + hb:3 + 2 * hb]
    bias_ref, o_ref = refs[3 + 2 * hb:]
    scale = HEAD_DIM ** -0.5
    for h in range(hb):
        sl = slice(h * HEAD_DIM, (h + 1) * HEAD_DIM)
        q = q_ref[:, sl].astype(BF16)
        s_c = _nt_dot(q, kc_refs[h][...].astype(BF16)) * scale + bias_ref[h, :, :p_len]
        s_n = _nt_dot(q, kn_ref[:, sl].astype(BF16)) * scale + bias_ref[h, :, p_len:]
        m = jnp.maximum(jnp.max(s_c, axis=-1, keepdims=True), jnp.max(s_n, axis=-1, keepdims=True))
        e_c = jnp.exp(s_c - m)
        e_n = jnp.exp(s_n - m)
        l = jnp.sum(e_c, axis=-1, keepdims=True) + jnp.sum(e_n, axis=-1, keepdims=True)
        o = jnp.dot((e_c / l).astype(BF16), vc_refs[h][...].astype(BF16), preferred_element_type=F32)
        o = o + jnp.dot((e_n / l).astype(BF16), vn_ref[:, sl].astype(BF16), preferred_element_type=F32)
        o_ref[:, sl] = o.astype(BF16)


def _band_attention_sample(z3, k_cache, v_cache, layer, bias, hb=N_MIX_HEADS):
    b, t, _ = z3.shape
    p_len = k_cache.shape[2]
    ng = N_MIX_HEADS // hb
    w = hb * HEAD_DIM
    k_cache = jnp.transpose(k_cache, (0, 1, 3, 2, 4))
    v_cache = jnp.transpose(v_cache, (0, 1, 3, 2, 4))

    def cache_spec(h):
        return pl.BlockSpec((None, None, None, p_len, HEAD_DIM), lambda i, g, h=h: (layer, i, g * hb + h, 0, 0))

    cache_specs = [cache_spec(h) for h in range(hb)]
    return pl.pallas_call(
        functools.partial(_band_sample_kernel, hb=hb, p_len=p_len),
        grid=(b, ng),
        in_specs=[
            pl.BlockSpec((None, t, w), lambda i, g: (i, 0, g)),
            pl.BlockSpec((None, t, w), lambda i, g: (i, 0, ng + g)),
            pl.BlockSpec((None, t, w), lambda i, g: (i, 0, 2 * ng + g)),
            *cache_specs, *cache_specs,
            pl.BlockSpec((hb, t, p_len + t), lambda i, g: (g, 0, 0)),
        ],
        out_specs=pl.BlockSpec((None, t, w), lambda i, g: (i, 0, g)),
        out_shape=jax.ShapeDtypeStruct((b, t, MIX_W), BF16),
        compiler_params=_cparams(("parallel", "parallel")),
        name="band_attn_sample",
    )(z3, z3, z3, *([k_cache] * hb), *([v_cache] * hb), bias)


def _mem_attn_kernel(q_ref, k_ref, v_ref, o_ref, *, head_major):
    for h in range(N_MEM_HEADS):
        sl = slice(h * HEAD_DIM, (h + 1) * HEAD_DIM)
        k = k_ref[h] if head_major else k_ref[:, sl]
        v = v_ref[h] if head_major else v_ref[:, sl]
        s = _nt_dot(q_ref[:, sl].astype(BF16), k.astype(BF16)) * (HEAD_DIM ** -0.5)
        p = _softmax_rows(s).astype(BF16)
        o_ref[:, sl] = jnp.dot(p, v.astype(BF16), preferred_element_type=F32).astype(BF16)


def _mem_attention(z3, q_blk, row0, t, karr, kspec, varr, vspec, head_major):
    b = z3.shape[0]
    tq = min(t, 512)
    return pl.pallas_call(
        functools.partial(_mem_attn_kernel, head_major=head_major),
        grid=(b, t // tq),
        in_specs=[
            pl.BlockSpec((None, tq, MEM_W), lambda i, j: (i, j + row0 // tq, q_blk)),
            kspec, vspec,
        ],
        out_specs=pl.BlockSpec((None, tq, MEM_W), lambda i, j: (i, j, 0)),
        out_shape=jax.ShapeDtypeStruct((b, t, MEM_W), BF16),
        compiler_params=_cparams(("parallel", "arbitrary")),
        name="mem_attn",
    )(z3, karr, varr)


def _log_sigmoid(x):
    return jnp.minimum(x, 0.0) - jnp.log1p(jnp.exp(-jnp.abs(x)))


def _logaddexp(a, b):
    hi = jnp.maximum(a, b)
    lo = jnp.minimum(a, b)
    return jnp.where(hi == -jnp.inf, hi, hi + jnp.log1p(jnp.exp(lo - hi)))


def _hgrn_kernel(*refs, hb, t, layer, has_state):
    if has_state:
        q_ref, f_ref, i_ref, g_ref, lbp_ref, gh_ref, s0_ref, o_ref, sout_ref, st_ref = refs
    else:
        q_ref, f_ref, i_ref, g_ref, lbp_ref, gh_ref, o_ref, sout_ref, st_ref = refs
    c = pl.program_id(2)
    nsub = t // SUB

    @pl.when(c == 0)
    def _():
        for h in range(hb):
            if has_state:
                st_ref[h] = s0_ref[h].T
            else:
                st_ref[h] = jnp.zeros((HEAD_DIM, HEAD_DIM), F32)

    row = lax.broadcasted_iota(jnp.int32, (t, t), 0)
    col = lax.broadcasted_iota(jnp.int32, (t, t), 1)
    rowd = lax.broadcasted_iota(jnp.int32, (t, HEAD_DIM), 0)

    pending = []
    for h in range(hb):
        sl = slice(h * HEAD_DIM, (h + 1) * HEAD_DIM)
        q = q_ref[:, sl]
        ff = f_ref[:, sl]
        v = i_ref[:, sl]

        lbp = lbp_ref[:, sl]
        e = jnp.exp(lbp - jnp.max(lbp, axis=0, keepdims=True))
        soft = e / jnp.sum(e, axis=0, keepdims=True)
        acc = soft[0:1]
        for r in range(1, layer + 1):
            acc = acc + soft[r:r + 1]
        lb = acc - soft[0:1]

        log_f = _logaddexp(jnp.log(lb), jnp.log1p(-lb) + _log_sigmoid(ff))
        k = (1.0 - lb) * jax.nn.sigmoid(-ff)

        b = log_f
        sh = 1
        while sh < t:
            b = b + jnp.where(rowd >= sh, pltpu.roll(b, sh, 0), 0.0)
            sh *= 2
        b_end = b[t - 1:t]

        st = st_ref[h]
        o = _nt_dot((q * jnp.exp(b)).astype(BF16), st.astype(BF16))

        sc = jnp.zeros((t, t), F32)
        for dl in range(SUB):
            if dl == 0:
                prod = q * k
            else:
                prod = q * pltpu.roll(k, dl, 0) * jnp.exp(jnp.minimum(b - pltpu.roll(b, dl, 0), 0.0))
            r = jnp.sum(prod, axis=1, keepdims=True)
            hit = jnp.logical_and(row - col == dl, jnp.bitwise_and(row, SUB - 1) >= dl)
            sc = jnp.where(hit, r, sc)

        q_parts = []
        k_rows = []
        for i in range(nsub):
            b_ref = b[i * SUB + SUB - 1:i * SUB + SUB]
            q_parts.append((q * jnp.exp(jnp.minimum(b - b_ref, 0.0))).astype(BF16))
            kh = k[i * SUB:(i + 1) * SUB] * jnp.exp(jnp.minimum(b_ref - b[i * SUB:(i + 1) * SUB], 0.0))
            zero = jnp.zeros_like(kh)
            k_rows.append(jnp.concatenate([kh if j == i else zero for j in range(nsub)], axis=1))
        q_cat = jnp.concatenate(q_parts, axis=1)
        k_bd = jnp.concatenate(k_rows, axis=0).astype(BF16)
        sc_far = _nt_dot(q_cat, k_bd)
        far = jnp.right_shift(col, 3) < jnp.right_shift(row, 3)
        scores = jnp.where(far, sc_far, sc)

        k_dec = (k * jnp.exp(jnp.minimum(b_end - b, 0.0))).astype(BF16)
        st_ref[h] = st * jnp.exp(b_end) + _tn_dot(v.astype(BF16), k_dec)
        pending.append((o, scores.astype(BF16)))

    for h in range(hb):
        sl = slice(h * HEAD_DIM, (h + 1) * HEAD_DIM)
        o, scores = pending[h]
        g = g_ref[:, sl]
        o = o + jnp.dot(scores, i_ref[:, sl].astype(BF16), preferred_element_type=F32)
        o = _rms(o, gh_ref[:, sl])
        o = o * (g * jax.nn.sigmoid(g))
        o_ref[:, sl] = o.astype(BF16)

    @pl.when(c == pl.num_programs(2) - 1)
    def _():
        for h in range(hb):
            sout_ref[h] = st_ref[h].T


def _hgrn(z3, lb_param, g_head, state, state_idx, layer, hb=HB_HGRN):
    assert SUB == 8
    b, s, _ = z3.shape
    t = min(CHUNK, s)
    nc = s // t
    ng = N_MIX_HEADS // hb
    w = hb * HEAD_DIM
    depth = lb_param.shape[0]
    has_state = state is not None

    def zspec(seg):
        return pl.BlockSpec((None, t, w), lambda i, g, c, seg=seg: (i, c, seg * ng + g))

    in_specs = [zspec(0), zspec(1), zspec(2), zspec(3),
                pl.BlockSpec((depth, w), lambda i, g, c: (0, g)),
                pl.BlockSpec((1, w), lambda i, g, c: (0, g))]
    args = [z3, z3, z3, z3, lb_param, g_head.reshape(1, MIX_W)]
    if has_state:
        in_specs.append(pl.BlockSpec((None, None, hb, HEAD_DIM, HEAD_DIM),
                                     lambda i, g, c: (state_idx, i, g, 0, 0)))
        args.append(state)
    return pl.pallas_call(
        functools.partial(_hgrn_kernel, hb=hb, t=t, layer=layer, has_state=has_state),
        grid=(b, ng, nc),
        in_specs=in_specs,
        out_specs=[
            pl.BlockSpec((None, t, w), lambda i, g, c: (i, c, g)),
            pl.BlockSpec((None, hb, HEAD_DIM, HEAD_DIM), lambda i, g, c: (i, g, 0, 0)),
        ],
        out_shape=[
            jax.ShapeDtypeStruct((b, s, MIX_W), BF16),
            jax.ShapeDtypeStruct((b, N_MIX_HEADS, HEAD_DIM, HEAD_DIM), F32),
        ],
        scratch_shapes=[pltpu.VMEM((hb, HEAD_DIM, HEAD_DIM), F32)],
        compiler_params=_cparams(("parallel", "parallel", "arbitrary")),
        name="hgrn2",
    )(*args)


def _out_proj_kernel(om_ref, oe_ref, wm_ref, we_ref, x_ref, g_ref, x1_ref, xnt_ref):
    x1 = x_ref[...] + jnp.dot(om_ref[...], wm_ref[...], preferred_element_type=F32)
    x1 = x1 + jnp.dot(oe_ref[...], we_ref[...], preferred_element_type=F32)
    x1_ref[...] = x1
    xnt_ref[...] = _rms(x1, g_ref[...]).T.astype(BF16)


def _out_proj(o_mix, o_mem, w_out, layer, x, g2):
    n, d = x.shape
    tm = min(512, n)
    once = pl.Buffered(1)
    return pl.pallas_call(
        _out_proj_kernel,
        grid=(n // tm,),
        in_specs=[
            pl.BlockSpec((tm, MIX_W), lambda i: (i, 0)),
            pl.BlockSpec((tm, MEM_W), lambda i: (i, 0)),
            pl.BlockSpec((None, MIX_W, d), lambda i: (layer, 0, 0), pipeline_mode=once),
            pl.BlockSpec((None, MEM_W, d), lambda i: (layer, MIX_W // MEM_W, 0), pipeline_mode=once),
            pl.BlockSpec((tm, d), lambda i: (i, 0)),
            pl.BlockSpec((1, d), lambda i: (0, 0)),
        ],
        out_specs=[
            pl.BlockSpec((tm, d), lambda i: (i, 0)),
            pl.BlockSpec((d, tm), lambda i: (0, i)),
        ],
        out_shape=[jax.ShapeDtypeStruct((n, d), F32), jax.ShapeDtypeStruct((d, n), BF16)],
        compiler_params=_cparams(("parallel",)),
        name="out_proj",
    )(o_mix, o_mem, w_out, w_out, x, g2.reshape(1, d))


_REMOVED = 2.0 ** 100


def _extract_exact(vals, order, rounds):
    rank = jnp.full(vals.shape, float(PEER_TOPK), F32)
    tops = []
    for r in range(rounds):
        m = jnp.max(vals, axis=0, keepdims=True)
        pick = jnp.min(jnp.where(vals == m, order, 1e9), axis=0, keepdims=True)
        sel = order == pick
        rank = jnp.where(sel, float(r), rank)
        vals = jnp.where(sel, -jnp.inf, vals)
        tops.append(m)
    return jnp.concatenate(tops, axis=0), rank


def _extract_fast(vals, rounds):
    tops = []
    for r in range(rounds):
        m = jnp.max(vals, axis=0, keepdims=True)
        vals = jnp.where(vals == m, -_REMOVED * (r + 1), vals)
        tops.append(m)
    rank = jnp.where(vals <= -_REMOVED, vals * (-1.0 / _REMOVED) - 1.0, float(PEER_TOPK))
    return jnp.concatenate(tops, axis=0), rank


def _route_lanes(s1, s2, exact):
    lanes = s1.shape[1]
    if exact:
        key_id = lax.broadcasted_iota(jnp.int32, (N_KEYS, lanes), 0).astype(F32)
        top1, rank1 = _extract_exact(s1, key_id, PEER_TOPK)
        top2, rank2 = _extract_exact(s2, key_id, PEER_TOPK)
    else:
        top1, rank1 = _extract_fast(s1, PEER_TOPK)
        top2, rank2 = _extract_fast(s2, PEER_TOPK)

    blk_a = top1[0:1] + top2
    blk_b = [top1[a:a + 1] + top2[0:8] for a in range(1, 8)]
    blk_c = top1[8:16] + top2[0:1]
    cand = jnp.concatenate([blk_a] + blk_b + [blk_c], axis=0)
    nrow = cand.shape[0]
    r = lax.broadcasted_iota(jnp.int32, (nrow, lanes), 0)
    a_mid = 1 + jnp.right_shift(r - 16, 3)
    b_mid = jnp.bitwise_and(r - 16, 7)
    valid = jnp.logical_or(jnp.logical_or(r < 16, r >= 72), (a_mid + 1) * (b_mid + 1) <= PEER_TOPK)
    cand = jnp.where(valid, cand, -jnp.inf)
    if exact:
        cand_id = jnp.where(r < 16, r, jnp.where(r < 72, a_mid * PEER_TOPK + b_mid, (r - 64) * PEER_TOPK))
        best, crank = _extract_exact(cand, cand_id.astype(F32), PEER_TOPK)
    else:
        best, crank = _extract_fast(cand, PEER_TOPK)
    chosen = jnp.logical_and(valid, crank < PEER_TOPK).astype(F32)
    z = jnp.sum(jnp.exp(best - best[0:1]), axis=0, keepdims=True)

    counts = [jnp.sum(chosen[0:16], axis=0, keepdims=True)]
    counts += [jnp.sum(chosen[16 + 8 * (a - 1):16 + 8 * a], axis=0, keepdims=True) for a in range(1, 8)]
    counts += [chosen[72 + a - 8:72 + a - 7] for a in range(8, 16)]
    c1 = jnp.zeros((N_KEYS, lanes), F32)
    for a in range(PEER_TOPK):
        c1 = jnp.where(rank1 == a, counts[a], c1)

    ranked = (jnp.sum((rank1 < PEER_TOPK).astype(F32), axis=0, keepdims=True)
              + jnp.sum((rank2 < PEER_TOPK).astype(F32), axis=0, keepdims=True)
              + jnp.sum(chosen, axis=0, keepdims=True))
    low = jnp.minimum(jnp.min(s1, axis=0, keepdims=True), jnp.min(s2, axis=0, keepdims=True))
    redo = jnp.logical_or(ranked != 3.0 * PEER_TOPK, jnp.logical_not(low > -0.25 * _REMOVED)).astype(F32)
    e1 = jnp.exp(s1 - top1[0:1])
    e2 = jnp.exp(s2 - top2[0:1]) / z
    return rank2, e2, c1, e1, redo


def _scores_kernel(xnt_ref, wq_ref, keys_ref, s_ref):
    qq = jnp.dot(wq_ref[...], xnt_ref[...], preferred_element_type=F32).astype(BF16)
    for h in range(PEER_HEADS):
        for c in range(2):
            row = (2 * h + c) * N_KEYS
            s_ref[2 * h + c] = jnp.dot(keys_ref[c, h], qq[row:row + N_KEYS], preferred_element_type=F32)


def _peer_scores(xnt, w_pq_t, keys, layer):
    d, n = xnt.shape
    tb = min(512, n)
    return pl.pallas_call(
        _scores_kernel,
        grid=(n // tb,),
        in_specs=[
            pl.BlockSpec((d, tb), lambda i: (0, i)),
            pl.BlockSpec((None,) + w_pq_t.shape[1:], lambda i: (layer, 0, 0)),
            pl.BlockSpec((None,) + keys.shape[1:], lambda i: (layer, 0, 0, 0, 0)),
        ],
        out_specs=pl.BlockSpec((2 * PEER_HEADS, N_KEYS, tb), lambda i: (0, 0, i)),
        out_shape=jax.ShapeDtypeStruct((2 * PEER_HEADS, N_KEYS, n), F32),
        compiler_params=_cparams(("parallel",)),
        name="peer_scores",
    )(xnt, w_pq_t, keys)


def _route_kernel(s_ref, r2_ref, e2_ref, c1_ref, e1_ref):
    def emit(exact):
        rank2, e2, c1, e1, redo = _route_lanes(s_ref[0], s_ref[1], exact)
        r2_ref[...] = rank2.astype(BF16)
        e2_ref[...] = e2.astype(BF16)
        c1_ref[...] = c1
        e1_ref[...] = e1
        return redo

    redo = emit(exact=False)

    @pl.when(jnp.max(redo) > 0.0)
    def _():
        emit(exact=True)


def _peer_route(scores):
    n = scores.shape[2]
    tb = min(TB_ROUTE, n)
    out = jax.ShapeDtypeStruct((PEER_HEADS, N_KEYS, n), F32)
    out16 = jax.ShapeDtypeStruct((PEER_HEADS, N_KEYS, n), BF16)
    ospec = pl.BlockSpec((None, N_KEYS, tb), lambda i, h: (h, 0, i))
    return pl.pallas_call(
        _route_kernel,
        grid=(n // tb, PEER_HEADS),
        in_specs=[pl.BlockSpec((2, N_KEYS, tb), lambda i, h: (h, 0, i))],
        out_specs=[ospec, ospec, ospec, ospec],
        out_shape=[out16, out16, out, out],
        compiler_params=_cparams(("parallel", "parallel")),
        name="peer_route",
    )(scores)


def _gelu(x):
    return 0.5 * x * (1.0 + lax.erf(x * (0.5 ** 0.5)))


def _peer_kernel(xnt_ref, u_ref, un_ref, vt_ref, r2_ref, e2_ref, c1_ref, e1_ref, y_ref, a_ref, coef_ref, *,
                 eb, ck, lc):
    e = pl.program_id(1)
    tb = xnt_ref.shape[1]
    per = ck // N_KEYS
    nck = eb // ck
    ahead = un_ref.shape[0] // ck
    assert nck % 2 == 0 and ahead < nck

    @pl.when(e == 0)
    def _():
        y_ref[...] = jnp.zeros_like(y_ref)
        for k in range(ahead):
            a_ref[k] = jnp.dot(u_ref[k * ck:(k + 1) * ck, :], xnt_ref[...], preferred_element_type=F32)

    for k in range(nck):
        kk = k + ahead
        u_next = u_ref[kk * ck:(kk + 1) * ck, :] if kk < nck else un_ref[(kk - nck) * ck:(kk - nck + 1) * ck, :]
        a_next = jnp.dot(u_next, xnt_ref[...], preferred_element_type=F32)
        a_ref[kk % nck] = a_next
        bits = pltpu.bitcast(a_next[0:1, tb - lc:], jnp.uint32)
        bits = lax.shift_right_logical(lax.shift_right_logical(bits, jnp.uint32(16)), jnp.uint32(16))
        zero = pltpu.bitcast(bits, F32)
        for ii in range(per):
            i = k * per + ii
            rows = slice(ii * N_KEYS, (ii + 1) * N_KEYS)
            for l0 in range(0, tb, lc):
                cols = slice(l0, l0 + lc)
                gate = jnp.zeros((N_KEYS, lc), BF16)
                for h in range(PEER_HEADS):
                    c1 = jnp.broadcast_to(c1_ref[h, i:i + 1, cols].astype(BF16), (N_KEYS, lc))
                    e1 = jnp.broadcast_to(e1_ref[h, i:i + 1, cols].astype(BF16), (N_KEYS, lc))
                    zero16 = jnp.zeros((N_KEYS, lc), BF16)
                    gate = gate + jnp.where(r2_ref[h, :, cols] < c1, e2_ref[h, :, cols], zero16) * e1
                act = gate.astype(F32) * _gelu(a_ref[k, rows, cols])
                if ii == per - 1 and l0 + lc == tb:
                    act = act + zero
                coef_ref[k % 2, rows, cols] = act.astype(BF16)
        y_ref[...] += jnp.dot(vt_ref[:, k * ck:(k + 1) * ck], coef_ref[k % 2], preferred_element_type=F32)


def _peer_sweep(xnt, u, vt, layer, r2, e2, c1, e1):
    d, n = xnt.shape
    tb = min(TB_PEER, n)
    eb = EB_PEER
    ck = CK_PEER
    lc = min(LC_PEER, tb)
    once = pl.Buffered(1)
    rspec = pl.BlockSpec((PEER_HEADS, N_KEYS, tb), lambda t, e: (0, 0, t), pipeline_mode=once)
    ispec = pl.BlockSpec((PEER_HEADS, eb // N_KEYS, tb), lambda t, e: (0, e, t))
    ahead = AHEAD_PEER
    last_un = N_EXPERTS // (ahead * ck) - 1
    return pl.pallas_call(
        functools.partial(_peer_kernel, eb=eb, ck=ck, lc=lc),
        grid=(n // tb, N_EXPERTS // eb),
        in_specs=[
            pl.BlockSpec((d, tb), lambda t, e: (0, t), pipeline_mode=once),
            pl.BlockSpec((None, eb, d), lambda t, e: (layer, e, 0)),
            pl.BlockSpec((None, ahead * ck, d),
                         lambda t, e: (layer, jnp.minimum((e + 1) * (eb // (ahead * ck)), last_un), 0)),
            pl.BlockSpec((None, d, eb), lambda t, e: (layer, 0, e)),
            rspec, rspec, ispec, ispec,
        ],
        out_specs=pl.BlockSpec((d, tb), lambda t, e: (0, t)),
        out_shape=jax.ShapeDtypeStruct((d, n), F32),
        scratch_shapes=[pltpu.VMEM((eb // ck, ck, tb), F32), pltpu.VMEM((2, ck, tb), BF16)],
        compiler_params=_cparams(("parallel", "arbitrary")),
        name="peer_sweep",
    )(xnt, u, u, vt, r2, e2, c1, e1)


def _add_t_kernel(x_ref, yt_ref, g_ref, o_ref, *, final):
    x = x_ref[...] + yt_ref[...].T
    if final:
        x = _rms(x, g_ref[...])
    o_ref[...] = x


def _add_transposed(x1, yt, g, final):
    n, d = x1.shape
    tm = min(256, n)
    return pl.pallas_call(
        functools.partial(_add_t_kernel, final=final),
        grid=(n // tm,),
        in_specs=[
            pl.BlockSpec((tm, d), lambda i: (i, 0)),
            pl.BlockSpec((d, tm), lambda i: (0, i)),
            pl.BlockSpec((1, d), lambda i: (0, 0)),
        ],
        out_specs=pl.BlockSpec((tm, d), lambda i: (i, 0)),
        out_shape=jax.ShapeDtypeStruct((n, d), F32),
        compiler_params=_cparams(("parallel",)),
        name="peer_residual",
    )(x1, yt, g.reshape(1, d))


def _trunk(x, mem, k_cache, v_cache, s_cache, mk_cache, mv_cache, p):
    prompt = mem is not None
    bsz, t, d = x.shape
    n = bsz * t
    depth = p["g_norm1"].shape[0]
    x2 = x.reshape(n, d)
    new_k, new_v, new_s, new_mk, new_mv = [], [], [], [], []
    for i in range(depth):
        j = i // 2
        attn = i % 2 == 0
        w_in = p["w_in_a"] if attn else p["w_in_b"]
        w_out = p["w_out_a"] if attn else p["w_out_b"]
        tm = min(TM_PROJ, t)
        pad = -(-WINDOW // tm) * tm if (attn and prompt) else 0
        z = _norm_matmul(x2, p["g_norm1"][i], w_in, j, seq=t, pad=pad)
        z3 = z.reshape(bsz, pad + t, z.shape[1])
        mq_blk = (z.shape[1] - MEM_W) // MEM_W

        if prompt:
            mkv = _norm_matmul(mem.reshape(bsz * N_MEM, d), p["g_mem"][i], p["w_mem_kv"], i)
            new_mk.append(mkv[:, :MEM_W].reshape(bsz, N_MEM, N_MEM_HEADS, HEAD_DIM))
            new_mv.append(mkv[:, MEM_W:].reshape(bsz, N_MEM, N_MEM_HEADS, HEAD_DIM))
            mkv3 = mkv.reshape(bsz, N_MEM, 2 * MEM_W)
            kspec = pl.BlockSpec((None, N_MEM, MEM_W), lambda b, q: (b, 0, 0))
            vspec = pl.BlockSpec((None, N_MEM, MEM_W), lambda b, q: (b, 0, 1))
            o_mem = _mem_attention(z3, mq_blk, pad, t, mkv3, kspec, mkv3, vspec, head_major=False)
        else:
            cspec = pl.BlockSpec((None, None, N_MEM_HEADS, N_MEM, HEAD_DIM), lambda b, q, i=i: (i, b, 0, 0, 0))
            o_mem = _mem_attention(z3, mq_blk, pad, t, jnp.transpose(mk_cache, (0, 1, 3, 2, 4)), cspec,
                                   jnp.transpose(mv_cache, (0, 1, 3, 2, 4)), cspec, head_major=True)

        if attn:
            k_new = z3[:, pad:, MIX_W:2 * MIX_W]
            v_new = z3[:, pad:, 2 * MIX_W:3 * MIX_W]
            if prompt:
                bias = _band_bias(p["rel_bias"][j], CHUNK, WINDOW + CHUNK, WINDOW)
                o_mix = _band_attention_prompt(z3, pad, bias)
                keep = min(WINDOW, t)
                k_new, v_new = k_new[:, t - keep:], v_new[:, t - keep:]
            else:
                p_len = k_cache.shape[2]
                bias = _band_bias(p["rel_bias"][j], t, p_len + t, p_len)
                o_mix = _band_attention_sample(z3, k_cache, v_cache, j, bias)
            new_k.append(k_new.reshape(bsz, -1, N_MIX_HEADS, HEAD_DIM))
            new_v.append(v_new.reshape(bsz, -1, N_MIX_HEADS, HEAD_DIM))
        else:
            o_mix, s_new = _hgrn(z3, p["lb_param"], p["g_hgrn"][j], None if prompt else s_cache, j, i)
            new_s.append(s_new)

        x1, xnt = _out_proj(o_mix.reshape(n, MIX_W), o_mem.reshape(n, MEM_W), w_out, j, x2, p["g_norm2"][i])
        r2, e2, c1, e1 = _peer_route(_peer_scores(xnt, p["w_pq_t"], p["peer_keys"], i))
        yt = _peer_sweep(xnt, p["peer_u"], p["peer_v_t"], i, r2, e2, c1, e1)
        x2 = _add_transposed(x1, yt, p["g_final"], final=(i == depth - 1))

    y = x2.reshape(bsz, t, d)
    mk_out = jnp.stack(new_mk) if new_mk else None
    mv_out = jnp.stack(new_mv) if new_mv else None
    return y, jnp.stack(new_k), jnp.stack(new_v), jnp.stack(new_s), mk_out, mv_out


def kernel(x_prompt, x_sample, cache_attn_k, cache_attn_v, state_hgrn, cache_mem_k, cache_mem_v, mem_prompt,
           g_norm1, g_norm2, g_mem, g_final, w_in_a, w_out_a, rel_bias, w_in_b, w_out_b, lb_param, g_hgrn,
           w_mem_kv, w_pq, peer_keys, peer_u, peer_v):
    p = dict(
        g_norm1=g_norm1, g_norm2=g_norm2, g_mem=g_mem, g_final=g_final, rel_bias=rel_bias,
        lb_param=lb_param, g_hgrn=g_hgrn,
        w_in_a=w_in_a.astype(BF16), w_out_a=w_out_a.astype(BF16),
        w_in_b=w_in_b.astype(BF16), w_out_b=w_out_b.astype(BF16),
        w_mem_kv=w_mem_kv.astype(BF16),
        w_pq_t=jnp.swapaxes(w_pq, 1, 2).astype(BF16),
        peer_keys=peer_keys.astype(BF16),
        peer_u=peer_u.astype(BF16),
        peer_v_t=jnp.swapaxes(peer_v, 1, 2).astype(BF16),
    )
    y_p, k_p, v_p, s_p, mk_p, mv_p = _trunk(x_prompt, mem_prompt, None, None, None, None, None, p)
    y_s, k_s, v_s, s_s, _, _ = _trunk(x_sample, None, cache_attn_k, cache_attn_v, state_hgrn,
                                      cache_mem_k, cache_mem_v, p)
    return (y_p, y_s, k_p, v_p, k_s, v_s, s_p, s_s, mk_p, mv_p)
```

```python
import functools
import math

import jax
import jax.numpy as jnp
from jax import lax
from jax.experimental import pallas as pl
from jax.experimental.pallas import tpu as pltpu

F32 = jnp.float32
BF16 = jnp.bfloat16

D_MODEL = 2048
PAST_LEN = 1024
CHUNK = 64
LEFT_CHUNKS = 8
WINDOW = LEFT_CHUNKS * CHUNK
HEAD_DIM = 128
N_MIX_HEADS = 12
N_MEM_HEADS = 4
MIX_W = N_MIX_HEADS * HEAD_DIM
MEM_W = N_MEM_HEADS * HEAD_DIM
N_MEM = 256
REL_CLIP = 256
PEER_HEADS = 8
N_KEYS = 128
N_EXPERTS = N_KEYS * N_KEYS
PEER_TOPK = 16
EPS = 1e-6
NEG_INF = -1e30

LANES = 128
SUBLANES = 8
SUB = SUBLANES
VMEM_LIMIT = 56 * 1024 * 1024

TM_PROJ = 1024
TN_PROJ = 1664
TB_ROUTE = 256
TB_PEER = 512
EB_PEER = 1024
CK_PEER = 256
LC_PEER = 256
HB_MIX = 4
QB_ATTN = 256
HB_HGRN = 12


def _cparams(sem, flags=None):
    return pltpu.CompilerParams(dimension_semantics=sem, vmem_limit_bytes=VMEM_LIMIT, flags=flags)


def _rms(x, g):
    return x * lax.rsqrt(jnp.mean(x * x, axis=-1, keepdims=True) + EPS) * g


def _norm_matmul_kernel(x_ref, g_ref, w_ref, o_ref, hn_ref, *, blocks_per_seq, pad_blocks):
    live = lax.rem(pl.program_id(0), blocks_per_seq) >= pad_blocks

    @pl.when(jnp.logical_and(live, pl.program_id(1) == 0))
    def _():
        hn_ref[...] = _rms(x_ref[...], g_ref[...]).astype(BF16)

    @pl.when(live)
    def _():
        o_ref[...] = jnp.dot(hn_ref[...], w_ref[...], preferred_element_type=F32)

    @pl.when(jnp.logical_not(live))
    def _():
        o_ref[...] = jnp.zeros_like(o_ref)


def _norm_matmul(x, g, w, layer, seq=None, pad=0):
    m, d = x.shape
    n = w.shape[2]
    tm = min(TM_PROJ, m) if pad == 0 else math.gcd(math.gcd(pad, seq), TM_PROJ)
    tn = n // -(-n // TN_PROJ)
    assert n % tn == 0 and tn % LANES == 0
    seq = m if pad == 0 else seq
    pad_blocks = pad // tm
    data_blocks = seq // tm
    blocks_per_seq = data_blocks + pad_blocks
    n_seq = m // seq

    def x_map(i, j):
        return ((i // blocks_per_seq) * data_blocks + jnp.maximum(lax.rem(i, blocks_per_seq) - pad_blocks, 0), 0)

    return pl.pallas_call(
        functools.partial(_norm_matmul_kernel, blocks_per_seq=blocks_per_seq, pad_blocks=pad_blocks),
        grid=(n_seq * blocks_per_seq, n // tn),
        in_specs=[
            pl.BlockSpec((tm, d), x_map),
            pl.BlockSpec((1, d), lambda i, j: (0, 0)),
            pl.BlockSpec((None, d, tn), lambda i, j: (layer, 0, j)),
        ],
        out_specs=pl.BlockSpec((tm, tn), lambda i, j: (i, j)),
        out_shape=jax.ShapeDtypeStruct((n_seq * blocks_per_seq * tm, n), F32),
        scratch_shapes=[pltpu.VMEM((tm, d), BF16)],
        compiler_params=_cparams(("parallel", "arbitrary")),
        name="norm_matmul",
    )(x, g.reshape(1, d), w)


def _bias_kernel(tab_ref, o_ref, *, tq, tk, off, hg):
    h0 = pl.program_id(0) * hg
    clip = lambda d: max(min(d, REL_CLIP), -REL_CLIP) + REL_CLIP
    for c0 in range(0, tk, LANES):
        w = min(LANES, tk - c0)
        qi = lax.broadcasted_iota(jnp.int32, (tq, w), 0)
        ki = lax.broadcasted_iota(jnp.int32, (tq, w), 1) + c0
        rel = jnp.clip(qi - ki + off, -REL_CLIP, REL_CLIP) + REL_CLIP
        lo = clip(-(c0 + w - 1) + off)
        hi = clip((tq - 1) - c0 + off)

        def body(r, accs, rel=rel):
            hit = rel == r
            return tuple(jnp.where(hit, tab_ref[h0 + h, r], acc) for h, acc in enumerate(accs))

        accs = lax.fori_loop(lo, hi + 1, body, tuple(jnp.zeros((tq, w), F32) for _ in range(hg)))
        for h in range(hg):
            o_ref[h, :, c0:c0 + w] = accs[h]


def _band_bias(table, tq, tk, off, hg=4):
    nh = table.shape[1]
    return pl.pallas_call(
        functools.partial(_bias_kernel, tq=tq, tk=tk, off=off, hg=hg),
        grid=(nh // hg,),
        in_specs=[pl.BlockSpec(memory_space=pltpu.SMEM)],
        out_specs=pl.BlockSpec((hg, tq, tk), lambda g: (g, 0, 0)),
        out_shape=jax.ShapeDtypeStruct((nh, tq, tk), F32),
        compiler_params=_cparams(("arbitrary",)),
        name="band_bias",
    )(table.T)


def _softmax_rows(s):
    m = jnp.max(s, axis=-1, keepdims=True)
    p = jnp.exp(s - m)
    return p / jnp.sum(p, axis=-1, keepdims=True)


def _nt_dot(a, b):
    return lax.dot_general(a, b, (((1,), (1,)), ((), ())), preferred_element_type=F32)


def _tn_dot(a, b):
    return lax.dot_general(a, b, (((0,), (0,)), ((), ())), preferred_element_type=F32)


def _band_prompt_kernel(*refs, hb, qb, nwb):
    q_ref = refs[0]
    k_refs = refs[1:1 + nwb]
    v_refs = refs[1 + nwb:1 + 2 * nwb]
    bias_ref, o_ref = refs[1 + 2 * nwb:]
    c = pl.program_id(2)
    tk = WINDOW + qb
    k_pos = lax.broadcasted_iota(jnp.int32, (qb, tk), 1) + (c * qb - WINDOW)
    live = k_pos >= 0
    scale = HEAD_DIM ** -0.5
    scores = []
    for h in range(hb):
        sl = slice(h * HEAD_DIM, (h + 1) * HEAD_DIM)
        kw = jnp.concatenate([r[:, sl].astype(BF16) for r in k_refs], axis=0)
        scores.append(_nt_dot(q_ref[:, sl].astype(BF16), kw))
    for h in range(hb):
        sl = slice(h * HEAD_DIM, (h + 1) * HEAD_DIM)
        vw = jnp.concatenate([r[:, sl].astype(BF16) for r in v_refs], axis=0)
        s = scores[h] * scale + bias_ref[h]
        s = jnp.where(live, s, NEG_INF)
        p = _softmax_rows(s).astype(BF16)
        o_ref[:, sl] = jnp.dot(p, vw, preferred_element_type=F32).astype(BF16)


def _band_attention_prompt(zp3, pad, bias, hb=N_MIX_HEADS):
    b, sp, _ = zp3.shape
    s = sp - pad
    qb = min(QB_ATTN, s)
    assert WINDOW % qb == 0 and (pad - WINDOW) % qb == 0
    nqc = qb // CHUNK
    nwb = WINDOW // qb + 1
    blk0 = (pad - WINDOW) // qb
    ng = N_MIX_HEADS // hb
    w = hb * HEAD_DIM
    tk = WINDOW + qb

    def win_spec(j, seg):
        return pl.BlockSpec((None, qb, w), lambda i, g, c, j=j, seg=seg: (i, c + blk0 + j, seg * ng + g))
    bias_blk = jnp.concatenate(
        [jnp.pad(bias, ((0, 0), (0, 0), (j * CHUNK, (nqc - 1 - j) * CHUNK)), constant_values=NEG_INF)
         for j in range(nqc)], axis=1)
    return pl.pallas_call(
        functools.partial(_band_prompt_kernel, hb=hb, qb=qb, nwb=nwb),
        grid=(b, ng, s // qb),
        in_specs=[
            win_spec(nwb - 1, 0),
            *[win_spec(j, 1) for j in range(nwb)],
            *[win_spec(j, 2) for j in range(nwb)],
            pl.BlockSpec((hb, qb, tk), lambda i, g, c: (g, 0, 0), pipeline_mode=pl.Buffered(1)),
        ],
        out_specs=pl.BlockSpec((None, qb, w), lambda i, g, c: (i, c, g)),
        out_shape=jax.ShapeDtypeStruct((b, s, MIX_W), BF16),
        compiler_params=_cparams(("parallel", "parallel", "arbitrary")),
        name="band_attn_prompt",
    )(*([zp3] * (1 + 2 * nwb)), bias_blk)


def _band_sample_kernel(*refs, hb, p_len):
    q_ref, kn_ref, vn_ref = refs[:3]
    kc_refs = refs[3:3 + hb]
    vc_refs = refs[3 + hb:3 + 2 * hb]
    bias_ref, o_ref = refs[3 + 2 * hb:]
    scale = HEAD_DIM ** -0.5
    scores = []
    for h in range(hb):
        sl = slice(h * HEAD_DIM, (h + 1) * HEAD_DIM)
        q = q_ref[:, sl].astype(BF16)
        scores.append((_nt_dot(q, kc_refs[h][...].astype(BF16)), _nt_dot(q, kn_ref[:, sl].astype(BF16))))
    for h in range(hb):
        sl = slice(h * HEAD_DIM, (h + 1) * HEAD_DIM)
        s_c = scores[h][0] * scale + bias_ref[h, :, :p_len]
        s_n = scores[h][1] * scale + bias_ref[h, :, p_len:]
        m = jnp.maximum(jnp.max(s_c, axis=-1, keepdims=True), jnp.max(s_n, axis=-1, keepdims=True))
        e_c = jnp.exp(s_c - m)
        e_n = jnp.exp(s_n - m)
        l = jnp.sum(e_c, axis=-1, keepdims=True) + jnp.sum(e_n, axis=-1, keepdims=True)
        o = jnp.dot((e_c / l).astype(BF16), vc_refs[h][...].astype(BF16), preferred_element_type=F32)
        o = o + jnp.dot((e_n / l).astype(BF16), vn_ref[:, sl].astype(BF16), preferred_element_type=F32)
        o_ref[:, sl] = o.astype(BF16)


def _band_attention_sample(z3, k_cache, v_cache, layer, bias, hb=N_MIX_HEADS):
    b, t, _ = z3.shape
    p_len = k_cache.shape[2]
    ng = N_MIX_HEADS // hb
    w = hb * HEAD_DIM
    k_cache = jnp.transpose(k_cache, (0, 1, 3, 2, 4))
    v_cache = jnp.transpose(v_cache, (0, 1, 3, 2, 4))

    def cache_spec(h):
        return pl.BlockSpec((None, None, None, p_len, HEAD_DIM), lambda i, g, h=h: (layer, i, g * hb + h, 0, 0))

    cache_specs = [cache_spec(h) for h in range(hb)]
    return pl.pallas_call(
        functools.partial(_band_sample_kernel, hb=hb, p_len=p_len),
        grid=(b, ng),
        in_specs=[
            pl.BlockSpec((None, t, w), lambda i, g: (i, 0, g)),
            pl.BlockSpec((None, t, w), lambda i, g: (i, 0, ng + g)),
            pl.BlockSpec((None, t, w), lambda i, g: (i, 0, 2 * ng + g)),
            *cache_specs, *cache_specs,
            pl.BlockSpec((hb, t, p_len + t), lambda i, g: (g, 0, 0)),
        ],
        out_specs=pl.BlockSpec((None, t, w), lambda i, g: (i, 0, g)),
        out_shape=jax.ShapeDtypeStruct((b, t, MIX_W), BF16),
        compiler_params=_cparams(("parallel", "parallel")),
        name="band_attn_sample",
    )(z3, z3, z3, *([k_cache] * hb), *([v_cache] * hb), bias)


def _mem_attn_kernel(q_ref, k_ref, v_ref, o_ref, *, head_major):
    scores = []
    for h in range(N_MEM_HEADS):
        sl = slice(h * HEAD_DIM, (h + 1) * HEAD_DIM)
        k = k_ref[h] if head_major else k_ref[:, sl]
        scores.append(_nt_dot(q_ref[:, sl].astype(BF16), k.astype(BF16)))
    for h in range(N_MEM_HEADS):
        sl = slice(h * HEAD_DIM, (h + 1) * HEAD_DIM)
        v = v_ref[h] if head_major else v_ref[:, sl]
        s = scores[h] * (HEAD_DIM ** -0.5)
        p = _softmax_rows(s).astype(BF16)
        o_ref[:, sl] = jnp.dot(p, v.astype(BF16), preferred_element_type=F32).astype(BF16)


def _mem_attention(z3, q_blk, row0, t, karr, kspec, varr, vspec, head_major):
    b = z3.shape[0]
    tq = min(t, 512)
    return pl.pallas_call(
        functools.partial(_mem_attn_kernel, head_major=head_major),
        grid=(b, t // tq),
        in_specs=[
            pl.BlockSpec((None, tq, MEM_W), lambda i, j: (i, j + row0 // tq, q_blk)),
            kspec, vspec,
        ],
        out_specs=pl.BlockSpec((None, tq, MEM_W), lambda i, j: (i, j, 0)),
        out_shape=jax.ShapeDtypeStruct((b, t, MEM_W), BF16),
        compiler_params=_cparams(("parallel", "arbitrary")),
        name="mem_attn",
    )(z3, karr, varr)


def _log_sigmoid(x):
    return jnp.minimum(x, 0.0) - jnp.log1p(jnp.exp(-jnp.abs(x)))


def _logaddexp(a, b):
    hi = jnp.maximum(a, b)
    lo = jnp.minimum(a, b)
    return jnp.where(hi == -jnp.inf, hi, hi + jnp.log1p(jnp.exp(lo - hi)))


def _hgrn_kernel(*refs, hb, t, layer, has_state):
    if has_state:
        q_ref, f_ref, i_ref, g_ref, lbp_ref, gh_ref, s0_ref, o_ref, sout_ref, st_ref = refs
    else:
        q_ref, f_ref, i_ref, g_ref, lbp_ref, gh_ref, o_ref, sout_ref, st_ref = refs
    c = pl.program_id(2)
    nsub = t // SUB

    @pl.when(c == 0)
    def _():
        for h in range(hb):
            if has_state:
                st_ref[h] = s0_ref[h].T
            else:
                st_ref[h] = jnp.zeros((HEAD_DIM, HEAD_DIM), F32)

    row = lax.broadcasted_iota(jnp.int32, (t, t), 0)
    col = lax.broadcasted_iota(jnp.int32, (t, t), 1)
    rowd = lax.broadcasted_iota(jnp.int32, (t, HEAD_DIM), 0)
    hits = [jnp.logical_and(row - col == dl, jnp.bitwise_and(row, SUB - 1) >= dl) for dl in range(SUB)]
    far = jnp.right_shift(col, 3) < jnp.right_shift(row, 3)

    pending = []
    for h in range(hb):
        sl = slice(h * HEAD_DIM, (h + 1) * HEAD_DIM)
        q = q_ref[:, sl]
        ff = f_ref[:, sl]
        v = i_ref[:, sl]

        lbp = lbp_ref[:, sl]
        e = jnp.exp(lbp - jnp.max(lbp, axis=0, keepdims=True))
        soft = e / jnp.sum(e, axis=0, keepdims=True)
        acc = soft[0:1]
        for r in range(1, layer + 1):
            acc = acc + soft[r:r + 1]
        lb = acc - soft[0:1]

        log_f = _logaddexp(jnp.log(lb), jnp.log1p(-lb) + _log_sigmoid(ff))
        k = (1.0 - lb) * jax.nn.sigmoid(-ff)

        b = log_f
        sh = 1
        while sh < t:
            b = b + jnp.where(rowd >= sh, pltpu.roll(b, sh, 0), 0.0)
            sh *= 2
        b_end = b[t - 1:t]

        st = st_ref[h]
        o = _nt_dot((q * jnp.exp(b)).astype(BF16), st.astype(BF16))

        sc = jnp.zeros((t, t), F32)
        for dl in range(SUB):
            if dl == 0:
                prod = q * k
            else:
                prod = q * pltpu.roll(k, dl, 0) * jnp.exp(jnp.minimum(b - pltpu.roll(b, dl, 0), 0.0))
            r = jnp.sum(prod, axis=1, keepdims=True)
            sc = jnp.where(hits[dl], r, sc)

        q_parts = []
        k_rows = []
        for i in range(nsub):
            b_ref = b[i * SUB + SUB - 1:i * SUB + SUB]
            q_parts.append((q * jnp.exp(jnp.minimum(b - b_ref, 0.0))).astype(BF16))
            kh = k[i * SUB:(i + 1) * SUB] * jnp.exp(jnp.minimum(b_ref - b[i * SUB:(i + 1) * SUB], 0.0))
            zero = jnp.zeros_like(kh)
            k_rows.append(jnp.concatenate([kh if j == i else zero for j in range(nsub)], axis=1))
        q_cat = jnp.concatenate(q_parts, axis=1)
        k_bd = jnp.concatenate(k_rows, axis=0).astype(BF16)
        sc_far = _nt_dot(q_cat, k_bd)
        scores = jnp.where(far, sc_far, sc)

        k_dec = (k * jnp.exp(jnp.minimum(b_end - b, 0.0))).astype(BF16)
        st_ref[h] = st * jnp.exp(b_end) + _tn_dot(v.astype(BF16), k_dec)
        pending.append((o, scores.astype(BF16)))

    for h in range(hb):
        sl = slice(h * HEAD_DIM, (h + 1) * HEAD_DIM)
        o, scores = pending[h]
        g = g_ref[:, sl]
        o = o + jnp.dot(scores, i_ref[:, sl].astype(BF16), preferred_element_type=F32)
        o = _rms(o, gh_ref[:, sl])
        o = o * (g * jax.nn.sigmoid(g))
        o_ref[:, sl] = o.astype(BF16)

    @pl.when(c == pl.num_programs(2) - 1)
    def _():
        for h in range(hb):
            sout_ref[h] = st_ref[h].T


def _hgrn(z3, lb_param, g_head, state, state_idx, layer, hb=HB_HGRN):
    assert SUB == 8
    b, s, _ = z3.shape
    t = min(CHUNK, s)
    nc = s // t
    ng = N_MIX_HEADS // hb
    w = hb * HEAD_DIM
    depth = lb_param.shape[0]
    has_state = state is not None

    def zspec(seg):
        return pl.BlockSpec((None, t, w), lambda i, g, c, seg=seg: (i, c, seg * ng + g))

    in_specs = [zspec(0), zspec(1), zspec(2), zspec(3),
                pl.BlockSpec((depth, w), lambda i, g, c: (0, g)),
                pl.BlockSpec((1, w), lambda i, g, c: (0, g))]
    args = [z3, z3, z3, z3, lb_param, g_head.reshape(1, MIX_W)]
    if has_state:
        in_specs.append(pl.BlockSpec((None, None, hb, HEAD_DIM, HEAD_DIM),
                                     lambda i, g, c: (state_idx, i, g, 0, 0)))
        args.append(state)
    return pl.pallas_call(
        functools.partial(_hgrn_kernel, hb=hb, t=t, layer=layer, has_state=has_state),
        grid=(b, ng, nc),
        in_specs=in_specs,
        out_specs=[
            pl.BlockSpec((None, t, w), lambda i, g, c: (i, c, g)),
            pl.BlockSpec((None, hb, HEAD_DIM, HEAD_DIM), lambda i, g, c: (i, g, 0, 0)),
        ],
        out_shape=[
            jax.ShapeDtypeStruct((b, s, MIX_W), BF16),
            jax.ShapeDtypeStruct((b, N_MIX_HEADS, HEAD_DIM, HEAD_DIM), F32),
        ],
        scratch_shapes=[pltpu.VMEM((hb, HEAD_DIM, HEAD_DIM), F32)],
        compiler_params=_cparams(("parallel", "parallel", "arbitrary")),
        name="hgrn2",
    )(*args)


def _out_proj_kernel(om_ref, oe_ref, wm_ref, we_ref, x_ref, g_ref, x1_ref, xnt_ref):
    x1 = x_ref[...] + jnp.dot(om_ref[...], wm_ref[...], preferred_element_type=F32)
    x1 = x1 + jnp.dot(oe_ref[...], we_ref[...], preferred_element_type=F32)
    x1_ref[...] = x1
    xnt_ref[...] = _rms(x1, g_ref[...]).T.astype(BF16)


def _out_proj(o_mix, o_mem, w_out, layer, x, g2):
    n, d = x.shape
    tm = min(512, n)
    once = pl.Buffered(1)
    return pl.pallas_call(
        _out_proj_kernel,
        grid=(n // tm,),
        in_specs=[
            pl.BlockSpec((tm, MIX_W), lambda i: (i, 0)),
            pl.BlockSpec((tm, MEM_W), lambda i: (i, 0)),
            pl.BlockSpec((None, MIX_W, d), lambda i: (layer, 0, 0), pipeline_mode=once),
            pl.BlockSpec((None, MEM_W, d), lambda i: (layer, MIX_W // MEM_W, 0), pipeline_mode=once),
            pl.BlockSpec((tm, d), lambda i: (i, 0)),
            pl.BlockSpec((1, d), lambda i: (0, 0)),
        ],
        out_specs=[
            pl.BlockSpec((tm, d), lambda i: (i, 0)),
            pl.BlockSpec((d, tm), lambda i: (0, i)),
        ],
        out_shape=[jax.ShapeDtypeStruct((n, d), F32), jax.ShapeDtypeStruct((d, n), BF16)],
        compiler_params=_cparams(("parallel",)),
        name="out_proj",
    )(o_mix, o_mem, w_out, w_out, x, g2.reshape(1, d))


_REMOVED = 2.0 ** 100


def _extract_exact(vals, order, rounds):
    rank = jnp.full(vals.shape, float(PEER_TOPK), F32)
    tops = []
    for r in range(rounds):
        m = jnp.max(vals, axis=0, keepdims=True)
        pick = jnp.min(jnp.where(vals == m, order, 1e9), axis=0, keepdims=True)
        sel = order == pick
        rank = jnp.where(sel, float(r), rank)
        vals = jnp.where(sel, -jnp.inf, vals)
        tops.append(m)
    return jnp.concatenate(tops, axis=0), rank


def _extract_fast(vals, rounds):
    tops = []
    for r in range(rounds):
        m = jnp.max(vals, axis=0, keepdims=True)
        vals = jnp.where(vals == m, -_REMOVED * (r + 1), vals)
        tops.append(m)
    rank = jnp.where(vals <= -_REMOVED, vals * (-1.0 / _REMOVED) - 1.0, float(PEER_TOPK))
    return jnp.concatenate(tops, axis=0), rank


def _route_lanes(s1, s2, exact):
    lanes = s1.shape[1]
    if exact:
        key_id = lax.broadcasted_iota(jnp.int32, (N_KEYS, lanes), 0).astype(F32)
        top1, rank1 = _extract_exact(s1, key_id, PEER_TOPK)
        top2, rank2 = _extract_exact(s2, key_id, PEER_TOPK)
    else:
        top1, rank1 = _extract_fast(s1, PEER_TOPK)
        top2, rank2 = _extract_fast(s2, PEER_TOPK)

    blk_a = top1[0:1] + top2
    blk_b = [top1[a:a + 1] + top2[0:8] for a in range(1, 8)]
    blk_c = top1[8:16] + top2[0:1]
    cand = jnp.concatenate([blk_a] + blk_b + [blk_c], axis=0)
    nrow = cand.shape[0]
    r = lax.broadcasted_iota(jnp.int32, (nrow, lanes), 0)
    a_mid = 1 + jnp.right_shift(r - 16, 3)
    b_mid = jnp.bitwise_and(r - 16, 7)
    valid = jnp.logical_or(jnp.logical_or(r < 16, r >= 72), (a_mid + 1) * (b_mid + 1) <= PEER_TOPK)
    cand = jnp.where(valid, cand, -jnp.inf)
    if exact:
        cand_id = jnp.where(r < 16, r, jnp.where(r < 72, a_mid * PEER_TOPK + b_mid, (r - 64) * PEER_TOPK))
        best, crank = _extract_exact(cand, cand_id.astype(F32), PEER_TOPK)
    else:
        best, crank = _extract_fast(cand, PEER_TOPK)
    chosen = jnp.logical_and(valid, crank < PEER_TOPK).astype(F32)
    z = jnp.sum(jnp.exp(best - best[0:1]), axis=0, keepdims=True)

    counts = [jnp.sum(chosen[0:16], axis=0, keepdims=True)]
    counts += [jnp.sum(chosen[16 + 8 * (a - 1):16 + 8 * a], axis=0, keepdims=True) for a in range(1, 8)]
    counts += [chosen[72 + a - 8:72 + a - 7] for a in range(8, 16)]
    c1 = jnp.zeros((N_KEYS, lanes), F32)
    for a in range(PEER_TOPK):
        c1 = jnp.where(rank1 == a, counts[a], c1)

    ranked = (jnp.sum((rank1 < PEER_TOPK).astype(F32), axis=0, keepdims=True)
              + jnp.sum((rank2 < PEER_TOPK).astype(F32), axis=0, keepdims=True)
              + jnp.sum(chosen, axis=0, keepdims=True))
    low = jnp.minimum(jnp.min(s1, axis=0, keepdims=True), jnp.min(s2, axis=0, keepdims=True))
    redo = jnp.logical_or(ranked != 3.0 * PEER_TOPK, jnp.logical_not(low > -0.25 * _REMOVED)).astype(F32)
    e1 = jnp.exp(s1 - top1[0:1])
    e2 = jnp.exp(s2 - top2[0:1]) / z
    return rank2, e2, c1, e1, redo


def _scores_kernel(xnt_ref, wq_ref, keys_ref, s_ref):
    qq = jnp.dot(wq_ref[...], xnt_ref[...], preferred_element_type=F32).astype(BF16)
    for h in range(PEER_HEADS):
        for c in range(2):
            row = (2 * h + c) * N_KEYS
            s_ref[2 * h + c] = jnp.dot(keys_ref[c, h], qq[row:row + N_KEYS], preferred_element_type=F32)


def _peer_scores(xnt, w_pq_t, keys, layer):
    d, n = xnt.shape
    tb = min(512, n)
    return pl.pallas_call(
        _scores_kernel,
        grid=(n // tb,),
        in_specs=[
            pl.BlockSpec((d, tb), lambda i: (0, i)),
            pl.BlockSpec((None,) + w_pq_t.shape[1:], lambda i: (layer, 0, 0)),
            pl.BlockSpec((None,) + keys.shape[1:], lambda i: (layer, 0, 0, 0, 0)),
        ],
        out_specs=pl.BlockSpec((2 * PEER_HEADS, N_KEYS, tb), lambda i: (0, 0, i)),
        out_shape=jax.ShapeDtypeStruct((2 * PEER_HEADS, N_KEYS, n), F32),
        compiler_params=_cparams(("parallel",)),
        name="peer_scores",
    )(xnt, w_pq_t, keys)


def _route_kernel(s_ref, r2_ref, e2_ref, c1_ref, e1_ref):
    def emit(exact):
        rank2, e2, c1, e1, redo = _route_lanes(s_ref[0], s_ref[1], exact)
        r2_ref[...] = rank2.astype(BF16)
        e2_ref[...] = e2.astype(BF16)
        c1_ref[...] = c1
        e1_ref[...] = e1
        return redo

    redo = emit(exact=False)

    @pl.when(jnp.max(redo) > 0.0)
    def _():
        emit(exact=True)


def _peer_route(scores):
    n = scores.shape[2]
    tb = min(TB_ROUTE, n)
    out = jax.ShapeDtypeStruct((PEER_HEADS, N_KEYS, n), F32)
    out16 = jax.ShapeDtypeStruct((PEER_HEADS, N_KEYS, n), BF16)
    ospec = pl.BlockSpec((None, N_KEYS, tb), lambda i, h: (h, 0, i))
    return pl.pallas_call(
        _route_kernel,
        grid=(n // tb, PEER_HEADS),
        in_specs=[pl.BlockSpec((2, N_KEYS, tb), lambda i, h: (h, 0, i))],
        out_specs=[ospec, ospec, ospec, ospec],
        out_shape=[out16, out16, out, out],
        compiler_params=_cparams(("parallel", "parallel")),
        name="peer_route",
    )(scores)


def _gelu(x):
    return 0.5 * x * (1.0 + lax.erf(x * (0.5 ** 0.5)))


def _peer_kernel(xnt_ref, u_ref, un_ref, vt_ref, r2_ref, e2_ref, c1_ref, e1_ref, y_ref, a_ref, coef_ref, *,
                 eb, ck, lc):
    e = pl.program_id(1)
    tb = xnt_ref.shape[1]
    per = ck // N_KEYS
    nck = eb // ck

    @pl.when(e == 0)
    def _():
        y_ref[...] = jnp.zeros_like(y_ref)
        a_ref[0] = jnp.dot(u_ref[0:ck, :], xnt_ref[...], preferred_element_type=F32)

    for k in range(nck):
        if k + 1 < nck:
            a_next = jnp.dot(u_ref[(k + 1) * ck:(k + 2) * ck, :], xnt_ref[...], preferred_element_type=F32)
            a_ref[k + 1] = a_next
        else:
            a_next = jnp.dot(un_ref[...], xnt_ref[...], preferred_element_type=F32)
        bits = pltpu.bitcast(a_next[0:1, tb - lc:], jnp.uint32)
        bits = lax.shift_right_logical(lax.shift_right_logical(bits, jnp.uint32(16)), jnp.uint32(16))
        zero = pltpu.bitcast(bits, F32)
        for ii in range(per):
            i = k * per + ii
            rows = slice(ii * N_KEYS, (ii + 1) * N_KEYS)
            for l0 in range(0, tb, lc):
                cols = slice(l0, l0 + lc)
                gate = jnp.zeros((N_KEYS, lc), BF16)
                for h in range(PEER_HEADS):
                    c1 = jnp.broadcast_to(c1_ref[h, i:i + 1, cols].astype(BF16), (N_KEYS, lc))
                    e1 = jnp.broadcast_to(e1_ref[h, i:i + 1, cols].astype(BF16), (N_KEYS, lc))
                    zero16 = jnp.zeros((N_KEYS, lc), BF16)
                    gate = gate + jnp.where(r2_ref[h, :, cols] < c1, e2_ref[h, :, cols], zero16) * e1
                act = gate.astype(F32) * _gelu(a_ref[k, rows, cols])
                if ii == per - 1 and l0 + lc == tb:
                    act = act + zero
                coef_ref[k, rows, cols] = act.astype(BF16)
        if k + 1 == nck:
            a_ref[0] = a_next
        y_ref[...] += jnp.dot(vt_ref[:, k * ck:(k + 1) * ck], coef_ref[k], preferred_element_type=F32)


def _peer_sweep(xnt, u, vt, layer, r2, e2, c1, e1):
    d, n = xnt.shape
    tb = min(TB_PEER, n)
    eb = EB_PEER
    ck = CK_PEER
    lc = min(LC_PEER, tb)
    rspec = pl.BlockSpec((PEER_HEADS, N_KEYS, tb), lambda t, e: (0, 0, t))
    ispec = pl.BlockSpec((PEER_HEADS, eb // N_KEYS, tb), lambda t, e: (0, e, t))
    last_ck = N_EXPERTS // ck - 1
    return pl.pallas_call(
        functools.partial(_peer_kernel, eb=eb, ck=ck, lc=lc),
        grid=(n // tb, N_EXPERTS // eb),
        in_specs=[
            pl.BlockSpec((d, tb), lambda t, e: (0, t)),
            pl.BlockSpec((None, eb, d), lambda t, e: (layer, e, 0)),
            pl.BlockSpec((None, ck, d), lambda t, e: (layer, jnp.minimum((e + 1) * (eb // ck), last_ck), 0)),
            pl.BlockSpec((None, d, eb), lambda t, e: (layer, 0, e)),
            rspec, rspec, ispec, ispec,
        ],
        out_specs=pl.BlockSpec((d, tb), lambda t, e: (0, t)),
        out_shape=jax.ShapeDtypeStruct((d, n), F32),
        scratch_shapes=[pltpu.VMEM((eb // ck, ck, tb), F32), pltpu.VMEM((eb // ck, ck, tb), BF16)],
        compiler_params=_cparams(("parallel", "arbitrary")),
        name="peer_sweep",
    )(xnt, u, u, vt, r2, e2, c1, e1)


def _add_t_kernel(x_ref, yt_ref, g_ref, o_ref, *, final):
    x = x_ref[...] + yt_ref[...].T
    if final:
        x = _rms(x, g_ref[...])
    o_ref[...] = x


def _add_transposed(x1, yt, g, final):
    n, d = x1.shape
    tm = min(256, n)
    return pl.pallas_call(
        functools.partial(_add_t_kernel, final=final),
        grid=(n // tm,),
        in_specs=[
            pl.BlockSpec((tm, d), lambda i: (i, 0)),
            pl.BlockSpec((d, tm), lambda i: (0, i)),
            pl.BlockSpec((1, d), lambda i: (0, 0)),
        ],
        out_specs=pl.BlockSpec((tm, d), lambda i: (i, 0)),
        out_shape=jax.ShapeDtypeStruct((n, d), F32),
        compiler_params=_cparams(("parallel",)),
        name="peer_residual",
    )(x1, yt, g.reshape(1, d))


def _trunk(x, mem, k_cache, v_cache, s_cache, mk_cache, mv_cache, p):
    prompt = mem is not None
    bsz, t, d = x.shape
    n = bsz * t
    depth = p["g_norm1"].shape[0]
    x2 = x.reshape(n, d)
    new_k, new_v, new_s, new_mk, new_mv = [], [], [], [], []
    for i in range(depth):
        j = i // 2
        attn = i % 2 == 0
        w_in = p["w_in_a"] if attn else p["w_in_b"]
        w_out = p["w_out_a"] if attn else p["w_out_b"]
        tm = min(TM_PROJ, t)
        pad = -(-WINDOW // tm) * tm if (attn and prompt) else 0
        z = _norm_matmul(x2, p["g_norm1"][i], w_in, j, seq=t, pad=pad)
        z3 = z.reshape(bsz, pad + t, z.shape[1])
        mq_blk = (z.shape[1] - MEM_W) // MEM_W

        if prompt:
            mkv = _norm_matmul(mem.reshape(bsz * N_MEM, d), p["g_mem"][i], p["w_mem_kv"], i)
            new_mk.append(mkv[:, :MEM_W].reshape(bsz, N_MEM, N_MEM_HEADS, HEAD_DIM))
            new_mv.append(mkv[:, MEM_W:].reshape(bsz, N_MEM, N_MEM_HEADS, HEAD_DIM))
            mkv3 = mkv.reshape(bsz, N_MEM, 2 * MEM_W)
            kspec = pl.BlockSpec((None, N_MEM, MEM_W), lambda b, q: (b, 0, 0))
            vspec = pl.BlockSpec((None, N_MEM, MEM_W), lambda b, q: (b, 0, 1))
            o_mem = _mem_attention(z3, mq_blk, pad, t, mkv3, kspec, mkv3, vspec, head_major=False)
        else:
            cspec = pl.BlockSpec((None, None, N_MEM_HEADS, N_MEM, HEAD_DIM), lambda b, q, i=i: (i, b, 0, 0, 0))
            o_mem = _mem_attention(z3, mq_blk, pad, t, jnp.transpose(mk_cache, (0, 1, 3, 2, 4)), cspec,
                                   jnp.transpose(mv_cache, (0, 1, 3, 2, 4)), cspec, head_major=True)

        if attn:
            k_new = z3[:, pad:, MIX_W:2 * MIX_W]
            v_new = z3[:, pad:, 2 * MIX_W:3 * MIX_W]
            if prompt:
                bias = _band_bias(p["rel_bias"][j], CHUNK, WINDOW + CHUNK, WINDOW)
                o_mix = _band_attention_prompt(z3, pad, bias)
                keep = min(WINDOW, t)
                k_new, v_new = k_new[:, t - keep:], v_new[:, t - keep:]
            else:
                p_len = k_cache.shape[2]
                bias = _band_bias(p["rel_bias"][j], t, p_len + t, p_len)
                o_mix = _band_attention_sample(z3, k_cache, v_cache, j, bias)
            new_k.append(k_new.reshape(bsz, -1, N_MIX_HEADS, HEAD_DIM))
            new_v.append(v_new.reshape(bsz, -1, N_MIX_HEADS, HEAD_DIM))
        else:
            o_mix, s_new = _hgrn(z3, p["lb_param"], p["g_hgrn"][j], None if prompt else s_cache, j, i)
            new_s.append(s_new)

        x1, xnt = _out_proj(o_mix.reshape(n, MIX_W), o_mem.reshape(n, MEM_W), w_out, j, x2, p["g_norm2"][i])
        r2, e2, c1, e1 = _peer_route(_peer_scores(xnt, p["w_pq_t"], p["peer_keys"], i))
        yt = _peer_sweep(xnt, p["peer_u"], p["peer_v_t"], i, r2, e2, c1, e1)
        x2 = _add_transposed(x1, yt, p["g_final"], final=(i == depth - 1))

    y = x2.reshape(bsz, t, d)
    mk_out = jnp.stack(new_mk) if new_mk else None
    mv_out = jnp.stack(new_mv) if new_mv else None
    return y, jnp.stack(new_k), jnp.stack(new_v), jnp.stack(new_s), mk_out, mv_out


def kernel(x_prompt, x_sample, cache_attn_k, cache_attn_v, state_hgrn, cache_mem_k, cache_mem_v, mem_prompt,
           g_norm1, g_norm2, g_mem, g_final, w_in_a, w_out_a, rel_bias, w_in_b, w_out_b, lb_param, g_hgrn,
           w_mem_kv, w_pq, peer_keys, peer_u, peer_v):
    p = dict(
        g_norm1=g_norm1, g_norm2=g_norm2, g_mem=g_mem, g_final=g_final, rel_bias=rel_bias,
        lb_param=lb_param, g_hgrn=g_hgrn,
        w_in_a=w_in_a.astype(BF16), w_out_a=w_out_a.astype(BF16),
        w_in_b=w_in_b.astype(BF16), w_out_b=w_out_b.astype(BF16),
        w_mem_kv=w_mem_kv.astype(BF16),
        w_pq_t=jnp.swapaxes(w_pq, 1, 2).astype(BF16),
        peer_keys=peer_keys.astype(BF16),
        peer_u=peer_u.astype(BF16),
        peer_v_t=jnp.swapaxes(peer_v, 1, 2).astype(BF16),
    )
    y_p, k_p, v_p, s_p, mk_p, mv_p = _trunk(x_prompt, mem_prompt, None, None, None, None, None, p)
    y_s, k_s, v_s, s_s, _, _ = _trunk(x_sample, None, cache_attn_k, cache_attn_v, state_hgrn,
                                      cache_mem_k, cache_mem_v, p)
    return (y_p, y_s, k_p, v_p, k_s, v_s, s_p, s_s, mk_p, mv_p)
```

```python
import functools
import math

import jax
import jax.numpy as jnp
from jax import lax
from jax.experimental import pallas as pl
from jax.experimental.pallas import tpu as pltpu

F32 = jnp.float32
BF16 = jnp.bfloat16

D_MODEL = 2048
PAST_LEN = 1024
CHUNK = 64
LEFT_CHUNKS = 8
WINDOW = LEFT_CHUNKS * CHUNK
HEAD_DIM = 128
N_MIX_HEADS = 12
N_MEM_HEADS = 4
MIX_W = N_MIX_HEADS * HEAD_DIM
MEM_W = N_MEM_HEADS * HEAD_DIM
N_MEM = 256
REL_CLIP = 256
PEER_HEADS = 8
N_KEYS = 128
N_EXPERTS = N_KEYS * N_KEYS
PEER_TOPK = 16
EPS = 1e-6
NEG_INF = -1e30

LANES = 128
SUBLANES = 8
SUB = SUBLANES
VMEM_LIMIT = 56 * 1024 * 1024

TM_PROJ = 1024
TN_PROJ = 1664
TB_ROUTE = 256
HR_ROUTE = 2
TB_PEER = 512
EB_PEER = 1024
CK_PEER = 256
LC_PEER = 256
HB_MIX = 4
QB_ATTN = 256
HB_HGRN = 12


def _cparams(sem, flags=None):
    return pltpu.CompilerParams(dimension_semantics=sem, vmem_limit_bytes=VMEM_LIMIT, flags=flags)


def _rms(x, g):
    return x * lax.rsqrt(jnp.mean(x * x, axis=-1, keepdims=True) + EPS) * g


def _norm_matmul_kernel(x_ref, g_ref, w_ref, o_ref, hn_ref, *, blocks_per_seq, pad_blocks):
    live = lax.rem(pl.program_id(0), blocks_per_seq) >= pad_blocks

    @pl.when(jnp.logical_and(live, pl.program_id(1) == 0))
    def _():
        hn_ref[...] = _rms(x_ref[...], g_ref[...]).astype(BF16)

    @pl.when(live)
    def _():
        o_ref[...] = jnp.dot(hn_ref[...], w_ref[...], preferred_element_type=F32)

    @pl.when(jnp.logical_not(live))
    def _():
        o_ref[...] = jnp.zeros_like(o_ref)


def _norm_matmul(x, g, w, layer, seq=None, pad=0):
    m, d = x.shape
    n = w.shape[2]
    tm = min(TM_PROJ, m) if pad == 0 else math.gcd(math.gcd(pad, seq), TM_PROJ)
    tn = n // -(-n // TN_PROJ)
    assert n % tn == 0 and tn % LANES == 0
    seq = m if pad == 0 else seq
    pad_blocks = pad // tm
    data_blocks = seq // tm
    blocks_per_seq = data_blocks + pad_blocks
    n_seq = m // seq

    def x_map(i, j):
        return ((i // blocks_per_seq) * data_blocks + jnp.maximum(lax.rem(i, blocks_per_seq) - pad_blocks, 0), 0)

    return pl.pallas_call(
        functools.partial(_norm_matmul_kernel, blocks_per_seq=blocks_per_seq, pad_blocks=pad_blocks),
        grid=(n_seq * blocks_per_seq, n // tn),
        in_specs=[
            pl.BlockSpec((tm, d), x_map),
            pl.BlockSpec((1, d), lambda i, j: (0, 0)),
            pl.BlockSpec((None, d, tn), lambda i, j: (layer, 0, j)),
        ],
        out_specs=pl.BlockSpec((tm, tn), lambda i, j: (i, j)),
        out_shape=jax.ShapeDtypeStruct((n_seq * blocks_per_seq * tm, n), F32),
        scratch_shapes=[pltpu.VMEM((tm, d), BF16)],
        compiler_params=_cparams(("parallel", "arbitrary")),
        name="norm_matmul",
    )(x, g.reshape(1, d), w)


def _bias_kernel(tab_ref, o_ref, *, tq, tk, off, hg):
    h0 = pl.program_id(0) * hg
    clip = lambda d: max(min(d, REL_CLIP), -REL_CLIP) + REL_CLIP
    for c0 in range(0, tk, LANES):
        w = min(LANES, tk - c0)
        qi = lax.broadcasted_iota(jnp.int32, (tq, w), 0)
        ki = lax.broadcasted_iota(jnp.int32, (tq, w), 1) + c0
        rel = jnp.clip(qi - ki + off, -REL_CLIP, REL_CLIP) + REL_CLIP
        lo = clip(-(c0 + w - 1) + off)
        hi = clip((tq - 1) - c0 + off)

        def body(r, accs, rel=rel):
            hit = rel == r
            return tuple(jnp.where(hit, tab_ref[h0 + h, r], acc) for h, acc in enumerate(accs))

        accs = lax.fori_loop(lo, hi + 1, body, tuple(jnp.zeros((tq, w), F32) for _ in range(hg)))
        for h in range(hg):
            o_ref[h, :, c0:c0 + w] = accs[h]


def _band_bias(table, tq, tk, off, hg=4):
    nh = table.shape[1]
    return pl.pallas_call(
        functools.partial(_bias_kernel, tq=tq, tk=tk, off=off, hg=hg),
        grid=(nh // hg,),
        in_specs=[pl.BlockSpec(memory_space=pltpu.SMEM)],
        out_specs=pl.BlockSpec((hg, tq, tk), lambda g: (g, 0, 0)),
        out_shape=jax.ShapeDtypeStruct((nh, tq, tk), F32),
        compiler_params=_cparams(("arbitrary",)),
        name="band_bias",
    )(table.T)


def _softmax_rows(s):
    m = jnp.max(s, axis=-1, keepdims=True)
    p = jnp.exp(s - m)
    return p / jnp.sum(p, axis=-1, keepdims=True)


def _nt_dot(a, b):
    return lax.dot_general(a, b, (((1,), (1,)), ((), ())), preferred_element_type=F32)


def _tn_dot(a, b):
    return lax.dot_general(a, b, (((0,), (0,)), ((), ())), preferred_element_type=F32)


def _band_prompt_kernel(*refs, hb, qb, nwb):
    q_ref = refs[0]
    k_refs = refs[1:1 + nwb]
    v_refs = refs[1 + nwb:1 + 2 * nwb]
    bias_ref, o_ref = refs[1 + 2 * nwb:]
    c = pl.program_id(2)
    tk = WINDOW + qb
    k_pos = lax.broadcasted_iota(jnp.int32, (qb, tk), 1) + (c * qb - WINDOW)
    live = k_pos >= 0
    scale = HEAD_DIM ** -0.5
    scores = []
    for h in range(hb):
        sl = slice(h * HEAD_DIM, (h + 1) * HEAD_DIM)
        kw = jnp.concatenate([r[:, sl].astype(BF16) for r in k_refs], axis=0)
        scores.append(_nt_dot(q_ref[:, sl].astype(BF16), kw))
    for h in range(hb):
        sl = slice(h * HEAD_DIM, (h + 1) * HEAD_DIM)
        vw = jnp.concatenate([r[:, sl].astype(BF16) for r in v_refs], axis=0)
        s = scores[h] * scale + bias_ref[h]
        s = jnp.where(live, s, NEG_INF)
        p = _softmax_rows(s).astype(BF16)
        o_ref[:, sl] = jnp.dot(p, vw, preferred_element_type=F32).astype(BF16)


def _band_attention_prompt(zp3, pad, bias, hb=N_MIX_HEADS):
    b, sp, _ = zp3.shape
    s = sp - pad
    qb = min(QB_ATTN, s)
    assert WINDOW % qb == 0 and (pad - WINDOW) % qb == 0
    nqc = qb // CHUNK
    nwb = WINDOW // qb + 1
    blk0 = (pad - WINDOW) // qb
    ng = N_MIX_HEADS // hb
    w = hb * HEAD_DIM
    tk = WINDOW + qb

    def win_spec(j, seg):
        return pl.BlockSpec((None, qb, w), lambda i, g, c, j=j, seg=seg: (i, c + blk0 + j, seg * ng + g))
    bias_blk = jnp.concatenate(
        [jnp.pad(bias, ((0, 0), (0, 0), (j * CHUNK, (nqc - 1 - j) * CHUNK)), constant_values=NEG_INF)
         for j in range(nqc)], axis=1)
    return pl.pallas_call(
        functools.partial(_band_prompt_kernel, hb=hb, qb=qb, nwb=nwb),
        grid=(b, ng, s // qb),
        in_specs=[
            win_spec(nwb - 1, 0),
            *[win_spec(j, 1) for j in range(nwb)],
            *[win_spec(j, 2) for j in range(nwb)],
            pl.BlockSpec((hb, qb, tk), lambda i, g, c: (g, 0, 0), pipeline_mode=pl.Buffered(1)),
        ],
        out_specs=pl.BlockSpec((None, qb, w), lambda i, g, c: (i, c, g)),
        out_shape=jax.ShapeDtypeStruct((b, s, MIX_W), BF16),
        compiler_params=_cparams(("parallel", "parallel", "arbitrary")),
        name="band_attn_prompt",
    )(*([zp3] * (1 + 2 * nwb)), bias_blk)


def _band_sample_kernel(*refs, hb, p_len):
    q_ref, kn_ref, vn_ref = refs[:3]
    kc_refs = refs[3:3 + hb]
    vc_refs = refs[3 + hb:3 + 2 * hb]
    bias_ref, o_ref = refs[3 + 2 * hb:]
    scale = HEAD_DIM ** -0.5
    scores = []
    for h in range(hb):
        sl = slice(h * HEAD_DIM, (h + 1) * HEAD_DIM)
        q = q_ref[:, sl].astype(BF16)
        scores.append((_nt_dot(q, kc_refs[h][...].astype(BF16)), _nt_dot(q, kn_ref[:, sl].astype(BF16))))
    for h in range(hb):
        sl = slice(h * HEAD_DIM, (h + 1) * HEAD_DIM)
        s_c = scores[h][0] * scale + bias_ref[h, :, :p_len]
        s_n = scores[h][1] * scale + bias_ref[h, :, p_len:]
        m = jnp.maximum(jnp.max(s_c, axis=-1, keepdims=True), jnp.max(s_n, axis=-1, keepdims=True))
        e_c = jnp.exp(s_c - m)
        e_n = jnp.exp(s_n - m)
        l = jnp.sum(e_c, axis=-1, keepdims=True) + jnp.sum(e_n, axis=-1, keepdims=True)
        o = jnp.dot((e_c / l).astype(BF16), vc_refs[h][...].astype(BF16), preferred_element_type=F32)
        o = o + jnp.dot((e_n / l).astype(BF16), vn_ref[:, sl].astype(BF16), preferred_element_type=F32)
        o_ref[:, sl] = o.astype(BF16)


def _band_attention_sample(z3, k_cache, v_cache, layer, bias, hb=N_MIX_HEADS):
    b, t, _ = z3.shape
    p_len = k_cache.shape[2]
    ng = N_MIX_HEADS // hb
    w = hb * HEAD_DIM
    k_cache = jnp.transpose(k_cache, (0, 1, 3, 2, 4))
    v_cache = jnp.transpose(v_cache, (0, 1, 3, 2, 4))

    def cache_spec(h):
        return pl.BlockSpec((None, None, None, p_len, HEAD_DIM), lambda i, g, h=h: (layer, i, g * hb + h, 0, 0))

    cache_specs = [cache_spec(h) for h in range(hb)]
    return pl.pallas_call(
        functools.partial(_band_sample_kernel, hb=hb, p_len=p_len),
        grid=(b, ng),
        in_specs=[
            pl.BlockSpec((None, t, w), lambda i, g: (i, 0, g)),
            pl.BlockSpec((None, t, w), lambda i, g: (i, 0, ng + g)),
            pl.BlockSpec((None, t, w), lambda i, g: (i, 0, 2 * ng + g)),
            *cache_specs, *cache_specs,
            pl.BlockSpec((hb, t, p_len + t), lambda i, g: (g, 0, 0)),
        ],
        out_specs=pl.BlockSpec((None, t, w), lambda i, g: (i, 0, g)),
        out_shape=jax.ShapeDtypeStruct((b, t, MIX_W), BF16),
        compiler_params=_cparams(("parallel", "parallel")),
        name="band_attn_sample",
    )(z3, z3, z3, *([k_cache] * hb), *([v_cache] * hb), bias)


def _mem_attn_kernel(q_ref, k_ref, v_ref, o_ref, *, head_major):
    scores = []
    for h in range(N_MEM_HEADS):
        sl = slice(h * HEAD_DIM, (h + 1) * HEAD_DIM)
        k = k_ref[h] if head_major else k_ref[:, sl]
        scores.append(_nt_dot(q_ref[:, sl].astype(BF16), k.astype(BF16)))
    for h in range(N_MEM_HEADS):
        sl = slice(h * HEAD_DIM, (h + 1) * HEAD_DIM)
        v = v_ref[h] if head_major else v_ref[:, sl]
        s = scores[h] * (HEAD_DIM ** -0.5)
        p = _softmax_rows(s).astype(BF16)
        o_ref[:, sl] = jnp.dot(p, v.astype(BF16), preferred_element_type=F32).astype(BF16)


def _mem_attention(z3, q_blk, row0, t, karr, kspec, varr, vspec, head_major):
    b = z3.shape[0]
    tq = min(t, 512)
    return pl.pallas_call(
        functools.partial(_mem_attn_kernel, head_major=head_major),
        grid=(b, t // tq),
        in_specs=[
            pl.BlockSpec((None, tq, MEM_W), lambda i, j: (i, j + row0 // tq, q_blk)),
            kspec, vspec,
        ],
        out_specs=pl.BlockSpec((None, tq, MEM_W), lambda i, j: (i, j, 0)),
        out_shape=jax.ShapeDtypeStruct((b, t, MEM_W), BF16),
        compiler_params=_cparams(("parallel", "arbitrary")),
        name="mem_attn",
    )(z3, karr, varr)


def _hgrn_kernel(*refs, hb, t, layer, has_state):
    if has_state:
        q_ref, f_ref, i_ref, g_ref, lbp_ref, gh_ref, s0_ref, o_ref, sout_ref, st_ref = refs
    else:
        q_ref, f_ref, i_ref, g_ref, lbp_ref, gh_ref, o_ref, sout_ref, st_ref = refs
    c = pl.program_id(2)
    nsub = t // SUB

    @pl.when(c == 0)
    def _():
        for h in range(hb):
            if has_state:
                st_ref[h] = s0_ref[h].T
            else:
                st_ref[h] = jnp.zeros((HEAD_DIM, HEAD_DIM), F32)

    row = lax.broadcasted_iota(jnp.int32, (t, t), 0)
    col = lax.broadcasted_iota(jnp.int32, (t, t), 1)
    rowd = lax.broadcasted_iota(jnp.int32, (t, HEAD_DIM), 0)
    hits = [jnp.logical_and(row - col == dl, jnp.bitwise_and(row, SUB - 1) >= dl) for dl in range(SUB)]
    far = jnp.right_shift(col, 3) < jnp.right_shift(row, 3)

    pending = []
    for h in range(hb):
        sl = slice(h * HEAD_DIM, (h + 1) * HEAD_DIM)
        q = q_ref[:, sl]
        ff = f_ref[:, sl]
        v = i_ref[:, sl]

        lbp = lbp_ref[:, sl]
        e = jnp.exp(lbp - jnp.max(lbp, axis=0, keepdims=True))
        soft = e / jnp.sum(e, axis=0, keepdims=True)
        acc = soft[0:1]
        for r in range(1, layer + 1):
            acc = acc + soft[r:r + 1]
        lb = acc - soft[0:1]

        en = jnp.exp(-jnp.abs(ff))
        rec = 1.0 / (1.0 + en)
        sig_pos = jnp.where(ff >= 0.0, rec, en * rec)
        sig_neg = jnp.where(ff >= 0.0, en * rec, rec)
        f = lb + (1.0 - lb) * sig_pos
        log_f = jnp.where(f > 0.0, jnp.log(f), ff)
        k = (1.0 - lb) * sig_neg

        b = log_f
        sh = 1
        while sh < t:
            b = b + jnp.where(rowd >= sh, pltpu.roll(b, sh, 0), 0.0)
            sh *= 2
        b_end = b[t - 1:t]

        st = st_ref[h]
        o = _nt_dot((q * jnp.exp(b)).astype(BF16), st.astype(BF16))

        sc = jnp.zeros((t, t), F32)
        for dl in range(SUB):
            if dl == 0:
                prod = q * k
            else:
                prod = q * pltpu.roll(k, dl, 0) * jnp.exp(jnp.minimum(b - pltpu.roll(b, dl, 0), 0.0))
            r = jnp.sum(prod, axis=1, keepdims=True)
            sc = jnp.where(hits[dl], r, sc)

        q_parts = []
        k_rows = []
        for i in range(nsub - 1):
            lo = (i + 1) * SUB
            b_ref = b[lo - 1:lo]
            later = q[lo:] * jnp.exp(jnp.minimum(b[lo:] - b_ref, 0.0))
            q_parts.append(jnp.concatenate([jnp.zeros((lo, HEAD_DIM), F32), later], axis=0).astype(BF16))
            kh = k[lo - SUB:lo] * jnp.exp(jnp.minimum(b_ref - b[lo - SUB:lo], 0.0))
            zero = jnp.zeros_like(kh)
            k_rows.append(jnp.concatenate([kh if j == i else zero for j in range(nsub - 1)], axis=1))
        k_rows.append(jnp.zeros((SUB, (nsub - 1) * HEAD_DIM), F32))
        q_cat = jnp.concatenate(q_parts, axis=1)
        k_bd = jnp.concatenate(k_rows, axis=0).astype(BF16)
        sc_far = _nt_dot(q_cat, k_bd)
        scores = jnp.where(far, sc_far, sc)

        k_dec = (k * jnp.exp(jnp.minimum(b_end - b, 0.0))).astype(BF16)
        st_ref[h] = st * jnp.exp(b_end) + _tn_dot(v.astype(BF16), k_dec)
        pending.append((o, scores.astype(BF16)))

    for h in range(hb):
        sl = slice(h * HEAD_DIM, (h + 1) * HEAD_DIM)
        o, scores = pending[h]
        g = g_ref[:, sl]
        o = o + jnp.dot(scores, i_ref[:, sl].astype(BF16), preferred_element_type=F32)
        o = _rms(o, gh_ref[:, sl])
        o = o * (g * jax.nn.sigmoid(g))
        o_ref[:, sl] = o.astype(BF16)

    @pl.when(c == pl.num_programs(2) - 1)
    def _():
        for h in range(hb):
            sout_ref[h] = st_ref[h].T


def _hgrn(z3, lb_param, g_head, state, state_idx, layer, hb=HB_HGRN):
    assert SUB == 8
    b, s, _ = z3.shape
    t = min(CHUNK, s)
    nc = s // t
    ng = N_MIX_HEADS // hb
    w = hb * HEAD_DIM
    depth = lb_param.shape[0]
    has_state = state is not None

    def zspec(seg):
        return pl.BlockSpec((None, t, w), lambda i, g, c, seg=seg: (i, c, seg * ng + g))

    in_specs = [zspec(0), zspec(1), zspec(2), zspec(3),
                pl.BlockSpec((depth, w), lambda i, g, c: (0, g)),
                pl.BlockSpec((1, w), lambda i, g, c: (0, g))]
    args = [z3, z3, z3, z3, lb_param, g_head.reshape(1, MIX_W)]
    if has_state:
        in_specs.append(pl.BlockSpec((None, None, hb, HEAD_DIM, HEAD_DIM),
                                     lambda i, g, c: (state_idx, i, g, 0, 0)))
        args.append(state)
    return pl.pallas_call(
        functools.partial(_hgrn_kernel, hb=hb, t=t, layer=layer, has_state=has_state),
        grid=(b, ng, nc),
        in_specs=in_specs,
        out_specs=[
            pl.BlockSpec((None, t, w), lambda i, g, c: (i, c, g)),
            pl.BlockSpec((None, hb, HEAD_DIM, HEAD_DIM), lambda i, g, c: (i, g, 0, 0)),
        ],
        out_shape=[
            jax.ShapeDtypeStruct((b, s, MIX_W), BF16),
            jax.ShapeDtypeStruct((b, N_MIX_HEADS, HEAD_DIM, HEAD_DIM), F32),
        ],
        scratch_shapes=[pltpu.VMEM((hb, HEAD_DIM, HEAD_DIM), F32)],
        compiler_params=_cparams(("parallel", "parallel", "arbitrary")),
        name="hgrn2",
    )(*args)


def _out_proj_kernel(om_ref, oe_ref, wm_ref, we_ref, x_ref, g_ref, x1_ref, xnt_ref):
    x1 = x_ref[...] + jnp.dot(om_ref[...], wm_ref[...], preferred_element_type=F32)
    x1 = x1 + jnp.dot(oe_ref[...], we_ref[...], preferred_element_type=F32)
    x1_ref[...] = x1
    xnt_ref[...] = _rms(x1, g_ref[...]).T.astype(BF16)


def _out_proj(o_mix, o_mem, w_out, layer, x, g2):
    n, d = x.shape
    tm = min(512, n)
    once = pl.Buffered(1)
    return pl.pallas_call(
        _out_proj_kernel,
        grid=(n // tm,),
        in_specs=[
            pl.BlockSpec((tm, MIX_W), lambda i: (i, 0)),
            pl.BlockSpec((tm, MEM_W), lambda i: (i, 0)),
            pl.BlockSpec((None, MIX_W, d), lambda i: (layer, 0, 0), pipeline_mode=once),
            pl.BlockSpec((None, MEM_W, d), lambda i: (layer, MIX_W // MEM_W, 0), pipeline_mode=once),
            pl.BlockSpec((tm, d), lambda i: (i, 0)),
            pl.BlockSpec((1, d), lambda i: (0, 0)),
        ],
        out_specs=[
            pl.BlockSpec((tm, d), lambda i: (i, 0)),
            pl.BlockSpec((d, tm), lambda i: (0, i)),
        ],
        out_shape=[jax.ShapeDtypeStruct((n, d), F32), jax.ShapeDtypeStruct((d, n), BF16)],
        compiler_params=_cparams(("parallel",)),
        name="out_proj",
    )(o_mix, o_mem, w_out, w_out, x, g2.reshape(1, d))


_REMOVED = 2.0 ** 100


def _extract_exact(vals, order, rounds):
    rank = jnp.full(vals.shape, float(PEER_TOPK), F32)
    tops = []
    for r in range(rounds):
        m = jnp.max(vals, axis=0, keepdims=True)
        pick = jnp.min(jnp.where(vals == m, order, 1e9), axis=0, keepdims=True)
        sel = order == pick
        rank = jnp.where(sel, float(r), rank)
        vals = jnp.where(sel, -jnp.inf, vals)
        tops.append(m)
    return jnp.concatenate(tops, axis=0), rank


def _extract_fast(vals, rounds):
    tops = []
    for r in range(rounds):
        m = jnp.max(vals, axis=0, keepdims=True)
        vals = jnp.where(vals == m, -_REMOVED * (r + 1), vals)
        tops.append(m)
    rank = jnp.where(vals <= -_REMOVED, vals * (-1.0 / _REMOVED) - 1.0, float(PEER_TOPK))
    return jnp.concatenate(tops, axis=0), rank


def _route_lanes(s1, s2, exact):
    lanes = s1.shape[1]
    if exact:
        key_id = lax.broadcasted_iota(jnp.int32, (N_KEYS, lanes), 0).astype(F32)
        top1, rank1 = _extract_exact(s1, key_id, PEER_TOPK)
        top2, rank2 = _extract_exact(s2, key_id, PEER_TOPK)
    else:
        top1, rank1 = _extract_fast(s1, PEER_TOPK)
        top2, rank2 = _extract_fast(s2, PEER_TOPK)

    blk_a = top1[0:1] + top2
    blk_b = [top1[a:a + 1] + top2[0:8] for a in range(1, 8)]
    blk_c = top1[8:16] + top2[0:1]
    cand = jnp.concatenate([blk_a] + blk_b + [blk_c], axis=0)
    nrow = cand.shape[0]
    r = lax.broadcasted_iota(jnp.int32, (nrow, lanes), 0)
    a_mid = 1 + jnp.right_shift(r - 16, 3)
    b_mid = jnp.bitwise_and(r - 16, 7)
    valid = jnp.logical_or(jnp.logical_or(r < 16, r >= 72), (a_mid + 1) * (b_mid + 1) <= PEER_TOPK)
    cand = jnp.where(valid, cand, -jnp.inf)
    if exact:
        cand_id = jnp.where(r < 16, r, jnp.where(r < 72, a_mid * PEER_TOPK + b_mid, (r - 64) * PEER_TOPK))
        best, crank = _extract_exact(cand, cand_id.astype(F32), PEER_TOPK)
    else:
        best, crank = _extract_fast(cand, PEER_TOPK)
    chosen = jnp.logical_and(valid, crank < PEER_TOPK).astype(F32)
    z = jnp.sum(jnp.exp(best - best[0:1]), axis=0, keepdims=True)

    counts = [jnp.sum(chosen[0:16], axis=0, keepdims=True)]
    counts += [jnp.sum(chosen[16 + 8 * (a - 1):16 + 8 * a], axis=0, keepdims=True) for a in range(1, 8)]
    counts += [chosen[72 + a - 8:72 + a - 7] for a in range(8, 16)]
    c1 = jnp.zeros((N_KEYS, lanes), F32)
    for a in range(PEER_TOPK):
        c1 = jnp.where(rank1 == a, counts[a], c1)

    ranked = (jnp.sum((rank1 < PEER_TOPK).astype(F32), axis=0, keepdims=True)
              + jnp.sum((rank2 < PEER_TOPK).astype(F32), axis=0, keepdims=True)
              + jnp.sum(chosen, axis=0, keepdims=True))
    low = jnp.minimum(jnp.min(s1, axis=0, keepdims=True), jnp.min(s2, axis=0, keepdims=True))
    redo = jnp.logical_or(ranked != 3.0 * PEER_TOPK, jnp.logical_not(low > -0.25 * _REMOVED)).astype(F32)
    e1 = jnp.exp(s1 - top1[0:1])
    e2 = jnp.exp(s2 - top2[0:1]) / z
    return rank2, e2, c1, e1, redo


def _scores_kernel(xnt_ref, wq_ref, keys_ref, s_ref):
    qq = jnp.dot(wq_ref[...], xnt_ref[...], preferred_element_type=F32).astype(BF16)
    for h in range(PEER_HEADS):
        for c in range(2):
            row = (2 * h + c) * N_KEYS
            s_ref[2 * h + c] = jnp.dot(keys_ref[c, h], qq[row:row + N_KEYS], preferred_element_type=F32)


def _peer_scores(xnt, w_pq_t, keys, layer):
    d, n = xnt.shape
    tb = min(512, n)
    return pl.pallas_call(
        _scores_kernel,
        grid=(n // tb,),
        in_specs=[
            pl.BlockSpec((d, tb), lambda i: (0, i)),
            pl.BlockSpec((None,) + w_pq_t.shape[1:], lambda i: (layer, 0, 0)),
            pl.BlockSpec((None,) + keys.shape[1:], lambda i: (layer, 0, 0, 0, 0)),
        ],
        out_specs=pl.BlockSpec((2 * PEER_HEADS, N_KEYS, tb), lambda i: (0, 0, i)),
        out_shape=jax.ShapeDtypeStruct((2 * PEER_HEADS, N_KEYS, n), F32),
        compiler_params=_cparams(("parallel",)),
        name="peer_scores",
    )(xnt, w_pq_t, keys)


def _route_kernel(s_ref, r2_ref, e2_ref, c1_ref, e1_ref, *, hr):
    def emit(h, exact):
        rank2, e2, c1, e1, redo = _route_lanes(s_ref[2 * h], s_ref[2 * h + 1], exact)
        r2_ref[h] = rank2.astype(BF16)
        e2_ref[h] = e2.astype(BF16)
        c1_ref[h] = c1
        e1_ref[h] = e1
        return redo

    redo = [emit(h, exact=False) for h in range(hr)]
    for h in range(hr):
        @pl.when(jnp.max(redo[h]) > 0.0)
        def _():
            emit(h, exact=True)


def _peer_route(scores, hr=HR_ROUTE):
    n = scores.shape[2]
    tb = min(TB_ROUTE, n)
    out = jax.ShapeDtypeStruct((PEER_HEADS, N_KEYS, n), F32)
    out16 = jax.ShapeDtypeStruct((PEER_HEADS, N_KEYS, n), BF16)
    ospec = pl.BlockSpec((hr, N_KEYS, tb), lambda i, h: (h, 0, i))
    return pl.pallas_call(
        functools.partial(_route_kernel, hr=hr),
        grid=(n // tb, PEER_HEADS // hr),
        in_specs=[pl.BlockSpec((2 * hr, N_KEYS, tb), lambda i, h: (h, 0, i))],
        out_specs=[ospec, ospec, ospec, ospec],
        out_shape=[out16, out16, out, out],
        compiler_params=_cparams(("parallel", "parallel")),
        name="peer_route",
    )(scores)


def _gelu(x):
    return 0.5 * x * (1.0 + lax.erf(x * (0.5 ** 0.5)))


def _peer_kernel(xnt_ref, u_ref, un_ref, vt_ref, r2_ref, e2_ref, c1_ref, e1_ref, y_ref, a_ref, coef_ref, *,
                 eb, ck, lc):
    e = pl.program_id(1)
    tb = xnt_ref.shape[1]
    per = ck // N_KEYS
    nck = eb // ck

    @pl.when(e == 0)
    def _():
        y_ref[...] = jnp.zeros_like(y_ref)
        a_ref[0] = jnp.dot(u_ref[0:ck, :], xnt_ref[...], preferred_element_type=F32)

    for k in range(nck):
        if k + 1 < nck:
            a_next = jnp.dot(u_ref[(k + 1) * ck:(k + 2) * ck, :], xnt_ref[...], preferred_element_type=F32)
            a_ref[k + 1] = a_next
        else:
            a_next = jnp.dot(un_ref[...], xnt_ref[...], preferred_element_type=F32)
        bits = pltpu.bitcast(a_next[0:1, tb - lc:], jnp.uint32)
        bits = lax.shift_right_logical(lax.shift_right_logical(bits, jnp.uint32(16)), jnp.uint32(16))
        zero = pltpu.bitcast(bits, F32)
        for ii in range(per):
            i = k * per + ii
            rows = slice(ii * N_KEYS, (ii + 1) * N_KEYS)
            for l0 in range(0, tb, lc):
                cols = slice(l0, l0 + lc)
                gate = jnp.zeros((N_KEYS, lc), BF16)
                for h in range(PEER_HEADS):
                    c1 = jnp.broadcast_to(c1_ref[h, i:i + 1, cols].astype(BF16), (N_KEYS, lc))
                    e1 = jnp.broadcast_to(e1_ref[h, i:i + 1, cols].astype(BF16), (N_KEYS, lc))
                    zero16 = jnp.zeros((N_KEYS, lc), BF16)
                    gate = gate + jnp.where(r2_ref[h, :, cols] < c1, e2_ref[h, :, cols], zero16) * e1
                act = gate.astype(F32) * _gelu(a_ref[k, rows, cols])
                if ii == per - 1 and l0 + lc == tb:
                    act = act + zero
                coef_ref[k, rows, cols] = act.astype(BF16)
        if k + 1 == nck:
            a_ref[0] = a_next
        y_ref[...] += jnp.dot(vt_ref[:, k * ck:(k + 1) * ck], coef_ref[k], preferred_element_type=F32)


def _peer_sweep(xnt, u, vt, layer, r2, e2, c1, e1):
    d, n = xnt.shape
    tb = min(TB_PEER, n)
    eb = EB_PEER
    ck = CK_PEER
    lc = min(LC_PEER, tb)
    rspec = pl.BlockSpec((PEER_HEADS, N_KEYS, tb), lambda t, e: (0, 0, t))
    ispec = pl.BlockSpec((PEER_HEADS, eb // N_KEYS, tb), lambda t, e: (0, e, t))
    last_ck = N_EXPERTS // ck - 1
    return pl.pallas_call(
        functools.partial(_peer_kernel, eb=eb, ck=ck, lc=lc),
        grid=(n // tb, N_EXPERTS // eb),
        in_specs=[
            pl.BlockSpec((d, tb), lambda t, e: (0, t)),
            pl.BlockSpec((None, eb, d), lambda t, e: (layer, e, 0)),
            pl.BlockSpec((None, ck, d), lambda t, e: (layer, jnp.minimum((e + 1) * (eb // ck), last_ck), 0)),
            pl.BlockSpec((None, d, eb), lambda t, e: (layer, 0, e)),
            rspec, rspec, ispec, ispec,
        ],
        out_specs=pl.BlockSpec((d, tb), lambda t, e: (0, t)),
        out_shape=jax.ShapeDtypeStruct((d, n), F32),
        scratch_shapes=[pltpu.VMEM((eb // ck, ck, tb), F32), pltpu.VMEM((eb // ck, ck, tb), BF16)],
        compiler_params=_cparams(("parallel", "arbitrary")),
        name="peer_sweep",
    )(xnt, u, u, vt, r2, e2, c1, e1)


def _add_t_kernel(x_ref, yt_ref, g_ref, o_ref, *, final):
    x = x_ref[...] + yt_ref[...].T
    if final:
        x = _rms(x, g_ref[...])
    o_ref[...] = x


def _add_transposed(x1, yt, g, final):
    n, d = x1.shape
    tm = min(256, n)
    return pl.pallas_call(
        functools.partial(_add_t_kernel, final=final),
        grid=(n // tm,),
        in_specs=[
            pl.BlockSpec((tm, d), lambda i: (i, 0)),
            pl.BlockSpec((d, tm), lambda i: (0, i)),
            pl.BlockSpec((1, d), lambda i: (0, 0)),
        ],
        out_specs=pl.BlockSpec((tm, d), lambda i: (i, 0)),
        out_shape=jax.ShapeDtypeStruct((n, d), F32),
        compiler_params=_cparams(("parallel",)),
        name="peer_residual",
    )(x1, yt, g.reshape(1, d))


def _trunk(x, mem, k_cache, v_cache, s_cache, mk_cache, mv_cache, p):
    prompt = mem is not None
    bsz, t, d = x.shape
    n = bsz * t
    depth = p["g_norm1"].shape[0]
    x2 = x.reshape(n, d)
    new_k, new_v, new_s, new_mk, new_mv = [], [], [], [], []
    for i in range(depth):
        j = i // 2
        attn = i % 2 == 0
        w_in = p["w_in_a"] if attn else p["w_in_b"]
        w_out = p["w_out_a"] if attn else p["w_out_b"]
        tm = min(TM_PROJ, t)
        pad = -(-WINDOW // tm) * tm if (attn and prompt) else 0
        z = _norm_matmul(x2, p["g_norm1"][i], w_in, j, seq=t, pad=pad)
        z3 = z.reshape(bsz, pad + t, z.shape[1])
        mq_blk = (z.shape[1] - MEM_W) // MEM_W

        if prompt:
            mkv = _norm_matmul(mem.reshape(bsz * N_MEM, d), p["g_mem"][i], p["w_mem_kv"], i)
            new_mk.append(mkv[:, :MEM_W].reshape(bsz, N_MEM, N_MEM_HEADS, HEAD_DIM))
            new_mv.append(mkv[:, MEM_W:].reshape(bsz, N_MEM, N_MEM_HEADS, HEAD_DIM))
            mkv3 = mkv.reshape(bsz, N_MEM, 2 * MEM_W)
            kspec = pl.BlockSpec((None, N_MEM, MEM_W), lambda b, q: (b, 0, 0))
            vspec = pl.BlockSpec((None, N_MEM, MEM_W), lambda b, q: (b, 0, 1))
            o_mem = _mem_attention(z3, mq_blk, pad, t, mkv3, kspec, mkv3, vspec, head_major=False)
        else:
            cspec = pl.BlockSpec((None, None, N_MEM_HEADS, N_MEM, HEAD_DIM), lambda b, q, i=i: (i, b, 0, 0, 0))
            o_mem = _mem_attention(z3, mq_blk, pad, t, jnp.transpose(mk_cache, (0, 1, 3, 2, 4)), cspec,
                                   jnp.transpose(mv_cache, (0, 1, 3, 2, 4)), cspec, head_major=True)

        if attn:
            k_new = z3[:, pad:, MIX_W:2 * MIX_W]
            v_new = z3[:, pad:, 2 * MIX_W:3 * MIX_W]
            if prompt:
                bias = _band_bias(p["rel_bias"][j], CHUNK, WINDOW + CHUNK, WINDOW)
                o_mix = _band_attention_prompt(z3, pad, bias)
                keep = min(WINDOW, t)
                k_new, v_new = k_new[:, t - keep:], v_new[:, t - keep:]
            else:
                p_len = k_cache.shape[2]
                bias = _band_bias(p["rel_bias"][j], t, p_len + t, p_len)
                o_mix = _band_attention_sample(z3, k_cache, v_cache, j, bias)
            new_k.append(k_new.reshape(bsz, -1, N_MIX_HEADS, HEAD_DIM))
            new_v.append(v_new.reshape(bsz, -1, N_MIX_HEADS, HEAD_DIM))
        else:
            o_mix, s_new = _hgrn(z3, p["lb_param"], p["g_hgrn"][j], None if prompt else s_cache, j, i)
            new_s.append(s_new)

        x1, xnt = _out_proj(o_mix.reshape(n, MIX_W), o_mem.reshape(n, MEM_W), w_out, j, x2, p["g_norm2"][i])
        r2, e2, c1, e1 = _peer_route(_peer_scores(xnt, p["w_pq_t"], p["peer_keys"], i))
        yt = _peer_sweep(xnt, p["peer_u"], p["peer_v_t"], i, r2, e2, c1, e1)
        x2 = _add_transposed(x1, yt, p["g_final"], final=(i == depth - 1))

    y = x2.reshape(bsz, t, d)
    mk_out = jnp.stack(new_mk) if new_mk else None
    mv_out = jnp.stack(new_mv) if new_mv else None
    return y, jnp.stack(new_k), jnp.stack(new_v), jnp.stack(new_s), mk_out, mv_out


def kernel(x_prompt, x_sample, cache_attn_k, cache_attn_v, state_hgrn, cache_mem_k, cache_mem_v, mem_prompt,
           g_norm1, g_norm2, g_mem, g_final, w_in_a, w_out_a, rel_bias, w_in_b, w_out_b, lb_param, g_hgrn,
           w_mem_kv, w_pq, peer_keys, peer_u, peer_v):
    p = dict(
        g_norm1=g_norm1, g_norm2=g_norm2, g_mem=g_mem, g_final=g_final, rel_bias=rel_bias,
        lb_param=lb_param, g_hgrn=g_hgrn,
        w_in_a=w_in_a.astype(BF16), w_out_a=w_out_a.astype(BF16),
        w_in_b=w_in_b.astype(BF16), w_out_b=w_out_b.astype(BF16),
        w_mem_kv=w_mem_kv.astype(BF16),
        w_pq_t=jnp.swapaxes(w_pq, 1, 2).astype(BF16),
        peer_keys=peer_keys.astype(BF16),
        peer_u=peer_u.astype(BF16),
        peer_v_t=jnp.swapaxes(peer_v, 1, 2).astype(BF16),
    )
    y_p, k_p, v_p, s_p, mk_p, mv_p = _trunk(x_prompt, mem_prompt, None, None, None, None, None, p)
    y_s, k_s, v_s, s_s, _, _ = _trunk(x_sample, None, cache_attn_k, cache_attn_v, state_hgrn,
                                      cache_mem_k, cache_mem_v, p)
    return (y_p, y_s, k_p, v_p, k_s, v_s, s_p, s_s, mk_p, mv_p)
```

```python
import functools

import jax
import jax.numpy as jnp
from jax import lax
from jax.experimental import pallas as pl
from jax.experimental.pallas import tpu as pltpu

F32 = jnp.float32
BF16 = jnp.bfloat16

D_MODEL = 2048
PAST_LEN = 1024
CHUNK = 64
LEFT_CHUNKS = 8
WINDOW = LEFT_CHUNKS * CHUNK
HEAD_DIM = 128
N_MIX_HEADS = 12
N_MEM_HEADS = 4
MIX_W = N_MIX_HEADS * HEAD_DIM
MEM_W = N_MEM_HEADS * HEAD_DIM
N_MEM = 256
REL_CLIP = 256
PEER_HEADS = 8
N_KEYS = 128
N_EXPERTS = N_KEYS * N_KEYS
PEER_TOPK = 16
EPS = 1e-6
NEG_INF = -1e30

LANES = 128
SUBLANES = 8
SUB = SUBLANES
VMEM_LIMIT = 56 * 1024 * 1024

TM_PROJ = 1024
TN_PROJ = 1664
TB_ROUTE = 256
HR_ROUTE = 2
TB_PEER = 512
EB_PEER = 1024
CK_PEER = 256
LC_PEER = 256
HB_MIX = 4
QB_ATTN = 256
HB_HGRN = 12


def _cparams(sem, flags=None):
    return pltpu.CompilerParams(dimension_semantics=sem, vmem_limit_bytes=VMEM_LIMIT, flags=flags)


def _rms(x, g):
    return x * lax.rsqrt(jnp.mean(x * x, axis=-1, keepdims=True) + EPS) * g


def _norm_matmul_kernel(x_ref, g_ref, w_ref, o_ref, hn_ref):
    @pl.when(pl.program_id(1) == 0)
    def _():
        hn_ref[...] = _rms(x_ref[...], g_ref[...]).astype(BF16)

    o_ref[...] = jnp.dot(hn_ref[...], w_ref[...], preferred_element_type=F32)


def _norm_matmul(x, g, w, layer):
    m, d = x.shape
    n = w.shape[2]
    tm = min(TM_PROJ, m)
    tn = n // -(-n // TN_PROJ)
    assert n % tn == 0 and tn % LANES == 0
    return pl.pallas_call(
        _norm_matmul_kernel,
        grid=(m // tm, n // tn),
        in_specs=[
            pl.BlockSpec((tm, d), lambda i, j: (i, 0)),
            pl.BlockSpec((1, d), lambda i, j: (0, 0)),
            pl.BlockSpec((None, d, tn), lambda i, j: (layer, 0, j)),
        ],
        out_specs=pl.BlockSpec((tm, tn), lambda i, j: (i, j)),
        out_shape=jax.ShapeDtypeStruct((m, n), F32),
        scratch_shapes=[pltpu.VMEM((tm, d), BF16)],
        compiler_params=_cparams(("parallel", "arbitrary")),
        name="norm_matmul",
    )(x, g.reshape(1, d), w)


def _bias_kernel(tab_ref, o_ref, *, tq, tk, off, hg):
    h0 = pl.program_id(0) * hg
    clip = lambda d: max(min(d, REL_CLIP), -REL_CLIP) + REL_CLIP
    for c0 in range(0, tk, LANES):
        w = min(LANES, tk - c0)
        qi = lax.broadcasted_iota(jnp.int32, (tq, w), 0)
        ki = lax.broadcasted_iota(jnp.int32, (tq, w), 1) + c0
        rel = jnp.clip(qi - ki + off, -REL_CLIP, REL_CLIP) + REL_CLIP
        lo = clip(-(c0 + w - 1) + off)
        hi = clip((tq - 1) - c0 + off)

        def body(r, accs, rel=rel):
            hit = rel == r
            return tuple(jnp.where(hit, tab_ref[h0 + h, r], acc) for h, acc in enumerate(accs))

        accs = lax.fori_loop(lo, hi + 1, body, tuple(jnp.zeros((tq, w), F32) for _ in range(hg)))
        for h in range(hg):
            o_ref[h, :, c0:c0 + w] = accs[h]


def _band_bias(table, tq, tk, off, hg=4):
    nh = table.shape[1]
    return pl.pallas_call(
        functools.partial(_bias_kernel, tq=tq, tk=tk, off=off, hg=hg),
        grid=(nh // hg,),
        in_specs=[pl.BlockSpec(memory_space=pltpu.SMEM)],
        out_specs=pl.BlockSpec((hg, tq, tk), lambda g: (g, 0, 0)),
        out_shape=jax.ShapeDtypeStruct((nh, tq, tk), F32),
        compiler_params=_cparams(("arbitrary",)),
        name="band_bias",
    )(table.T)


def _softmax_rows(s):
    m = jnp.max(s, axis=-1, keepdims=True)
    p = jnp.exp(s - m)
    return p / jnp.sum(p, axis=-1, keepdims=True)


def _nt_dot(a, b):
    return lax.dot_general(a, b, (((1,), (1,)), ((), ())), preferred_element_type=F32)


def _tn_dot(a, b):
    return lax.dot_general(a, b, (((0,), (0,)), ((), ())), preferred_element_type=F32)


def _band_prompt_kernel(*refs, hb, qb, nwb):
    q_ref = refs[0]
    k_refs = refs[1:1 + nwb]
    v_refs = refs[1 + nwb:1 + 2 * nwb]
    bias_ref, o_ref = refs[1 + 2 * nwb:]
    c = pl.program_id(2)
    tk = WINDOW + qb
    k_pos = lax.broadcasted_iota(jnp.int32, (qb, tk), 1) + (c * qb - WINDOW)
    live = k_pos >= 0
    scale = HEAD_DIM ** -0.5
    scores = []
    for h in range(hb):
        sl = slice(h * HEAD_DIM, (h + 1) * HEAD_DIM)
        kw = jnp.concatenate([r[:, sl].astype(BF16) for r in k_refs], axis=0)
        scores.append(_nt_dot(q_ref[:, sl].astype(BF16), kw))
    for h in range(hb):
        sl = slice(h * HEAD_DIM, (h + 1) * HEAD_DIM)
        vw = jnp.concatenate([r[:, sl].astype(BF16) for r in v_refs], axis=0)
        s = scores[h] * scale + bias_ref[h]
        s = jnp.where(live, s, NEG_INF)
        p = _softmax_rows(s).astype(BF16)
        o_ref[:, sl] = jnp.dot(p, vw, preferred_element_type=F32).astype(BF16)


def _band_attention_prompt(z3, bias, hb=N_MIX_HEADS):
    b, s, _ = z3.shape
    qb = min(QB_ATTN, s)
    assert WINDOW % qb == 0
    nqc = qb // CHUNK
    nwb = WINDOW // qb + 1
    ng = N_MIX_HEADS // hb
    w = hb * HEAD_DIM
    tk = WINDOW + qb

    def win_spec(j, seg):
        return pl.BlockSpec((None, qb, w), lambda i, g, c, j=j, seg=seg:
                            (i, jnp.maximum(c + j - (nwb - 1), 0), seg * ng + g))
    bias_blk = jnp.concatenate(
        [jnp.pad(bias, ((0, 0), (0, 0), (j * CHUNK, (nqc - 1 - j) * CHUNK)), constant_values=NEG_INF)
         for j in range(nqc)], axis=1)
    return pl.pallas_call(
        functools.partial(_band_prompt_kernel, hb=hb, qb=qb, nwb=nwb),
        grid=(b, ng, s // qb),
        in_specs=[
            win_spec(nwb - 1, 0),
            *[win_spec(j, 1) for j in range(nwb)],
            *[win_spec(j, 2) for j in range(nwb)],
            pl.BlockSpec((hb, qb, tk), lambda i, g, c: (g, 0, 0), pipeline_mode=pl.Buffered(1)),
        ],
        out_specs=pl.BlockSpec((None, qb, w), lambda i, g, c: (i, c, g)),
        out_shape=jax.ShapeDtypeStruct((b, s, MIX_W), BF16),
        compiler_params=_cparams(("parallel", "parallel", "arbitrary")),
        name="band_attn_prompt",
    )(*([z3] * (1 + 2 * nwb)), bias_blk)


def _band_sample_kernel(*refs, hb, p_len):
    q_ref, kn_ref, vn_ref = refs[:3]
    kc_refs = refs[3:3 + hb]
    vc_refs = refs[3 + hb:3 + 2 * hb]
    bias_ref, o_ref = refs[3 + 2 * hb:]
    scale = HEAD_DIM ** -0.5
    scores = []
    for h in range(hb):
        sl = slice(h * HEAD_DIM, (h + 1) * HEAD_DIM)
        q = q_ref[:, sl].astype(BF16)
        scores.append((_nt_dot(q, kc_refs[h][...].astype(BF16)), _nt_dot(q, kn_ref[:, sl].astype(BF16))))
    for h in range(hb):
        sl = slice(h * HEAD_DIM, (h + 1) * HEAD_DIM)
        s_c = scores[h][0] * scale + bias_ref[h, :, :p_len]
        s_n = scores[h][1] * scale + bias_ref[h, :, p_len:]
        m = jnp.maximum(jnp.max(s_c, axis=-1, keepdims=True), jnp.max(s_n, axis=-1, keepdims=True))
        e_c = jnp.exp(s_c - m)
        e_n = jnp.exp(s_n - m)
        l = jnp.sum(e_c, axis=-1, keepdims=True) + jnp.sum(e_n, axis=-1, keepdims=True)
        o = jnp.dot((e_c / l).astype(BF16), vc_refs[h][...].astype(BF16), preferred_element_type=F32)
        o = o + jnp.dot((e_n / l).astype(BF16), vn_ref[:, sl].astype(BF16), preferred_element_type=F32)
        o_ref[:, sl] = o.astype(BF16)


def _band_attention_sample(z3, k_cache, v_cache, layer, bias, hb=N_MIX_HEADS):
    b, t, _ = z3.shape
    p_len = k_cache.shape[2]
    ng = N_MIX_HEADS // hb
    w = hb * HEAD_DIM
    k_cache = jnp.transpose(k_cache, (0, 1, 3, 2, 4))
    v_cache = jnp.transpose(v_cache, (0, 1, 3, 2, 4))

    def cache_spec(h):
        return pl.BlockSpec((None, None, None, p_len, HEAD_DIM), lambda i, g, h=h: (layer, i, g * hb + h, 0, 0))

    cache_specs = [cache_spec(h) for h in range(hb)]
    return pl.pallas_call(
        functools.partial(_band_sample_kernel, hb=hb, p_len=p_len),
        grid=(b, ng),
        in_specs=[
            pl.BlockSpec((None, t, w), lambda i, g: (i, 0, g)),
            pl.BlockSpec((None, t, w), lambda i, g: (i, 0, ng + g)),
            pl.BlockSpec((None, t, w), lambda i, g: (i, 0, 2 * ng + g)),
            *cache_specs, *cache_specs,
            pl.BlockSpec((hb, t, p_len + t), lambda i, g: (g, 0, 0)),
        ],
        out_specs=pl.BlockSpec((None, t, w), lambda i, g: (i, 0, g)),
        out_shape=jax.ShapeDtypeStruct((b, t, MIX_W), BF16),
        compiler_params=_cparams(("parallel", "parallel")),
        name="band_attn_sample",
    )(z3, z3, z3, *([k_cache] * hb), *([v_cache] * hb), bias)


def _mem_attn_kernel(q_ref, k_ref, v_ref, o_ref, *, head_major):
    scores = []
    for h in range(N_MEM_HEADS):
        sl = slice(h * HEAD_DIM, (h + 1) * HEAD_DIM)
        k = k_ref[h] if head_major else k_ref[:, sl]
        scores.append(_nt_dot(q_ref[:, sl].astype(BF16), k.astype(BF16)))
    for h in range(N_MEM_HEADS):
        sl = slice(h * HEAD_DIM, (h + 1) * HEAD_DIM)
        v = v_ref[h] if head_major else v_ref[:, sl]
        s = scores[h] * (HEAD_DIM ** -0.5)
        p = _softmax_rows(s).astype(BF16)
        o_ref[:, sl] = jnp.dot(p, v.astype(BF16), preferred_element_type=F32).astype(BF16)


def _mem_attention(z3, q_blk, karr, kspec, varr, vspec, head_major):
    b, t, _ = z3.shape
    tq = min(t, 512)
    return pl.pallas_call(
        functools.partial(_mem_attn_kernel, head_major=head_major),
        grid=(b, t // tq),
        in_specs=[
            pl.BlockSpec((None, tq, MEM_W), lambda i, j: (i, j, q_blk)),
            kspec, vspec,
        ],
        out_specs=pl.BlockSpec((None, tq, MEM_W), lambda i, j: (i, j, 0)),
        out_shape=jax.ShapeDtypeStruct((b, t, MEM_W), BF16),
        compiler_params=_cparams(("parallel", "arbitrary")),
        name="mem_attn",
    )(z3, karr, varr)


def _hgrn_kernel(*refs, hb, t, layer, has_state):
    if has_state:
        q_ref, f_ref, i_ref, g_ref, lbp_ref, gh_ref, s0_ref, o_ref, sout_ref, st_ref = refs
    else:
        q_ref, f_ref, i_ref, g_ref, lbp_ref, gh_ref, o_ref, sout_ref, st_ref = refs
    c = pl.program_id(2)
    nsub = t // SUB

    @pl.when(c == 0)
    def _():
        for h in range(hb):
            if has_state:
                st_ref[h] = s0_ref[h].T
            else:
                st_ref[h] = jnp.zeros((HEAD_DIM, HEAD_DIM), F32)

    row = lax.broadcasted_iota(jnp.int32, (t, t), 0)
    col = lax.broadcasted_iota(jnp.int32, (t, t), 1)
    rowd = lax.broadcasted_iota(jnp.int32, (t, HEAD_DIM), 0)
    hits = [jnp.logical_and(row - col == dl, jnp.bitwise_and(row, SUB - 1) >= dl) for dl in range(SUB)]
    far = jnp.right_shift(col, 3) < jnp.right_shift(row, 3)

    pending = []
    for h in range(hb):
        sl = slice(h * HEAD_DIM, (h + 1) * HEAD_DIM)
        q = q_ref[:, sl]
        ff = f_ref[:, sl]
        v = i_ref[:, sl]

        lbp = lbp_ref[:, sl]
        e = jnp.exp(lbp - jnp.max(lbp, axis=0, keepdims=True))
        soft = e / jnp.sum(e, axis=0, keepdims=True)
        acc = soft[0:1]
        for r in range(1, layer + 1):
            acc = acc + soft[r:r + 1]
        lb = acc - soft[0:1]

        en = jnp.exp(-jnp.abs(ff))
        rec = 1.0 / (1.0 + en)
        sig_pos = jnp.where(ff >= 0.0, rec, en * rec)
        sig_neg = jnp.where(ff >= 0.0, en * rec, rec)
        f = lb + (1.0 - lb) * sig_pos
        log_f = jnp.where(f > 0.0, jnp.log(f), ff)
        k = (1.0 - lb) * sig_neg

        b = log_f
        sh = 1
        while sh < t:
            b = b + jnp.where(rowd >= sh, pltpu.roll(b, sh, 0), 0.0)
            sh *= 2
        b_end = b[t - 1:t]

        st = st_ref[h]
        o = _nt_dot((q * jnp.exp(b)).astype(BF16), st.astype(BF16))

        sc = jnp.zeros((t, t), F32)
        for dl in range(SUB):
            if dl == 0:
                prod = q * k
            else:
                prod = q * pltpu.roll(k, dl, 0) * jnp.exp(jnp.minimum(b - pltpu.roll(b, dl, 0), 0.0))
            r = jnp.sum(prod, axis=1, keepdims=True)
            sc = jnp.where(hits[dl], r, sc)

        q_parts = []
        k_rows = []
        for i in range(nsub - 1):
            lo = (i + 1) * SUB
            b_ref = b[lo - 1:lo]
            later = q[lo:] * jnp.exp(jnp.minimum(b[lo:] - b_ref, 0.0))
            q_parts.append(jnp.concatenate([jnp.zeros((lo, HEAD_DIM), F32), later], axis=0).astype(BF16))
            kh = k[lo - SUB:lo] * jnp.exp(jnp.minimum(b_ref - b[lo - SUB:lo], 0.0))
            zero = jnp.zeros_like(kh)
            k_rows.append(jnp.concatenate([kh if j == i else zero for j in range(nsub - 1)], axis=1))
        k_rows.append(jnp.zeros((SUB, (nsub - 1) * HEAD_DIM), F32))
        q_cat = jnp.concatenate(q_parts, axis=1)
        k_bd = jnp.concatenate(k_rows, axis=0).astype(BF16)
        sc_far = _nt_dot(q_cat, k_bd)
        scores = jnp.where(far, sc_far, sc)

        k_dec = (k * jnp.exp(jnp.minimum(b_end - b, 0.0))).astype(BF16)
        st_ref[h] = st * jnp.exp(b_end) + _tn_dot(v.astype(BF16), k_dec)
        pending.append((o, scores.astype(BF16)))

    for h in range(hb):
        sl = slice(h * HEAD_DIM, (h + 1) * HEAD_DIM)
        o, scores = pending[h]
        g = g_ref[:, sl]
        o = o + jnp.dot(scores, i_ref[:, sl].astype(BF16), preferred_element_type=F32)
        o = _rms(o, gh_ref[:, sl])
        o = o * (g * jax.nn.sigmoid(g))
        o_ref[:, sl] = o.astype(BF16)

    @pl.when(c == pl.num_programs(2) - 1)
    def _():
        for h in range(hb):
            sout_ref[h] = st_ref[h].T


def _hgrn(z3, lb_param, g_head, state, state_idx, layer, hb=HB_HGRN):
    assert SUB == 8
    b, s, _ = z3.shape
    t = min(CHUNK, s)
    nc = s // t
    ng = N_MIX_HEADS // hb
    w = hb * HEAD_DIM
    depth = lb_param.shape[0]
    has_state = state is not None

    def zspec(seg):
        return pl.BlockSpec((None, t, w), lambda i, g, c, seg=seg: (i, c, seg * ng + g))

    in_specs = [zspec(0), zspec(1), zspec(2), zspec(3),
                pl.BlockSpec((depth, w), lambda i, g, c: (0, g)),
                pl.BlockSpec((1, w), lambda i, g, c: (0, g))]
    args = [z3, z3, z3, z3, lb_param, g_head.reshape(1, MIX_W)]
    if has_state:
        in_specs.append(pl.BlockSpec((None, None, hb, HEAD_DIM, HEAD_DIM),
                                     lambda i, g, c: (state_idx, i, g, 0, 0)))
        args.append(state)
    return pl.pallas_call(
        functools.partial(_hgrn_kernel, hb=hb, t=t, layer=layer, has_state=has_state),
        grid=(b, ng, nc),
        in_specs=in_specs,
        out_specs=[
            pl.BlockSpec((None, t, w), lambda i, g, c: (i, c, g)),
            pl.BlockSpec((None, hb, HEAD_DIM, HEAD_DIM), lambda i, g, c: (i, g, 0, 0)),
        ],
        out_shape=[
            jax.ShapeDtypeStruct((b, s, MIX_W), BF16),
            jax.ShapeDtypeStruct((b, N_MIX_HEADS, HEAD_DIM, HEAD_DIM), F32),
        ],
        scratch_shapes=[pltpu.VMEM((hb, HEAD_DIM, HEAD_DIM), F32)],
        compiler_params=_cparams(("parallel", "parallel", "arbitrary")),
        name="hgrn2",
    )(*args)


def _out_proj_kernel(om_ref, oe_ref, wm_ref, we_ref, x_ref, g_ref, x1_ref, xnt_ref):
    x1 = x_ref[...] + jnp.dot(om_ref[...], wm_ref[...], preferred_element_type=F32)
    x1 = x1 + jnp.dot(oe_ref[...], we_ref[...], preferred_element_type=F32)
    x1_ref[...] = x1
    xnt_ref[...] = _rms(x1, g_ref[...]).T.astype(BF16)


def _out_proj(o_mix, o_mem, w_out, layer, x, g2):
    n, d = x.shape
    tm = min(512, n)
    once = pl.Buffered(1)
    return pl.pallas_call(
        _out_proj_kernel,
        grid=(n // tm,),
        in_specs=[
            pl.BlockSpec((tm, MIX_W), lambda i: (i, 0)),
            pl.BlockSpec((tm, MEM_W), lambda i: (i, 0)),
            pl.BlockSpec((None, MIX_W, d), lambda i: (layer, 0, 0), pipeline_mode=once),
            pl.BlockSpec((None, MEM_W, d), lambda i: (layer, MIX_W // MEM_W, 0), pipeline_mode=once),
            pl.BlockSpec((tm, d), lambda i: (i, 0)),
            pl.BlockSpec((1, d), lambda i: (0, 0)),
        ],
        out_specs=[
            pl.BlockSpec((tm, d), lambda i: (i, 0)),
            pl.BlockSpec((d, tm), lambda i: (0, i)),
        ],
        out_shape=[jax.ShapeDtypeStruct((n, d), F32), jax.ShapeDtypeStruct((d, n), BF16)],
        compiler_params=_cparams(("parallel",)),
        name="out_proj",
    )(o_mix, o_mem, w_out, w_out, x, g2.reshape(1, d))


_REMOVED = 2.0 ** 100


def _extract_exact(vals, order, rounds):
    rank = jnp.full(vals.shape, float(PEER_TOPK), F32)
    tops = []
    for r in range(rounds):
        m = jnp.max(vals, axis=0, keepdims=True)
        pick = jnp.min(jnp.where(vals == m, order, 1e9), axis=0, keepdims=True)
        sel = order == pick
        rank = jnp.where(sel, float(r), rank)
        vals = jnp.where(sel, -jnp.inf, vals)
        tops.append(m)
    return jnp.concatenate(tops, axis=0), rank


def _extract_fast(vals, rounds):
    tops = []
    for r in range(rounds):
        m = jnp.max(vals, axis=0, keepdims=True)
        vals = jnp.where(vals == m, -_REMOVED * (r + 1), vals)
        tops.append(m)
    rank = jnp.where(vals <= -_REMOVED, vals * (-1.0 / _REMOVED) - 1.0, float(PEER_TOPK))
    return jnp.concatenate(tops, axis=0), rank


def _route_lanes(s1, s2, exact):
    lanes = s1.shape[1]
    if exact:
        key_id = lax.broadcasted_iota(jnp.int32, (N_KEYS, lanes), 0).astype(F32)
        top1, rank1 = _extract_exact(s1, key_id, PEER_TOPK)
        top2, rank2 = _extract_exact(s2, key_id, PEER_TOPK)
    else:
        top1, rank1 = _extract_fast(s1, PEER_TOPK)
        top2, rank2 = _extract_fast(s2, PEER_TOPK)

    blk_a = top1[0:1] + top2
    blk_b = [top1[a:a + 1] + top2[0:8] for a in range(1, 8)]
    blk_c = top1[8:16] + top2[0:1]
    cand = jnp.concatenate([blk_a] + blk_b + [blk_c], axis=0)
    nrow = cand.shape[0]
    r = lax.broadcasted_iota(jnp.int32, (nrow, lanes), 0)
    a_mid = 1 + jnp.right_shift(r - 16, 3)
    b_mid = jnp.bitwise_and(r - 16, 7)
    valid = jnp.logical_or(jnp.logical_or(r < 16, r >= 72), (a_mid + 1) * (b_mid + 1) <= PEER_TOPK)
    cand = jnp.where(valid, cand, -jnp.inf)
    if exact:
        cand_id = jnp.where(r < 16, r, jnp.where(r < 72, a_mid * PEER_TOPK + b_mid, (r - 64) * PEER_TOPK))
        best, crank = _extract_exact(cand, cand_id.astype(F32), PEER_TOPK)
    else:
        best, crank = _extract_fast(cand, PEER_TOPK)
    chosen = jnp.logical_and(valid, crank < PEER_TOPK).astype(F32)
    z = jnp.sum(jnp.exp(best - best[0:1]), axis=0, keepdims=True)

    counts = [jnp.sum(chosen[0:16], axis=0, keepdims=True)]
    counts += [jnp.sum(chosen[16 + 8 * (a - 1):16 + 8 * a], axis=0, keepdims=True) for a in range(1, 8)]
    counts += [chosen[72 + a - 8:72 + a - 7] for a in range(8, 16)]
    c1 = jnp.zeros((N_KEYS, lanes), F32)
    for a in range(PEER_TOPK):
        c1 = jnp.where(rank1 == a, counts[a], c1)

    ranked = (jnp.sum((rank1 < PEER_TOPK).astype(F32), axis=0, keepdims=True)
              + jnp.sum((rank2 < PEER_TOPK).astype(F32), axis=0, keepdims=True)
              + jnp.sum(chosen, axis=0, keepdims=True))
    low = jnp.minimum(jnp.min(s1, axis=0, keepdims=True), jnp.min(s2, axis=0, keepdims=True))
    redo = jnp.logical_or(ranked != 3.0 * PEER_TOPK, jnp.logical_not(low > -0.25 * _REMOVED)).astype(F32)
    e1 = jnp.exp(s1 - top1[0:1])
    e2 = jnp.exp(s2 - top2[0:1]) / z
    return rank2, e2, c1, e1, redo


def _scores_kernel(xnt_ref, wq_ref, keys_ref, s_ref):
    qq = jnp.dot(wq_ref[...], xnt_ref[...], preferred_element_type=F32).astype(BF16)
    for h in range(PEER_HEADS):
        for c in range(2):
            row = (2 * h + c) * N_KEYS
            s_ref[2 * h + c] = jnp.dot(keys_ref[c, h], qq[row:row + N_KEYS], preferred_element_type=F32)


def _peer_scores(xnt, w_pq_t, keys, layer):
    d, n = xnt.shape
    tb = min(512, n)
    return pl.pallas_call(
        _scores_kernel,
        grid=(n // tb,),
        in_specs=[
            pl.BlockSpec((d, tb), lambda i: (0, i)),
            pl.BlockSpec((None,) + w_pq_t.shape[1:], lambda i: (layer, 0, 0)),
            pl.BlockSpec((None,) + keys.shape[1:], lambda i: (layer, 0, 0, 0, 0)),
        ],
        out_specs=pl.BlockSpec((2 * PEER_HEADS, N_KEYS, tb), lambda i: (0, 0, i)),
        out_shape=jax.ShapeDtypeStruct((2 * PEER_HEADS, N_KEYS, n), F32),
        compiler_params=_cparams(("parallel",)),
        name="peer_scores",
    )(xnt, w_pq_t, keys)


def _route_kernel(s_ref, r2_ref, e2_ref, c1_ref, e1_ref, *, hr):
    def emit(h, exact):
        rank2, e2, c1, e1, redo = _route_lanes(s_ref[2 * h], s_ref[2 * h + 1], exact)
        r2_ref[h] = rank2.astype(BF16)
        e2_ref[h] = e2.astype(BF16)
        c1_ref[h] = c1
        e1_ref[h] = e1
        return redo

    redo = [emit(h, exact=False) for h in range(hr)]
    for h in range(hr):
        @pl.when(jnp.max(redo[h]) > 0.0)
        def _():
            emit(h, exact=True)


def _peer_route(scores, hr=HR_ROUTE):
    n = scores.shape[2]
    tb = min(TB_ROUTE, n)
    out = jax.ShapeDtypeStruct((PEER_HEADS, N_KEYS, n), F32)
    out16 = jax.ShapeDtypeStruct((PEER_HEADS, N_KEYS, n), BF16)
    ospec = pl.BlockSpec((hr, N_KEYS, tb), lambda i, h: (h, 0, i))
    return pl.pallas_call(
        functools.partial(_route_kernel, hr=hr),
        grid=(n // tb, PEER_HEADS // hr),
        in_specs=[pl.BlockSpec((2 * hr, N_KEYS, tb), lambda i, h: (h, 0, i))],
        out_specs=[ospec, ospec, ospec, ospec],
        out_shape=[out16, out16, out, out],
        compiler_params=_cparams(("parallel", "parallel")),
        name="peer_route",
    )(scores)


def _gelu(x):
    return 0.5 * x * (1.0 + lax.erf(x * (0.5 ** 0.5)))


def _peer_kernel(xnt_ref, u_ref, un_ref, vt_ref, r2_ref, e2_ref, c1_ref, e1_ref, y_ref, a_ref, coef_ref, *,
                 eb, ck, lc):
    e = pl.program_id(1)
    tb = xnt_ref.shape[1]
    per = ck // N_KEYS
    nck = eb // ck

    @pl.when(e == 0)
    def _():
        y_ref[...] = jnp.zeros_like(y_ref)
        a_ref[0] = jnp.dot(u_ref[0:ck, :], xnt_ref[...], preferred_element_type=F32)

    for k in range(nck):
        if k + 1 < nck:
            a_next = jnp.dot(u_ref[(k + 1) * ck:(k + 2) * ck, :], xnt_ref[...], preferred_element_type=F32)
            a_ref[k + 1] = a_next
        else:
            a_next = jnp.dot(un_ref[...], xnt_ref[...], preferred_element_type=F32)
        bits = pltpu.bitcast(a_next[0:1, tb - lc:], jnp.uint32)
        bits = lax.shift_right_logical(lax.shift_right_logical(bits, jnp.uint32(16)), jnp.uint32(16))
        zero = pltpu.bitcast(bits, F32)
        for ii in range(per):
            i = k * per + ii
            rows = slice(ii * N_KEYS, (ii + 1) * N_KEYS)
            for l0 in range(0, tb, lc):
                cols = slice(l0, l0 + lc)
                gate = jnp.zeros((N_KEYS, lc), BF16)
                for h in range(PEER_HEADS):
                    c1 = jnp.broadcast_to(c1_ref[h, i:i + 1, cols].astype(BF16), (N_KEYS, lc))
                    e1 = jnp.broadcast_to(e1_ref[h, i:i + 1, cols].astype(BF16), (N_KEYS, lc))
                    zero16 = jnp.zeros((N_KEYS, lc), BF16)
                    gate = gate + jnp.where(r2_ref[h, :, cols] < c1, e2_ref[h, :, cols], zero16) * e1
                act = gate.astype(F32) * _gelu(a_ref[k, rows, cols])
                if ii == per - 1 and l0 + lc == tb:
                    act = act + zero
                coef_ref[k, rows, cols] = act.astype(BF16)
        if k + 1 == nck:
            a_ref[0] = a_next
        y_ref[...] += jnp.dot(vt_ref[:, k * ck:(k + 1) * ck], coef_ref[k], preferred_element_type=F32)


def _peer_sweep(xnt, u, vt, layer, r2, e2, c1, e1):
    d, n = xnt.shape
    tb = min(TB_PEER, n)
    eb = EB_PEER
    ck = CK_PEER
    lc = min(LC_PEER, tb)
    rspec = pl.BlockSpec((PEER_HEADS, N_KEYS, tb), lambda t, e: (0, 0, t))
    ispec = pl.BlockSpec((PEER_HEADS, eb // N_KEYS, tb), lambda t, e: (0, e, t))
    last_ck = N_EXPERTS // ck - 1
    return pl.pallas_call(
        functools.partial(_peer_kernel, eb=eb, ck=ck, lc=lc),
        grid=(n // tb, N_EXPERTS // eb),
        in_specs=[
            pl.BlockSpec((d, tb), lambda t, e: (0, t)),
            pl.BlockSpec((None, eb, d), lambda t, e: (layer, e, 0)),
            pl.BlockSpec((None, ck, d), lambda t, e: (layer, jnp.minimum((e + 1) * (eb // ck), last_ck), 0)),
            pl.BlockSpec((None, d, eb), lambda t, e: (layer, 0, e)),
            rspec, rspec, ispec, ispec,
        ],
        out_specs=pl.BlockSpec((d, tb), lambda t, e: (0, t)),
        out_shape=jax.ShapeDtypeStruct((d, n), F32),
        scratch_shapes=[pltpu.VMEM((eb // ck, ck, tb), F32), pltpu.VMEM((eb // ck, ck, tb), BF16)],
        compiler_params=_cparams(("parallel", "arbitrary")),
        name="peer_sweep",
    )(xnt, u, u, vt, r2, e2, c1, e1)


def _add_t_kernel(x_ref, yt_ref, g_ref, o_ref, *, final):
    x = x_ref[...] + yt_ref[...].T
    if final:
        x = _rms(x, g_ref[...])
    o_ref[...] = x


def _add_transposed(x1, yt, g, final):
    n, d = x1.shape
    tm = min(256, n)
    return pl.pallas_call(
        functools.partial(_add_t_kernel, final=final),
        grid=(n // tm,),
        in_specs=[
            pl.BlockSpec((tm, d), lambda i: (i, 0)),
            pl.BlockSpec((d, tm), lambda i: (0, i)),
            pl.BlockSpec((1, d), lambda i: (0, 0)),
        ],
        out_specs=pl.BlockSpec((tm, d), lambda i: (i, 0)),
        out_shape=jax.ShapeDtypeStruct((n, d), F32),
        compiler_params=_cparams(("parallel",)),
        name="peer_residual",
    )(x1, yt, g.reshape(1, d))


def _trunk(x, mem, k_cache, v_cache, s_cache, mk_cache, mv_cache, p):
    prompt = mem is not None
    bsz, t, d = x.shape
    n = bsz * t
    depth = p["g_norm1"].shape[0]
    x2 = x.reshape(n, d)
    new_k, new_v, new_s, new_mk, new_mv = [], [], [], [], []
    for i in range(depth):
        j = i // 2
        attn = i % 2 == 0
        w_in = p["w_in_a"] if attn else p["w_in_b"]
        w_out = p["w_out_a"] if attn else p["w_out_b"]
        z = _norm_matmul(x2, p["g_norm1"][i], w_in, j)
        z3 = z.reshape(bsz, t, z.shape[1])
        mq_blk = (z.shape[1] - MEM_W) // MEM_W

        if prompt:
            mkv = _norm_matmul(mem.reshape(bsz * N_MEM, d), p["g_mem"][i], p["w_mem_kv"], i)
            new_mk.append(mkv[:, :MEM_W].reshape(bsz, N_MEM, N_MEM_HEADS, HEAD_DIM))
            new_mv.append(mkv[:, MEM_W:].reshape(bsz, N_MEM, N_MEM_HEADS, HEAD_DIM))
            mkv3 = mkv.reshape(bsz, N_MEM, 2 * MEM_W)
            kspec = pl.BlockSpec((None, N_MEM, MEM_W), lambda b, q: (b, 0, 0))
            vspec = pl.BlockSpec((None, N_MEM, MEM_W), lambda b, q: (b, 0, 1))
            o_mem = _mem_attention(z3, mq_blk, mkv3, kspec, mkv3, vspec, head_major=False)
        else:
            cspec = pl.BlockSpec((None, None, N_MEM_HEADS, N_MEM, HEAD_DIM), lambda b, q, i=i: (i, b, 0, 0, 0))
            o_mem = _mem_attention(z3, mq_blk, jnp.transpose(mk_cache, (0, 1, 3, 2, 4)), cspec,
                                   jnp.transpose(mv_cache, (0, 1, 3, 2, 4)), cspec, head_major=True)

        if attn:
            k_new = z3[:, :, MIX_W:2 * MIX_W]
            v_new = z3[:, :, 2 * MIX_W:3 * MIX_W]
            if prompt:
                bias = _band_bias(p["rel_bias"][j], CHUNK, WINDOW + CHUNK, WINDOW)
                o_mix = _band_attention_prompt(z3, bias)
                keep = min(WINDOW, t)
                k_new, v_new = k_new[:, t - keep:], v_new[:, t - keep:]
            else:
                p_len = k_cache.shape[2]
                bias = _band_bias(p["rel_bias"][j], t, p_len + t, p_len)
                o_mix = _band_attention_sample(z3, k_cache, v_cache, j, bias)
            new_k.append(k_new.reshape(bsz, -1, N_MIX_HEADS, HEAD_DIM))
            new_v.append(v_new.reshape(bsz, -1, N_MIX_HEADS, HEAD_DIM))
        else:
            o_mix, s_new = _hgrn(z3, p["lb_param"], p["g_hgrn"][j], None if prompt else s_cache, j, i)
            new_s.append(s_new)

        x1, xnt = _out_proj(o_mix.reshape(n, MIX_W), o_mem.reshape(n, MEM_W), w_out, j, x2, p["g_norm2"][i])
        r2, e2, c1, e1 = _peer_route(_peer_scores(xnt, p["w_pq_t"], p["peer_keys"], i))
        yt = _peer_sweep(xnt, p["peer_u"], p["peer_v_t"], i, r2, e2, c1, e1)
        x2 = _add_transposed(x1, yt, p["g_final"], final=(i == depth - 1))

    y = x2.reshape(bsz, t, d)
    mk_out = jnp.stack(new_mk) if new_mk else None
    mv_out = jnp.stack(new_mv) if new_mv else None
    return y, jnp.stack(new_k), jnp.stack(new_v), jnp.stack(new_s), mk_out, mv_out


def kernel(x_prompt, x_sample, cache_attn_k, cache_attn_v, state_hgrn, cache_mem_k, cache_mem_v, mem_prompt,
           g_norm1, g_norm2, g_mem, g_final, w_in_a, w_out_a, rel_bias, w_in_b, w_out_b, lb_param, g_hgrn,
           w_mem_kv, w_pq, peer_keys, peer_u, peer_v):
    p = dict(
        g_norm1=g_norm1, g_norm2=g_norm2, g_mem=g_mem, g_final=g_final, rel_bias=rel_bias,
        lb_param=lb_param, g_hgrn=g_hgrn,
        w_in_a=w_in_a.astype(BF16), w_out_a=w_out_a.astype(BF16),
        w_in_b=w_in_b.astype(BF16), w_out_b=w_out_b.astype(BF16),
        w_mem_kv=w_mem_kv.astype(BF16),
        w_pq_t=jnp.swapaxes(w_pq, 1, 2).astype(BF16),
        peer_keys=peer_keys.astype(BF16),
        peer_u=peer_u.astype(BF16),
        peer_v_t=jnp.swapaxes(peer_v, 1, 2).astype(BF16),
    )
    y_p, k_p, v_p, s_p, mk_p, mv_p = _trunk(x_prompt, mem_prompt, None, None, None, None, None, p)
    y_s, k_s, v_s, s_s, _, _ = _trunk(x_sample, None, cache_attn_k, cache_attn_v, state_hgrn,
                                      cache_mem_k, cache_mem_v, p)
    return (y_p, y_s, k_p, v_p, k_s, v_s, s_p, s_s, mk_p, mv_p)
```

```python
import functools

import jax
import jax.numpy as jnp
from jax import lax
from jax.experimental import pallas as pl
from jax.experimental.pallas import tpu as pltpu

F32 = jnp.float32
BF16 = jnp.bfloat16

D_MODEL = 2048
PAST_LEN = 1024
CHUNK = 64
LEFT_CHUNKS = 8
WINDOW = LEFT_CHUNKS * CHUNK
HEAD_DIM = 128
N_MIX_HEADS = 12
N_MEM_HEADS = 4
MIX_W = N_MIX_HEADS * HEAD_DIM
MEM_W = N_MEM_HEADS * HEAD_DIM
N_MEM = 256
REL_CLIP = 256
PEER_HEADS = 8
N_KEYS = 128
N_EXPERTS = N_KEYS * N_KEYS
PEER_TOPK = 16
EPS = 1e-6
NEG_INF = -1e30

LANES = 128
SUBLANES = 8
SUB = SUBLANES
VMEM_LIMIT = 56 * 1024 * 1024

TM_PROJ = 1024
TN_PROJ = 1664
TB_ROUTE = 256
HR_ROUTE = 2
TB_PEER = 512
EB_PEER = 1024
CK_PEER = 512
LC_PEER = 256
HB_MIX = 4
QB_ATTN = 256
HB_HGRN = 12


def _cparams(sem, flags=None):
    return pltpu.CompilerParams(dimension_semantics=sem, vmem_limit_bytes=VMEM_LIMIT, flags=flags)


def _rms(x, g):
    return x * lax.rsqrt(jnp.mean(x * x, axis=-1, keepdims=True) + EPS) * g


def _norm_matmul_kernel(x_ref, g_ref, w_ref, o_ref, hn_ref):
    @pl.when(pl.program_id(1) == 0)
    def _():
        hn_ref[...] = _rms(x_ref[...], g_ref[...]).astype(BF16)

    o_ref[...] = jnp.dot(hn_ref[...], w_ref[...], preferred_element_type=F32)


def _norm_matmul(x, g, w, layer):
    m, d = x.shape
    n = w.shape[2]
    tm = min(TM_PROJ, m)
    tn = n // -(-n // TN_PROJ)
    assert n % tn == 0 and tn % LANES == 0
    return pl.pallas_call(
        _norm_matmul_kernel,
        grid=(m // tm, n // tn),
        in_specs=[
            pl.BlockSpec((tm, d), lambda i, j: (i, 0)),
            pl.BlockSpec((1, d), lambda i, j: (0, 0)),
            pl.BlockSpec((None, d, tn), lambda i, j: (layer, 0, j)),
        ],
        out_specs=pl.BlockSpec((tm, tn), lambda i, j: (i, j)),
        out_shape=jax.ShapeDtypeStruct((m, n), F32),
        scratch_shapes=[pltpu.VMEM((tm, d), BF16)],
        compiler_params=_cparams(("parallel", "arbitrary")),
        name="norm_matmul",
    )(x, g.reshape(1, d), w)


def _bias_kernel(tab_ref, o_ref, *, tq, tk, off, hg):
    h0 = pl.program_id(0) * hg
    clip = lambda d: max(min(d, REL_CLIP), -REL_CLIP) + REL_CLIP
    for c0 in range(0, tk, LANES):
        w = min(LANES, tk - c0)
        qi = lax.broadcasted_iota(jnp.int32, (tq, w), 0)
        ki = lax.broadcasted_iota(jnp.int32, (tq, w), 1) + c0
        rel = jnp.clip(qi - ki + off, -REL_CLIP, REL_CLIP) + REL_CLIP
        lo = clip(-(c0 + w - 1) + off)
        hi = clip((tq - 1) - c0 + off)

        def body(r, accs, rel=rel):
            hit = rel == r
            return tuple(jnp.where(hit, tab_ref[h0 + h, r], acc) for h, acc in enumerate(accs))

        accs = lax.fori_loop(lo, hi + 1, body, tuple(jnp.zeros((tq, w), F32) for _ in range(hg)))
        for h in range(hg):
            o_ref[h, :, c0:c0 + w] = accs[h]


def _band_bias(table, tq, tk, off, hg=4):
    nh = table.shape[1]
    return pl.pallas_call(
        functools.partial(_bias_kernel, tq=tq, tk=tk, off=off, hg=hg),
        grid=(nh // hg,),
        in_specs=[pl.BlockSpec(memory_space=pltpu.SMEM)],
        out_specs=pl.BlockSpec((hg, tq, tk), lambda g: (g, 0, 0)),
        out_shape=jax.ShapeDtypeStruct((nh, tq, tk), F32),
        compiler_params=_cparams(("arbitrary",)),
        name="band_bias",
    )(table.T)


def _softmax_rows(s):
    m = jnp.max(s, axis=-1, keepdims=True)
    p = jnp.exp(s - m)
    return p / jnp.sum(p, axis=-1, keepdims=True)


def _nt_dot(a, b):
    return lax.dot_general(a, b, (((1,), (1,)), ((), ())), preferred_element_type=F32)


def _tn_dot(a, b):
    return lax.dot_general(a, b, (((0,), (0,)), ((), ())), preferred_element_type=F32)


def _band_prompt_kernel(*refs, hb, qb, nwb):
    q_ref = refs[0]
    k_refs = refs[1:1 + nwb]
    v_refs = refs[1 + nwb:1 + 2 * nwb]
    bias_ref, o_ref = refs[1 + 2 * nwb:]
    c = pl.program_id(2)
    tk = WINDOW + qb
    k_pos = lax.broadcasted_iota(jnp.int32, (qb, tk), 1) + (c * qb - WINDOW)
    live = k_pos >= 0
    scale = HEAD_DIM ** -0.5
    scores = []
    for h in range(hb):
        sl = slice(h * HEAD_DIM, (h + 1) * HEAD_DIM)
        kw = jnp.concatenate([r[:, sl].astype(BF16) for r in k_refs], axis=0)
        scores.append(_nt_dot(q_ref[:, sl].astype(BF16), kw))
    for h in range(hb):
        sl = slice(h * HEAD_DIM, (h + 1) * HEAD_DIM)
        vw = jnp.concatenate([r[:, sl].astype(BF16) for r in v_refs], axis=0)
        s = scores[h] * scale + bias_ref[h]
        s = jnp.where(live, s, NEG_INF)
        p = _softmax_rows(s).astype(BF16)
        o_ref[:, sl] = jnp.dot(p, vw, preferred_element_type=F32).astype(BF16)


def _band_attention_prompt(z3, bias, hb=N_MIX_HEADS):
    b, s, _ = z3.shape
    qb = min(QB_ATTN, s)
    assert WINDOW % qb == 0
    nqc = qb // CHUNK
    nwb = WINDOW // qb + 1
    ng = N_MIX_HEADS // hb
    w = hb * HEAD_DIM
    tk = WINDOW + qb

    def win_spec(j, seg):
        return pl.BlockSpec((None, qb, w), lambda i, g, c, j=j, seg=seg:
                            (i, jnp.maximum(c + j - (nwb - 1), 0), seg * ng + g))
    bias_blk = jnp.concatenate(
        [jnp.pad(bias, ((0, 0), (0, 0), (j * CHUNK, (nqc - 1 - j) * CHUNK)), constant_values=NEG_INF)
         for j in range(nqc)], axis=1)
    return pl.pallas_call(
        functools.partial(_band_prompt_kernel, hb=hb, qb=qb, nwb=nwb),
        grid=(b, ng, s // qb),
        in_specs=[
            win_spec(nwb - 1, 0),
            *[win_spec(j, 1) for j in range(nwb)],
            *[win_spec(j, 2) for j in range(nwb)],
            pl.BlockSpec((hb, qb, tk), lambda i, g, c: (g, 0, 0), pipeline_mode=pl.Buffered(1)),
        ],
        out_specs=pl.BlockSpec((None, qb, w), lambda i, g, c: (i, c, g)),
        out_shape=jax.ShapeDtypeStruct((b, s, MIX_W), BF16),
        compiler_params=_cparams(("parallel", "parallel", "arbitrary")),
        name="band_attn_prompt",
    )(*([z3] * (1 + 2 * nwb)), bias_blk)


def _band_sample_kernel(*refs, hb, p_len):
    q_ref, kn_ref, vn_ref = refs[:3]
    kc_refs = refs[3:3 + hb]
    vc_refs = refs[3 + hb:3 + 2 * hb]
    bias_ref, o_ref = refs[3 + 2 * hb:]
    scale = HEAD_DIM ** -0.5
    scores = []
    for h in range(hb):
        sl = slice(h * HEAD_DIM, (h + 1) * HEAD_DIM)
        q = q_ref[:, sl].astype(BF16)
        scores.append((_nt_dot(q, kc_refs[h][...].astype(BF16)), _nt_dot(q, kn_ref[:, sl].astype(BF16))))
    for h in range(hb):
        sl = slice(h * HEAD_DIM, (h + 1) * HEAD_DIM)
        s_c = scores[h][0] * scale + bias_ref[h, :, :p_len]
        s_n = scores[h][1] * scale + bias_ref[h, :, p_len:]
        m = jnp.maximum(jnp.max(s_c, axis=-1, keepdims=True), jnp.max(s_n, axis=-1, keepdims=True))
        e_c = jnp.exp(s_c - m)
        e_n = jnp.exp(s_n - m)
        l = jnp.sum(e_c, axis=-1, keepdims=True) + jnp.sum(e_n, axis=-1, keepdims=True)
        o = jnp.dot((e_c / l).astype(BF16), vc_refs[h][...].astype(BF16), preferred_element_type=F32)
        o = o + jnp.dot((e_n / l).astype(BF16), vn_ref[:, sl].astype(BF16), preferred_element_type=F32)
        o_ref[:, sl] = o.astype(BF16)


def _band_attention_sample(z3, k_cache, v_cache, layer, bias, hb=N_MIX_HEADS):
    b, t, _ = z3.shape
    p_len = k_cache.shape[2]
    ng = N_MIX_HEADS // hb
    w = hb * HEAD_DIM
    k_cache = jnp.transpose(k_cache, (0, 1, 3, 2, 4))
    v_cache = jnp.transpose(v_cache, (0, 1, 3, 2, 4))

    def cache_spec(h):
        return pl.BlockSpec((None, None, None, p_len, HEAD_DIM), lambda i, g, h=h: (layer, i, g * hb + h, 0, 0))

    cache_specs = [cache_spec(h) for h in range(hb)]
    return pl.pallas_call(
        functools.partial(_band_sample_kernel, hb=hb, p_len=p_len),
        grid=(b, ng),
        in_specs=[
            pl.BlockSpec((None, t, w), lambda i, g: (i, 0, g)),
            pl.BlockSpec((None, t, w), lambda i, g: (i, 0, ng + g)),
            pl.BlockSpec((None, t, w), lambda i, g: (i, 0, 2 * ng + g)),
            *cache_specs, *cache_specs,
            pl.BlockSpec((hb, t, p_len + t), lambda i, g: (g, 0, 0)),
        ],
        out_specs=pl.BlockSpec((None, t, w), lambda i, g: (i, 0, g)),
        out_shape=jax.ShapeDtypeStruct((b, t, MIX_W), BF16),
        compiler_params=_cparams(("parallel", "parallel")),
        name="band_attn_sample",
    )(z3, z3, z3, *([k_cache] * hb), *([v_cache] * hb), bias)


def _mem_attn_kernel(q_ref, k_ref, v_ref, o_ref, *, head_major):
    scores = []
    for h in range(N_MEM_HEADS):
        sl = slice(h * HEAD_DIM, (h + 1) * HEAD_DIM)
        k = k_ref[h] if head_major else k_ref[:, sl]
        scores.append(_nt_dot(q_ref[:, sl].astype(BF16), k.astype(BF16)))
    for h in range(N_MEM_HEADS):
        sl = slice(h * HEAD_DIM, (h + 1) * HEAD_DIM)
        v = v_ref[h] if head_major else v_ref[:, sl]
        s = scores[h] * (HEAD_DIM ** -0.5)
        p = _softmax_rows(s).astype(BF16)
        o_ref[:, sl] = jnp.dot(p, v.astype(BF16), preferred_element_type=F32).astype(BF16)


def _mem_attention(z3, q_blk, karr, kspec, varr, vspec, head_major):
    b, t, _ = z3.shape
    tq = min(t, 512)
    return pl.pallas_call(
        functools.partial(_mem_attn_kernel, head_major=head_major),
        grid=(b, t // tq),
        in_specs=[
            pl.BlockSpec((None, tq, MEM_W), lambda i, j: (i, j, q_blk)),
            kspec, vspec,
        ],
        out_specs=pl.BlockSpec((None, tq, MEM_W), lambda i, j: (i, j, 0)),
        out_shape=jax.ShapeDtypeStruct((b, t, MEM_W), BF16),
        compiler_params=_cparams(("parallel", "arbitrary")),
        name="mem_attn",
    )(z3, karr, varr)


def _hgrn_kernel(*refs, hb, t, layer, has_state):
    if has_state:
        q_ref, f_ref, i_ref, g_ref, lbp_ref, gh_ref, s0_ref, o_ref, sout_ref, st_ref = refs
    else:
        q_ref, f_ref, i_ref, g_ref, lbp_ref, gh_ref, o_ref, sout_ref, st_ref = refs
    c = pl.program_id(2)
    nsub = t // SUB

    @pl.when(c == 0)
    def _():
        for h in range(hb):
            if has_state:
                st_ref[h] = s0_ref[h].T
            else:
                st_ref[h] = jnp.zeros((HEAD_DIM, HEAD_DIM), F32)

    row = lax.broadcasted_iota(jnp.int32, (t, t), 0)
    col = lax.broadcasted_iota(jnp.int32, (t, t), 1)
    rowd = lax.broadcasted_iota(jnp.int32, (t, HEAD_DIM), 0)
    hits = [jnp.logical_and(row - col == dl, jnp.bitwise_and(row, SUB - 1) >= dl) for dl in range(SUB)]
    far = jnp.right_shift(col, 3) < jnp.right_shift(row, 3)

    pending = []
    for h in range(hb):
        sl = slice(h * HEAD_DIM, (h + 1) * HEAD_DIM)
        q = q_ref[:, sl]
        ff = f_ref[:, sl]
        v = i_ref[:, sl]

        lbp = lbp_ref[:, sl]
        e = jnp.exp(lbp - jnp.max(lbp, axis=0, keepdims=True))
        soft = e / jnp.sum(e, axis=0, keepdims=True)
        acc = soft[0:1]
        for r in range(1, layer + 1):
            acc = acc + soft[r:r + 1]
        lb = acc - soft[0:1]

        en = jnp.exp(-jnp.abs(ff))
        rec = 1.0 / (1.0 + en)
        sig_pos = jnp.where(ff >= 0.0, rec, en * rec)
        sig_neg = jnp.where(ff >= 0.0, en * rec, rec)
        f = lb + (1.0 - lb) * sig_pos
        log_f = jnp.where(f > 0.0, jnp.log(f), ff)
        k = (1.0 - lb) * sig_neg

        b = log_f
        sh = 1
        while sh < t:
            b = b + jnp.where(rowd >= sh, pltpu.roll(b, sh, 0), 0.0)
            sh *= 2
        b_end = b[t - 1:t]

        st = st_ref[h]
        o = _nt_dot((q * jnp.exp(b)).astype(BF16), st.astype(BF16))

        sc = jnp.zeros((t, t), F32)
        for dl in range(SUB):
            if dl == 0:
                prod = q * k
            else:
                prod = q * pltpu.roll(k, dl, 0) * jnp.exp(jnp.minimum(b - pltpu.roll(b, dl, 0), 0.0))
            r = jnp.sum(prod, axis=1, keepdims=True)
            sc = jnp.where(hits[dl], r, sc)

        q_parts = []
        k_rows = []
        for i in range(nsub - 1):
            lo = (i + 1) * SUB
            b_ref = b[lo - 1:lo]
            later = q[lo:] * jnp.exp(jnp.minimum(b[lo:] - b_ref, 0.0))
            q_parts.append(jnp.concatenate([jnp.zeros((lo, HEAD_DIM), F32), later], axis=0).astype(BF16))
            kh = k[lo - SUB:lo] * jnp.exp(jnp.minimum(b_ref - b[lo - SUB:lo], 0.0))
            zero = jnp.zeros_like(kh)
            k_rows.append(jnp.concatenate([kh if j == i else zero for j in range(nsub - 1)], axis=1))
        k_rows.append(jnp.zeros((SUB, (nsub - 1) * HEAD_DIM), F32))
        q_cat = jnp.concatenate(q_parts, axis=1)
        k_bd = jnp.concatenate(k_rows, axis=0).astype(BF16)
        sc_far = _nt_dot(q_cat, k_bd)
        scores = jnp.where(far, sc_far, sc)

        k_dec = (k * jnp.exp(jnp.minimum(b_end - b, 0.0))).astype(BF16)
        st_ref[h] = st * jnp.exp(b_end) + _tn_dot(v.astype(BF16), k_dec)
        pending.append((o, scores.astype(BF16)))

    for h in range(hb):
        sl = slice(h * HEAD_DIM, (h + 1) * HEAD_DIM)
        o, scores = pending[h]
        g = g_ref[:, sl]
        o = o + jnp.dot(scores, i_ref[:, sl].astype(BF16), preferred_element_type=F32)
        o = _rms(o, gh_ref[:, sl])
        o = o * (g * jax.nn.sigmoid(g))
        o_ref[:, sl] = o.astype(BF16)

    @pl.when(c == pl.num_programs(2) - 1)
    def _():
        for h in range(hb):
            sout_ref[h] = st_ref[h].T


def _hgrn(z3, lb_param, g_head, state, state_idx, layer, hb=HB_HGRN):
    assert SUB == 8
    b, s, _ = z3.shape
    t = min(CHUNK, s)
    nc = s // t
    ng = N_MIX_HEADS // hb
    w = hb * HEAD_DIM
    depth = lb_param.shape[0]
    has_state = state is not None

    def zspec(seg):
        return pl.BlockSpec((None, t, w), lambda i, g, c, seg=seg: (i, c, seg * ng + g))

    in_specs = [zspec(0), zspec(1), zspec(2), zspec(3),
                pl.BlockSpec((depth, w), lambda i, g, c: (0, g)),
                pl.BlockSpec((1, w), lambda i, g, c: (0, g))]
    args = [z3, z3, z3, z3, lb_param, g_head.reshape(1, MIX_W)]
    if has_state:
        in_specs.append(pl.BlockSpec((None, None, hb, HEAD_DIM, HEAD_DIM),
                                     lambda i, g, c: (state_idx, i, g, 0, 0)))
        args.append(state)
    return pl.pallas_call(
        functools.partial(_hgrn_kernel, hb=hb, t=t, layer=layer, has_state=has_state),
        grid=(b, ng, nc),
        in_specs=in_specs,
        out_specs=[
            pl.BlockSpec((None, t, w), lambda i, g, c: (i, c, g)),
            pl.BlockSpec((None, hb, HEAD_DIM, HEAD_DIM), lambda i, g, c: (i, g, 0, 0)),
        ],
        out_shape=[
            jax.ShapeDtypeStruct((b, s, MIX_W), BF16),
            jax.ShapeDtypeStruct((b, N_MIX_HEADS, HEAD_DIM, HEAD_DIM), F32),
        ],
        scratch_shapes=[pltpu.VMEM((hb, HEAD_DIM, HEAD_DIM), F32)],
        compiler_params=_cparams(("parallel", "parallel", "arbitrary")),
        name="hgrn2",
    )(*args)


def _out_proj_kernel(om_ref, oe_ref, wm_ref, we_ref, x_ref, g_ref, x1_ref, xnt_ref):
    x1 = x_ref[...] + jnp.dot(om_ref[...], wm_ref[...], preferred_element_type=F32)
    x1 = x1 + jnp.dot(oe_ref[...], we_ref[...], preferred_element_type=F32)
    x1_ref[...] = x1
    xnt_ref[...] = _rms(x1, g_ref[...]).T.astype(BF16)


def _out_proj(o_mix, o_mem, w_out, layer, x, g2):
    n, d = x.shape
    tm = min(512, n)
    once = pl.Buffered(1)
    return pl.pallas_call(
        _out_proj_kernel,
        grid=(n // tm,),
        in_specs=[
            pl.BlockSpec((tm, MIX_W), lambda i: (i, 0)),
            pl.BlockSpec((tm, MEM_W), lambda i: (i, 0)),
            pl.BlockSpec((None, MIX_W, d), lambda i: (layer, 0, 0), pipeline_mode=once),
            pl.BlockSpec((None, MEM_W, d), lambda i: (layer, MIX_W // MEM_W, 0), pipeline_mode=once),
            pl.BlockSpec((tm, d), lambda i: (i, 0)),
            pl.BlockSpec((1, d), lambda i: (0, 0)),
        ],
        out_specs=[
            pl.BlockSpec((tm, d), lambda i: (i, 0)),
            pl.BlockSpec((d, tm), lambda i: (0, i)),
        ],
        out_shape=[jax.ShapeDtypeStruct((n, d), F32), jax.ShapeDtypeStruct((d, n), BF16)],
        compiler_params=_cparams(("parallel",)),
        name="out_proj",
    )(o_mix, o_mem, w_out, w_out, x, g2.reshape(1, d))


_REMOVED = 2.0 ** 100


def _extract_exact(vals, order, rounds):
    rank = jnp.full(vals.shape, float(PEER_TOPK), F32)
    tops = []
    for r in range(rounds):
        m = jnp.max(vals, axis=0, keepdims=True)
        pick = jnp.min(jnp.where(vals == m, order, 1e9), axis=0, keepdims=True)
        sel = order == pick
        rank = jnp.where(sel, float(r), rank)
        vals = jnp.where(sel, -jnp.inf, vals)
        tops.append(m)
    return jnp.concatenate(tops, axis=0), rank


def _extract_fast(vals, rounds):
    tops = []
    for r in range(rounds):
        m = jnp.max(vals, axis=0, keepdims=True)
        vals = jnp.where(vals == m, -_REMOVED * (r + 1), vals)
        tops.append(m)
    rank = jnp.where(vals <= -_REMOVED, vals * (-1.0 / _REMOVED) - 1.0, float(PEER_TOPK))
    return jnp.concatenate(tops, axis=0), rank


def _route_lanes(s1, s2, exact):
    lanes = s1.shape[1]
    if exact:
        key_id = lax.broadcasted_iota(jnp.int32, (N_KEYS, lanes), 0).astype(F32)
        top1, rank1 = _extract_exact(s1, key_id, PEER_TOPK)
        top2, rank2 = _extract_exact(s2, key_id, PEER_TOPK)
    else:
        top1, rank1 = _extract_fast(s1, PEER_TOPK)
        top2, rank2 = _extract_fast(s2, PEER_TOPK)

    blk_a = top1[0:1] + top2
    blk_b = [top1[a:a + 1] + top2[0:8] for a in range(1, 8)]
    blk_c = top1[8:16] + top2[0:1]
    cand = jnp.concatenate([blk_a] + blk_b + [blk_c], axis=0)
    nrow = cand.shape[0]
    r = lax.broadcasted_iota(jnp.int32, (nrow, lanes), 0)
    a_mid = 1 + jnp.right_shift(r - 16, 3)
    b_mid = jnp.bitwise_and(r - 16, 7)
    valid = jnp.logical_or(jnp.logical_or(r < 16, r >= 72), (a_mid + 1) * (b_mid + 1) <= PEER_TOPK)
    cand = jnp.where(valid, cand, -jnp.inf)
    if exact:
        cand_id = jnp.where(r < 16, r, jnp.where(r < 72, a_mid * PEER_TOPK + b_mid, (r - 64) * PEER_TOPK))
        best, crank = _extract_exact(cand, cand_id.astype(F32), PEER_TOPK)
    else:
        best, crank = _extract_fast(cand, PEER_TOPK)
    chosen = jnp.logical_and(valid, crank < PEER_TOPK).astype(F32)
    z = jnp.sum(jnp.exp(best - best[0:1]), axis=0, keepdims=True)

    counts = [jnp.sum(chosen[0:16], axis=0, keepdims=True)]
    counts += [jnp.sum(chosen[16 + 8 * (a - 1):16 + 8 * a], axis=0, keepdims=True) for a in range(1, 8)]
    counts += [chosen[72 + a - 8:72 + a - 7] for a in range(8, 16)]
    c1 = jnp.zeros((N_KEYS, lanes), F32)
    for a in range(PEER_TOPK):
        c1 = jnp.where(rank1 == a, counts[a], c1)

    ranked = (jnp.sum((rank1 < PEER_TOPK).astype(F32), axis=0, keepdims=True)
              + jnp.sum((rank2 < PEER_TOPK).astype(F32), axis=0, keepdims=True)
              + jnp.sum(chosen, axis=0, keepdims=True))
    low = jnp.minimum(jnp.min(s1, axis=0, keepdims=True), jnp.min(s2, axis=0, keepdims=True))
    redo = jnp.logical_or(ranked != 3.0 * PEER_TOPK, jnp.logical_not(low > -0.25 * _REMOVED)).astype(F32)
    e1 = jnp.exp(s1 - top1[0:1])
    e2 = jnp.exp(s2 - top2[0:1]) / z
    return rank2, e2, c1, e1, redo


def _scores_kernel(xnt_ref, wq_ref, keys_ref, s_ref):
    qq = jnp.dot(wq_ref[...], xnt_ref[...], preferred_element_type=F32).astype(BF16)
    for h in range(PEER_HEADS):
        for c in range(2):
            row = (2 * h + c) * N_KEYS
            s_ref[2 * h + c] = jnp.dot(keys_ref[c, h], qq[row:row + N_KEYS], preferred_element_type=F32)


def _peer_scores(xnt, w_pq_t, keys, layer):
    d, n = xnt.shape
    tb = min(512, n)
    return pl.pallas_call(
        _scores_kernel,
        grid=(n // tb,),
        in_specs=[
            pl.BlockSpec((d, tb), lambda i: (0, i)),
            pl.BlockSpec((None,) + w_pq_t.shape[1:], lambda i: (layer, 0, 0)),
            pl.BlockSpec((None,) + keys.shape[1:], lambda i: (layer, 0, 0, 0, 0)),
        ],
        out_specs=pl.BlockSpec((2 * PEER_HEADS, N_KEYS, tb), lambda i: (0, 0, i)),
        out_shape=jax.ShapeDtypeStruct((2 * PEER_HEADS, N_KEYS, n), F32),
        compiler_params=_cparams(("parallel",)),
        name="peer_scores",
    )(xnt, w_pq_t, keys)


def _route_kernel(s_ref, r2_ref, e2_ref, c1_ref, e1_ref, *, hr):
    def emit(h, exact):
        rank2, e2, c1, e1, redo = _route_lanes(s_ref[2 * h], s_ref[2 * h + 1], exact)
        r2_ref[h] = rank2.astype(BF16)
        e2_ref[h] = e2.astype(BF16)
        c1_ref[h] = c1
        e1_ref[h] = e1
        return redo

    redo = [emit(h, exact=False) for h in range(hr)]
    for h in range(hr):
        @pl.when(jnp.max(redo[h]) > 0.0)
        def _():
            emit(h, exact=True)


def _peer_route(scores, hr=HR_ROUTE):
    n = scores.shape[2]
    tb = min(TB_ROUTE, n)
    out = jax.ShapeDtypeStruct((PEER_HEADS, N_KEYS, n), F32)
    out16 = jax.ShapeDtypeStruct((PEER_HEADS, N_KEYS, n), BF16)
    ospec = pl.BlockSpec((hr, N_KEYS, tb), lambda i, h: (h, 0, i))
    return pl.pallas_call(
        functools.partial(_route_kernel, hr=hr),
        grid=(n // tb, PEER_HEADS // hr),
        in_specs=[pl.BlockSpec((2 * hr, N_KEYS, tb), lambda i, h: (h, 0, i))],
        out_specs=[ospec, ospec, ospec, ospec],
        out_shape=[out16, out16, out, out],
        compiler_params=_cparams(("parallel", "parallel")),
        name="peer_route",
    )(scores)


def _gelu(x):
    return 0.5 * x * (1.0 + lax.erf(x * (0.5 ** 0.5)))


def _peer_kernel(xnt_ref, u_ref, un_ref, vt_ref, r2_ref, e2_ref, c1_ref, e1_ref, y_ref, a_ref, coef_ref, *,
                 eb, ck, lc):
    e = pl.program_id(1)
    tb = xnt_ref.shape[1]
    per = ck // N_KEYS
    nck = eb // ck

    @pl.when(e == 0)
    def _():
        y_ref[...] = jnp.zeros_like(y_ref)
        a_ref[0] = jnp.dot(u_ref[0:ck, :], xnt_ref[...], preferred_element_type=F32)

    for k in range(nck):
        if k + 1 < nck:
            a_next = jnp.dot(u_ref[(k + 1) * ck:(k + 2) * ck, :], xnt_ref[...], preferred_element_type=F32)
            a_ref[k + 1] = a_next
        else:
            a_next = jnp.dot(un_ref[...], xnt_ref[...], preferred_element_type=F32)
        bits = pltpu.bitcast(a_next[0:1, tb - lc:], jnp.uint32)
        bits = lax.shift_right_logical(lax.shift_right_logical(bits, jnp.uint32(16)), jnp.uint32(16))
        zero = pltpu.bitcast(bits, F32)
        for ii in range(per):
            i = k * per + ii
            rows = slice(ii * N_KEYS, (ii + 1) * N_KEYS)
            for l0 in range(0, tb, lc):
                cols = slice(l0, l0 + lc)
                gate = jnp.zeros((N_KEYS, lc), BF16)
                for h in range(PEER_HEADS):
                    c1 = jnp.broadcast_to(c1_ref[h, i:i + 1, cols], (N_KEYS, lc)).astype(BF16)
                    e1 = jnp.broadcast_to(e1_ref[h, i:i + 1, cols], (N_KEYS, lc)).astype(BF16)
                    zero16 = jnp.zeros((N_KEYS, lc), BF16)
                    gate = gate + jnp.where(r2_ref[h, :, cols] < c1, e2_ref[h, :, cols], zero16) * e1
                act = gate.astype(F32) * _gelu(a_ref[k, rows, cols])
                if ii == per - 1 and l0 + lc == tb:
                    act = act + zero
                coef_ref[k, rows, cols] = act.astype(BF16)
        if k + 1 == nck:
            a_ref[0] = a_next
        y_ref[...] += jnp.dot(vt_ref[:, k * ck:(k + 1) * ck], coef_ref[k], preferred_element_type=F32)


def _peer_sweep(xnt, u, vt, layer, r2, e2, c1, e1):
    d, n = xnt.shape
    tb = min(TB_PEER, n)
    eb = EB_PEER
    ck = CK_PEER
    lc = min(LC_PEER, tb)
    rspec = pl.BlockSpec((PEER_HEADS, N_KEYS, tb), lambda t, e: (0, 0, t))
    ispec = pl.BlockSpec((PEER_HEADS, eb // N_KEYS, tb), lambda t, e: (0, e, t))
    last_ck = N_EXPERTS // ck - 1
    return pl.pallas_call(
        functools.partial(_peer_kernel, eb=eb, ck=ck, lc=lc),
        grid=(n // tb, N_EXPERTS // eb),
        in_specs=[
            pl.BlockSpec((d, tb), lambda t, e: (0, t)),
            pl.BlockSpec((None, eb, d), lambda t, e: (layer, e, 0)),
            pl.BlockSpec((None, ck, d), lambda t, e: (layer, jnp.minimum((e + 1) * (eb // ck), last_ck), 0)),
            pl.BlockSpec((None, d, eb), lambda t, e: (layer, 0, e)),
            rspec, rspec, ispec, ispec,
        ],
        out_specs=pl.BlockSpec((d, tb), lambda t, e: (0, t)),
        out_shape=jax.ShapeDtypeStruct((d, n), F32),
        scratch_shapes=[pltpu.VMEM((eb // ck, ck, tb), F32), pltpu.VMEM((eb // ck, ck, tb), BF16)],
        compiler_params=_cparams(("parallel", "arbitrary")),
        name="peer_sweep",
    )(xnt, u, u, vt, r2, e2, c1, e1)


def _add_t_kernel(x_ref, yt_ref, g_ref, o_ref, *, final):
    x = x_ref[...] + yt_ref[...].T
    if final:
        x = _rms(x, g_ref[...])
    o_ref[...] = x


def _add_transposed(x1, yt, g, final):
    n, d = x1.shape
    tm = min(256, n)
    return pl.pallas_call(
        functools.partial(_add_t_kernel, final=final),
        grid=(n // tm,),
        in_specs=[
            pl.BlockSpec((tm, d), lambda i: (i, 0)),
            pl.BlockSpec((d, tm), lambda i: (0, i)),
            pl.BlockSpec((1, d), lambda i: (0, 0)),
        ],
        out_specs=pl.BlockSpec((tm, d), lambda i: (i, 0)),
        out_shape=jax.ShapeDtypeStruct((n, d), F32),
        compiler_params=_cparams(("parallel",)),
        name="peer_residual",
    )(x1, yt, g.reshape(1, d))


def _trunk(x, mem, k_cache, v_cache, s_cache, mk_cache, mv_cache, p):
    prompt = mem is not None
    bsz, t, d = x.shape
    n = bsz * t
    depth = p["g_norm1"].shape[0]
    x2 = x.reshape(n, d)
    new_k, new_v, new_s, new_mk, new_mv = [], [], [], [], []
    for i in range(depth):
        j = i // 2
        attn = i % 2 == 0
        w_in = p["w_in_a"] if attn else p["w_in_b"]
        w_out = p["w_out_a"] if attn else p["w_out_b"]
        z = _norm_matmul(x2, p["g_norm1"][i], w_in, j)
        z3 = z.reshape(bsz, t, z.shape[1])
        mq_blk = (z.shape[1] - MEM_W) // MEM_W

        if prompt:
            mkv = _norm_matmul(mem.reshape(bsz * N_MEM, d), p["g_mem"][i], p["w_mem_kv"], i)
            new_mk.append(mkv[:, :MEM_W].reshape(bsz, N_MEM, N_MEM_HEADS, HEAD_DIM))
            new_mv.append(mkv[:, MEM_W:].reshape(bsz, N_MEM, N_MEM_HEADS, HEAD_DIM))
            mkv3 = mkv.reshape(bsz, N_MEM, 2 * MEM_W)
            kspec = pl.BlockSpec((None, N_MEM, MEM_W), lambda b, q: (b, 0, 0))
            vspec = pl.BlockSpec((None, N_MEM, MEM_W), lambda b, q: (b, 0, 1))
            o_mem = _mem_attention(z3, mq_blk, mkv3, kspec, mkv3, vspec, head_major=False)
        else:
            cspec = pl.BlockSpec((None, None, N_MEM_HEADS, N_MEM, HEAD_DIM), lambda b, q, i=i: (i, b, 0, 0, 0))
            o_mem = _mem_attention(z3, mq_blk, jnp.transpose(mk_cache, (0, 1, 3, 2, 4)), cspec,
                                   jnp.transpose(mv_cache, (0, 1, 3, 2, 4)), cspec, head_major=True)

        if attn:
            k_new = z3[:, :, MIX_W:2 * MIX_W]
            v_new = z3[:, :, 2 * MIX_W:3 * MIX_W]
            if prompt:
                bias = _band_bias(p["rel_bias"][j], CHUNK, WINDOW + CHUNK, WINDOW)
                o_mix = _band_attention_prompt(z3, bias)
                keep = min(WINDOW, t)
                k_new, v_new = k_new[:, t - keep:], v_new[:, t - keep:]
            else:
                p_len = k_cache.shape[2]
                bias = _band_bias(p["rel_bias"][j], t, p_len + t, p_len)
                o_mix = _band_attention_sample(z3, k_cache, v_cache, j, bias)
            new_k.append(k_new.reshape(bsz, -1, N_MIX_HEADS, HEAD_DIM))
            new_v.append(v_new.reshape(bsz, -1, N_MIX_HEADS, HEAD_DIM))
        else:
            o_mix, s_new = _hgrn(z3, p["lb_param"], p["g_hgrn"][j], None if prompt else s_cache, j, i)
            new_s.append(s_new)

        x1, xnt = _out_proj(o_mix.reshape(n, MIX_W), o_mem.reshape(n, MEM_W), w_out, j, x2, p["g_norm2"][i])
        r2, e2, c1, e1 = _peer_route(_peer_scores(xnt, p["w_pq_t"], p["peer_keys"], i))
        yt = _peer_sweep(xnt, p["peer_u"], p["peer_v_t"], i, r2, e2, c1, e1)
        x2 = _add_transposed(x1, yt, p["g_final"], final=(i == depth - 1))

    y = x2.reshape(bsz, t, d)
    mk_out = jnp.stack(new_mk) if new_mk else None
    mv_out = jnp.stack(new_mv) if new_mv else None
    return y, jnp.stack(new_k), jnp.stack(new_v), jnp.stack(new_s), mk_out, mv_out


def kernel(x_prompt, x_sample, cache_attn_k, cache_attn_v, state_hgrn, cache_mem_k, cache_mem_v, mem_prompt,
           g_norm1, g_norm2, g_mem, g_final, w_in_a, w_out_a, rel_bias, w_in_b, w_out_b, lb_param, g_hgrn,
           w_mem_kv, w_pq, peer_keys, peer_u, peer_v):
    p = dict(
        g_norm1=g_norm1, g_norm2=g_norm2, g_mem=g_mem, g_final=g_final, rel_bias=rel_bias,
        lb_param=lb_param, g_hgrn=g_hgrn,
        w_in_a=w_in_a.astype(BF16), w_out_a=w_out_a.astype(BF16),
        w_in_b=w_in_b.astype(BF16), w_out_b=w_out_b.astype(BF16),
        w_mem_kv=w_mem_kv.astype(BF16),
        w_pq_t=jnp.swapaxes(w_pq, 1, 2).astype(BF16),
        peer_keys=peer_keys.astype(BF16),
        peer_u=peer_u.astype(BF16),
        peer_v_t=jnp.swapaxes(peer_v, 1, 2).astype(BF16),
    )
    y_p, k_p, v_p, s_p, mk_p, mv_p = _trunk(x_prompt, mem_prompt, None, None, None, None, None, p)
    y_s, k_s, v_s, s_s, _, _ = _trunk(x_sample, None, cache_attn_k, cache_attn_v, state_hgrn,
                                      cache_mem_k, cache_mem_v, p)
    return (y_p, y_s, k_p, v_p, k_s, v_s, s_p, s_s, mk_p, mv_p)
```

```python
import functools

import jax
import jax.numpy as jnp
from jax import lax
from jax.experimental import pallas as pl
from jax.experimental.pallas import tpu as pltpu

F32 = jnp.float32
BF16 = jnp.bfloat16

PAST_LEN = 1024
CHUNK = 64
LEFT_CHUNKS = 8
WINDOW = LEFT_CHUNKS * CHUNK
HEAD_DIM = 128
N_MIX_HEADS = 12
N_MEM_HEADS = 4
MIX_W = N_MIX_HEADS * HEAD_DIM
MEM_W = N_MEM_HEADS * HEAD_DIM
N_MEM = 256
REL_CLIP = 256
PEER_HEADS = 8
N_KEYS = 128
N_EXPERTS = N_KEYS * N_KEYS
PEER_TOPK = 16
EPS = 1e-6
NEG_INF = -1e30

LANES = 128
SUBLANES = 8
SUB = SUBLANES
VMEM_LIMIT = 56 * 1024 * 1024

TM_PROJ = 1024
TN_PROJ = 1664
TM_OUT = 512
TM_RES = 256
TQ_MEM = 512
HG_BIAS = 4
TB_SCORES = 512
TB_ROUTE = 256
HR_ROUTE = 2
TB_PEER = 512
EB_PEER = 1024
CK_PEER = 512
LC_PEER = 256
QB_ATTN = 256
HB_HGRN = 12


def _cparams(sem):
    return pltpu.CompilerParams(dimension_semantics=sem, vmem_limit_bytes=VMEM_LIMIT)


def _rms(x, g):
    return x * lax.rsqrt(jnp.mean(x * x, axis=-1, keepdims=True) + EPS) * g


def _norm_matmul_kernel(x_ref, g_ref, w_ref, o_ref, hn_ref):
    @pl.when(pl.program_id(1) == 0)
    def _():
        hn_ref[...] = _rms(x_ref[...], g_ref[...]).astype(BF16)

    o_ref[...] = jnp.dot(hn_ref[...], w_ref[...], preferred_element_type=F32)


def _norm_matmul(x, g, w, layer):
    m, d = x.shape
    n = w.shape[2]
    tm = min(TM_PROJ, m)
    tn = n // -(-n // TN_PROJ)
    assert n % tn == 0 and tn % LANES == 0
    return pl.pallas_call(
        _norm_matmul_kernel,
        grid=(m // tm, n // tn),
        in_specs=[
            pl.BlockSpec((tm, d), lambda i, j: (i, 0)),
            pl.BlockSpec((1, d), lambda i, j: (0, 0)),
            pl.BlockSpec((None, d, tn), lambda i, j: (layer, 0, j)),
        ],
        out_specs=pl.BlockSpec((tm, tn), lambda i, j: (i, j)),
        out_shape=jax.ShapeDtypeStruct((m, n), F32),
        scratch_shapes=[pltpu.VMEM((tm, d), BF16)],
        compiler_params=_cparams(("parallel", "arbitrary")),
        name="norm_matmul",
    )(x, g.reshape(1, d), w)


def _bias_kernel(tab_ref, o_ref, *, tq, tk, off, hg):
    h0 = pl.program_id(0) * hg
    clip = lambda d: max(min(d, REL_CLIP), -REL_CLIP) + REL_CLIP
    for c0 in range(0, tk, LANES):
        w = min(LANES, tk - c0)
        qi = lax.broadcasted_iota(jnp.int32, (tq, w), 0)
        ki = lax.broadcasted_iota(jnp.int32, (tq, w), 1) + c0
        rel = jnp.clip(qi - ki + off, -REL_CLIP, REL_CLIP) + REL_CLIP
        lo = clip(-(c0 + w - 1) + off)
        hi = clip((tq - 1) - c0 + off)

        def body(r, accs, rel=rel):
            hit = rel == r
            return tuple(jnp.where(hit, tab_ref[h0 + h, r], acc) for h, acc in enumerate(accs))

        accs = lax.fori_loop(lo, hi + 1, body, tuple(jnp.zeros((tq, w), F32) for _ in range(hg)))
        for h in range(hg):
            o_ref[h, :, c0:c0 + w] = accs[h]


def _band_bias(table, tq, tk, off, hg=HG_BIAS):
    nh = table.shape[1]
    return pl.pallas_call(
        functools.partial(_bias_kernel, tq=tq, tk=tk, off=off, hg=hg),
        grid=(nh // hg,),
        in_specs=[pl.BlockSpec(memory_space=pltpu.SMEM)],
        out_specs=pl.BlockSpec((hg, tq, tk), lambda g: (g, 0, 0)),
        out_shape=jax.ShapeDtypeStruct((nh, tq, tk), F32),
        compiler_params=_cparams(("arbitrary",)),
        name="band_bias",
    )(table.T)


def _softmax_rows(s):
    m = jnp.max(s, axis=-1, keepdims=True)
    p = jnp.exp(s - m)
    return p / jnp.sum(p, axis=-1, keepdims=True)


def _nt_dot(a, b):
    return lax.dot_general(a, b, (((1,), (1,)), ((), ())), preferred_element_type=F32)


def _tn_dot(a, b):
    return lax.dot_general(a, b, (((0,), (0,)), ((), ())), preferred_element_type=F32)


def _band_prompt_kernel(*refs, hb, qb, nwb):
    q_ref = refs[0]
    k_refs = refs[1:1 + nwb]
    v_refs = refs[1 + nwb:1 + 2 * nwb]
    bias_ref, o_ref = refs[1 + 2 * nwb:]
    c = pl.program_id(2)
    tk = WINDOW + qb
    k_pos = lax.broadcasted_iota(jnp.int32, (qb, tk), 1) + (c * qb - WINDOW)
    live = k_pos >= 0
    scale = HEAD_DIM ** -0.5
    scores = []
    for h in range(hb):
        sl = slice(h * HEAD_DIM, (h + 1) * HEAD_DIM)
        kw = jnp.concatenate([r[:, sl].astype(BF16) for r in k_refs], axis=0)
        scores.append(_nt_dot(q_ref[:, sl].astype(BF16), kw))
    for h in range(hb):
        sl = slice(h * HEAD_DIM, (h + 1) * HEAD_DIM)
        vw = jnp.concatenate([r[:, sl].astype(BF16) for r in v_refs], axis=0)
        s = scores[h] * scale + bias_ref[h]
        s = jnp.where(live, s, NEG_INF)
        p = _softmax_rows(s).astype(BF16)
        o_ref[:, sl] = jnp.dot(p, vw, preferred_element_type=F32).astype(BF16)


def _band_attention_prompt(z3, bias, hb=N_MIX_HEADS):
    b, s, _ = z3.shape
    qb = min(QB_ATTN, s)
    assert WINDOW % qb == 0
    nqc = qb // CHUNK
    nwb = WINDOW // qb + 1
    ng = N_MIX_HEADS // hb
    w = hb * HEAD_DIM
    tk = WINDOW + qb

    def win_spec(j, seg):
        return pl.BlockSpec((None, qb, w), lambda i, g, c, j=j, seg=seg:
                            (i, jnp.maximum(c + j - (nwb - 1), 0), seg * ng + g))
    bias_blk = jnp.concatenate(
        [jnp.pad(bias, ((0, 0), (0, 0), (j * CHUNK, (nqc - 1 - j) * CHUNK)), constant_values=NEG_INF)
         for j in range(nqc)], axis=1)
    return pl.pallas_call(
        functools.partial(_band_prompt_kernel, hb=hb, qb=qb, nwb=nwb),
        grid=(b, ng, s // qb),
        in_specs=[
            win_spec(nwb - 1, 0),
            *[win_spec(j, 1) for j in range(nwb)],
            *[win_spec(j, 2) for j in range(nwb)],
            pl.BlockSpec((hb, qb, tk), lambda i, g, c: (g, 0, 0), pipeline_mode=pl.Buffered(1)),
        ],
        out_specs=pl.BlockSpec((None, qb, w), lambda i, g, c: (i, c, g)),
        out_shape=jax.ShapeDtypeStruct((b, s, MIX_W), BF16),
        compiler_params=_cparams(("parallel", "parallel", "arbitrary")),
        name="band_attn_prompt",
    )(*([z3] * (1 + 2 * nwb)), bias_blk)


def _band_sample_kernel(*refs, hb, p_len):
    q_ref, kn_ref, vn_ref = refs[:3]
    kc_refs = refs[3:3 + hb]
    vc_refs = refs[3 + hb:3 + 2 * hb]
    bias_ref, o_ref = refs[3 + 2 * hb:]
    scale = HEAD_DIM ** -0.5
    scores = []
    for h in range(hb):
        sl = slice(h * HEAD_DIM, (h + 1) * HEAD_DIM)
        q = q_ref[:, sl].astype(BF16)
        scores.append((_nt_dot(q, kc_refs[h][...].astype(BF16)), _nt_dot(q, kn_ref[:, sl].astype(BF16))))
    for h in range(hb):
        sl = slice(h * HEAD_DIM, (h + 1) * HEAD_DIM)
        s_c = scores[h][0] * scale + bias_ref[h, :, :p_len]
        s_n = scores[h][1] * scale + bias_ref[h, :, p_len:]
        m = jnp.maximum(jnp.max(s_c, axis=-1, keepdims=True), jnp.max(s_n, axis=-1, keepdims=True))
        e_c = jnp.exp(s_c - m)
        e_n = jnp.exp(s_n - m)
        l = jnp.sum(e_c, axis=-1, keepdims=True) + jnp.sum(e_n, axis=-1, keepdims=True)
        o = jnp.dot((e_c / l).astype(BF16), vc_refs[h][...].astype(BF16), preferred_element_type=F32)
        o = o + jnp.dot((e_n / l).astype(BF16), vn_ref[:, sl].astype(BF16), preferred_element_type=F32)
        o_ref[:, sl] = o.astype(BF16)


def _band_attention_sample(z3, k_cache, v_cache, layer, bias, hb=N_MIX_HEADS):
    b, t, _ = z3.shape
    p_len = k_cache.shape[2]
    assert PAST_LEN % CHUNK == 0 and p_len == WINDOW and PAST_LEN >= p_len and t <= CHUNK
    ng = N_MIX_HEADS // hb
    w = hb * HEAD_DIM
    k_cache = jnp.transpose(k_cache, (0, 1, 3, 2, 4))
    v_cache = jnp.transpose(v_cache, (0, 1, 3, 2, 4))

    def cache_spec(h):
        return pl.BlockSpec((None, None, None, p_len, HEAD_DIM), lambda i, g, h=h: (layer, i, g * hb + h, 0, 0))

    cache_specs = [cache_spec(h) for h in range(hb)]
    return pl.pallas_call(
        functools.partial(_band_sample_kernel, hb=hb, p_len=p_len),
        grid=(b, ng),
        in_specs=[
            pl.BlockSpec((None, t, w), lambda i, g: (i, 0, g)),
            pl.BlockSpec((None, t, w), lambda i, g: (i, 0, ng + g)),
            pl.BlockSpec((None, t, w), lambda i, g: (i, 0, 2 * ng + g)),
            *cache_specs, *cache_specs,
            pl.BlockSpec((hb, t, p_len + t), lambda i, g: (g, 0, 0)),
        ],
        out_specs=pl.BlockSpec((None, t, w), lambda i, g: (i, 0, g)),
        out_shape=jax.ShapeDtypeStruct((b, t, MIX_W), BF16),
        compiler_params=_cparams(("parallel", "parallel")),
        name="band_attn_sample",
    )(z3, z3, z3, *([k_cache] * hb), *([v_cache] * hb), bias)


def _mem_attn_kernel(q_ref, k_ref, v_ref, o_ref, *, head_major):
    scores = []
    for h in range(N_MEM_HEADS):
        sl = slice(h * HEAD_DIM, (h + 1) * HEAD_DIM)
        k = k_ref[h] if head_major else k_ref[:, sl]
        scores.append(_nt_dot(q_ref[:, sl].astype(BF16), k.astype(BF16)))
    for h in range(N_MEM_HEADS):
        sl = slice(h * HEAD_DIM, (h + 1) * HEAD_DIM)
        v = v_ref[h] if head_major else v_ref[:, sl]
        s = scores[h] * (HEAD_DIM ** -0.5)
        p = _softmax_rows(s).astype(BF16)
        o_ref[:, sl] = jnp.dot(p, v.astype(BF16), preferred_element_type=F32).astype(BF16)


def _mem_attention(z3, q_blk, karr, kspec, varr, vspec, head_major):
    b, t, _ = z3.shape
    tq = min(t, TQ_MEM)
    return pl.pallas_call(
        functools.partial(_mem_attn_kernel, head_major=head_major),
        grid=(b, t // tq),
        in_specs=[
            pl.BlockSpec((None, tq, MEM_W), lambda i, j: (i, j, q_blk)),
            kspec, vspec,
        ],
        out_specs=pl.BlockSpec((None, tq, MEM_W), lambda i, j: (i, j, 0)),
        out_shape=jax.ShapeDtypeStruct((b, t, MEM_W), BF16),
        compiler_params=_cparams(("parallel", "arbitrary")),
        name="mem_attn",
    )(z3, karr, varr)


def _hgrn_kernel(*refs, hb, t, layer, has_state):
    if has_state:
        q_ref, f_ref, i_ref, g_ref, lbp_ref, gh_ref, s0_ref, o_ref, sout_ref, st_ref = refs
    else:
        q_ref, f_ref, i_ref, g_ref, lbp_ref, gh_ref, o_ref, sout_ref, st_ref = refs
    c = pl.program_id(2)
    nsub = t // SUB

    @pl.when(c == 0)
    def _():
        for h in range(hb):
            if has_state:
                st_ref[h] = s0_ref[h].T
            else:
                st_ref[h] = jnp.zeros((HEAD_DIM, HEAD_DIM), F32)

    row = lax.broadcasted_iota(jnp.int32, (t, t), 0)
    col = lax.broadcasted_iota(jnp.int32, (t, t), 1)
    rowd = lax.broadcasted_iota(jnp.int32, (t, HEAD_DIM), 0)
    hits = [jnp.logical_and(row - col == dl, jnp.bitwise_and(row, SUB - 1) >= dl) for dl in range(SUB)]
    sub_shift = SUB.bit_length() - 1
    far = jnp.right_shift(col, sub_shift) < jnp.right_shift(row, sub_shift)

    pending = []
    for h in range(hb):
        sl = slice(h * HEAD_DIM, (h + 1) * HEAD_DIM)
        q = q_ref[:, sl]
        ff = f_ref[:, sl]
        v = i_ref[:, sl]

        lbp = lbp_ref[:, sl]
        e = jnp.exp(lbp - jnp.max(lbp, axis=0, keepdims=True))
        soft = e / jnp.sum(e, axis=0, keepdims=True)
        acc = soft[0:1]
        for r in range(1, layer + 1):
            acc = acc + soft[r:r + 1]
        lb = acc - soft[0:1]

        en = jnp.exp(-jnp.abs(ff))
        rec = 1.0 / (1.0 + en)
        sig_pos = jnp.where(ff >= 0.0, rec, en * rec)
        sig_neg = jnp.where(ff >= 0.0, en * rec, rec)
        f = lb + (1.0 - lb) * sig_pos
        log_f = jnp.where(f > 0.0, jnp.log(f), ff)
        k = (1.0 - lb) * sig_neg

        b = log_f
        sh = 1
        while sh < t:
            b = b + jnp.where(rowd >= sh, pltpu.roll(b, sh, 0), 0.0)
            sh *= 2
        b_end = b[t - 1:t]

        st = st_ref[h]
        o = _nt_dot((q * jnp.exp(b)).astype(BF16), st.astype(BF16))

        sc = jnp.zeros((t, t), F32)
        for dl in range(SUB):
            if dl == 0:
                prod = q * k
            else:
                prod = q * pltpu.roll(k, dl, 0) * jnp.exp(jnp.minimum(b - pltpu.roll(b, dl, 0), 0.0))
            r = jnp.sum(prod, axis=1, keepdims=True)
            sc = jnp.where(hits[dl], r, sc)

        q_parts = []
        k_rows = []
        for i in range(nsub - 1):
            lo = (i + 1) * SUB
            b_ref = b[lo - 1:lo]
            later = q[lo:] * jnp.exp(jnp.minimum(b[lo:] - b_ref, 0.0))
            q_parts.append(jnp.concatenate([jnp.zeros((lo, HEAD_DIM), F32), later], axis=0).astype(BF16))
            kh = k[lo - SUB:lo] * jnp.exp(jnp.minimum(b_ref - b[lo - SUB:lo], 0.0))
            zero = jnp.zeros_like(kh)
            k_rows.append(jnp.concatenate([kh if j == i else zero for j in range(nsub - 1)], axis=1))
        k_rows.append(jnp.zeros((SUB, (nsub - 1) * HEAD_DIM), F32))
        q_cat = jnp.concatenate(q_parts, axis=1)
        k_bd = jnp.concatenate(k_rows, axis=0).astype(BF16)
        sc_far = _nt_dot(q_cat, k_bd)
        scores = jnp.where(far, sc_far, sc)

        k_dec = (k * jnp.exp(jnp.minimum(b_end - b, 0.0))).astype(BF16)
        st_ref[h] = st * jnp.exp(b_end) + _tn_dot(v.astype(BF16), k_dec)
        pending.append((o, scores.astype(BF16)))

    for h in range(hb):
        sl = slice(h * HEAD_DIM, (h + 1) * HEAD_DIM)
        o, scores = pending[h]
        g = g_ref[:, sl]
        o = o + jnp.dot(scores, i_ref[:, sl].astype(BF16), preferred_element_type=F32)
        o = _rms(o, gh_ref[:, sl])
        o = o * (g * jax.nn.sigmoid(g))
        o_ref[:, sl] = o.astype(BF16)

    @pl.when(c == pl.num_programs(2) - 1)
    def _():
        for h in range(hb):
            sout_ref[h] = st_ref[h].T


def _hgrn(z3, lb_param, g_head, state, state_idx, layer, hb=HB_HGRN):
    assert SUB & (SUB - 1) == 0
    b, s, _ = z3.shape
    t = min(CHUNK, s)
    nc = s // t
    ng = N_MIX_HEADS // hb
    w = hb * HEAD_DIM
    depth = lb_param.shape[0]
    has_state = state is not None

    def zspec(seg):
        return pl.BlockSpec((None, t, w), lambda i, g, c, seg=seg: (i, c, seg * ng + g))

    in_specs = [zspec(0), zspec(1), zspec(2), zspec(3),
                pl.BlockSpec((depth, w), lambda i, g, c: (0, g)),
                pl.BlockSpec((1, w), lambda i, g, c: (0, g))]
    args = [z3, z3, z3, z3, lb_param, g_head.reshape(1, MIX_W)]
    if has_state:
        in_specs.append(pl.BlockSpec((None, None, hb, HEAD_DIM, HEAD_DIM),
                                     lambda i, g, c: (state_idx, i, g, 0, 0)))
        args.append(state)
    return pl.pallas_call(
        functools.partial(_hgrn_kernel, hb=hb, t=t, layer=layer, has_state=has_state),
        grid=(b, ng, nc),
        in_specs=in_specs,
        out_specs=[
            pl.BlockSpec((None, t, w), lambda i, g, c: (i, c, g)),
            pl.BlockSpec((None, hb, HEAD_DIM, HEAD_DIM), lambda i, g, c: (i, g, 0, 0)),
        ],
        out_shape=[
            jax.ShapeDtypeStruct((b, s, MIX_W), BF16),
            jax.ShapeDtypeStruct((b, N_MIX_HEADS, HEAD_DIM, HEAD_DIM), F32),
        ],
        scratch_shapes=[pltpu.VMEM((hb, HEAD_DIM, HEAD_DIM), F32)],
        compiler_params=_cparams(("parallel", "parallel", "arbitrary")),
        name="hgrn2",
    )(*args)


def _out_proj_kernel(om_ref, oe_ref, wm_ref, we_ref, x_ref, g_ref, x1_ref, xnt_ref):
    x1 = x_ref[...] + jnp.dot(om_ref[...], wm_ref[...], preferred_element_type=F32)
    x1 = x1 + jnp.dot(oe_ref[...], we_ref[...], preferred_element_type=F32)
    x1_ref[...] = x1
    xnt_ref[...] = _rms(x1, g_ref[...]).T.astype(BF16)


def _out_proj(o_mix, o_mem, w_out, layer, x, g2):
    n, d = x.shape
    tm = min(TM_OUT, n)
    once = pl.Buffered(1)
    return pl.pallas_call(
        _out_proj_kernel,
        grid=(n // tm,),
        in_specs=[
            pl.BlockSpec((tm, MIX_W), lambda i: (i, 0)),
            pl.BlockSpec((tm, MEM_W), lambda i: (i, 0)),
            pl.BlockSpec((None, MIX_W, d), lambda i: (layer, 0, 0), pipeline_mode=once),
            pl.BlockSpec((None, MEM_W, d), lambda i: (layer, MIX_W // MEM_W, 0), pipeline_mode=once),
            pl.BlockSpec((tm, d), lambda i: (i, 0)),
            pl.BlockSpec((1, d), lambda i: (0, 0)),
        ],
        out_specs=[
            pl.BlockSpec((tm, d), lambda i: (i, 0)),
            pl.BlockSpec((d, tm), lambda i: (0, i)),
        ],
        out_shape=[jax.ShapeDtypeStruct((n, d), F32), jax.ShapeDtypeStruct((d, n), BF16)],
        compiler_params=_cparams(("parallel",)),
        name="out_proj",
    )(o_mix, o_mem, w_out, w_out, x, g2.reshape(1, d))


_REMOVED = 2.0 ** 100


def _extract_exact(vals, order, rounds):
    rank = jnp.full(vals.shape, float(PEER_TOPK), F32)
    tops = []
    for r in range(rounds):
        m = jnp.max(vals, axis=0, keepdims=True)
        pick = jnp.min(jnp.where(vals == m, order, 1e9), axis=0, keepdims=True)
        sel = order == pick
        rank = jnp.where(sel, float(r), rank)
        vals = jnp.where(sel, -jnp.inf, vals)
        tops.append(m)
    return jnp.concatenate(tops, axis=0), rank


def _extract_fast(vals, rounds):
    tops = []
    for r in range(rounds):
        m = jnp.max(vals, axis=0, keepdims=True)
        vals = jnp.where(vals == m, -_REMOVED * (r + 1), vals)
        tops.append(m)
    rank = jnp.where(vals <= -_REMOVED, vals * (-1.0 / _REMOVED) - 1.0, float(PEER_TOPK))
    return jnp.concatenate(tops, axis=0), rank


def _route_lanes(s1, s2, exact):
    lanes = s1.shape[1]
    if exact:
        key_id = lax.broadcasted_iota(jnp.int32, (N_KEYS, lanes), 0).astype(F32)
        top1, rank1 = _extract_exact(s1, key_id, PEER_TOPK)
        top2, rank2 = _extract_exact(s2, key_id, PEER_TOPK)
    else:
        top1, rank1 = _extract_fast(s1, PEER_TOPK)
        top2, rank2 = _extract_fast(s2, PEER_TOPK)

    blk_a = top1[0:1] + top2
    blk_b = [top1[a:a + 1] + top2[0:8] for a in range(1, 8)]
    blk_c = top1[8:16] + top2[0:1]
    cand = jnp.concatenate([blk_a] + blk_b + [blk_c], axis=0)
    nrow = cand.shape[0]
    r = lax.broadcasted_iota(jnp.int32, (nrow, lanes), 0)
    a_mid = 1 + jnp.right_shift(r - 16, 3)
    b_mid = jnp.bitwise_and(r - 16, 7)
    valid = jnp.logical_or(jnp.logical_or(r < 16, r >= 72), (a_mid + 1) * (b_mid + 1) <= PEER_TOPK)
    cand = jnp.where(valid, cand, -jnp.inf)
    if exact:
        cand_id = jnp.where(r < 16, r, jnp.where(r < 72, a_mid * PEER_TOPK + b_mid, (r - 64) * PEER_TOPK))
        best, crank = _extract_exact(cand, cand_id.astype(F32), PEER_TOPK)
    else:
        best, crank = _extract_fast(cand, PEER_TOPK)
    chosen = jnp.logical_and(valid, crank < PEER_TOPK).astype(F32)
    z = jnp.sum(jnp.exp(best - best[0:1]), axis=0, keepdims=True)

    counts = [jnp.sum(chosen[0:16], axis=0, keepdims=True)]
    counts += [jnp.sum(chosen[16 + 8 * (a - 1):16 + 8 * a], axis=0, keepdims=True) for a in range(1, 8)]
    counts += [chosen[72 + a - 8:72 + a - 7] for a in range(8, 16)]
    c1 = jnp.zeros((N_KEYS, lanes), F32)
    for a in range(PEER_TOPK):
        c1 = jnp.where(rank1 == a, counts[a], c1)

    ranked = (jnp.sum((rank1 < PEER_TOPK).astype(F32), axis=0, keepdims=True)
              + jnp.sum((rank2 < PEER_TOPK).astype(F32), axis=0, keepdims=True)
              + jnp.sum(chosen, axis=0, keepdims=True))
    low = jnp.minimum(jnp.min(s1, axis=0, keepdims=True), jnp.min(s2, axis=0, keepdims=True))
    redo = jnp.logical_or(ranked != 3.0 * PEER_TOPK, jnp.logical_not(low > -0.25 * _REMOVED)).astype(F32)
    e1 = jnp.exp(s1 - top1[0:1])
    e2 = jnp.exp(s2 - top2[0:1]) / z
    return rank2, e2, c1, e1, redo


def _scores_kernel(xnt_ref, wq_ref, keys_ref, s_ref):
    qq = jnp.dot(wq_ref[...], xnt_ref[...], preferred_element_type=F32).astype(BF16)
    for h in range(PEER_HEADS):
        for c in range(2):
            row = (2 * h + c) * N_KEYS
            s_ref[2 * h + c] = jnp.dot(keys_ref[c, h], qq[row:row + N_KEYS], preferred_element_type=F32)


def _peer_scores(xnt, w_pq_t, keys, layer):
    d, n = xnt.shape
    tb = min(TB_SCORES, n)
    return pl.pallas_call(
        _scores_kernel,
        grid=(n // tb,),
        in_specs=[
            pl.BlockSpec((d, tb), lambda i: (0, i)),
            pl.BlockSpec((None,) + w_pq_t.shape[1:], lambda i: (layer, 0, 0)),
            pl.BlockSpec((None,) + keys.shape[1:], lambda i: (layer, 0, 0, 0, 0)),
        ],
        out_specs=pl.BlockSpec((2 * PEER_HEADS, N_KEYS, tb), lambda i: (0, 0, i)),
        out_shape=jax.ShapeDtypeStruct((2 * PEER_HEADS, N_KEYS, n), F32),
        compiler_params=_cparams(("parallel",)),
        name="peer_scores",
    )(xnt, w_pq_t, keys)


def _route_kernel(s_ref, r2_ref, e2_ref, c1_ref, e1_ref, *, hr):
    def emit(h, exact):
        rank2, e2, c1, e1, redo = _route_lanes(s_ref[2 * h], s_ref[2 * h + 1], exact)
        r2_ref[h] = rank2.astype(BF16)
        e2_ref[h] = e2.astype(BF16)
        c1_ref[h] = c1
        e1_ref[h] = e1
        return redo

    redo = [emit(h, exact=False) for h in range(hr)]
    for h in range(hr):
        @pl.when(jnp.max(redo[h]) > 0.0)
        def _():
            emit(h, exact=True)


def _peer_route(scores, hr=HR_ROUTE):
    n = scores.shape[2]
    tb = min(TB_ROUTE, n)
    out = jax.ShapeDtypeStruct((PEER_HEADS, N_KEYS, n), F32)
    out16 = jax.ShapeDtypeStruct((PEER_HEADS, N_KEYS, n), BF16)
    ospec = pl.BlockSpec((hr, N_KEYS, tb), lambda i, h: (h, 0, i))
    return pl.pallas_call(
        functools.partial(_route_kernel, hr=hr),
        grid=(n // tb, PEER_HEADS // hr),
        in_specs=[pl.BlockSpec((2 * hr, N_KEYS, tb), lambda i, h: (h, 0, i))],
        out_specs=[ospec, ospec, ospec, ospec],
        out_shape=[out16, out16, out, out],
        compiler_params=_cparams(("parallel", "parallel")),
        name="peer_route",
    )(scores)


def _gelu(x):
    return 0.5 * x * (1.0 + lax.erf(x * (0.5 ** 0.5)))


def _peer_kernel(xnt_ref, u_ref, un_ref, vt_ref, r2_ref, e2_ref, c1_ref, e1_ref, y_ref, a_ref, coef_ref, *,
                 eb, ck, lc):
    e = pl.program_id(1)
    tb = xnt_ref.shape[1]
    per = ck // N_KEYS
    nck = eb // ck

    @pl.when(e == 0)
    def _():
        y_ref[...] = jnp.zeros_like(y_ref)
        a_ref[0] = jnp.dot(u_ref[0:ck, :], xnt_ref[...], preferred_element_type=F32)

    for k in range(nck):
        if k + 1 < nck:
            a_next = jnp.dot(u_ref[(k + 1) * ck:(k + 2) * ck, :], xnt_ref[...], preferred_element_type=F32)
            a_ref[k + 1] = a_next
        else:
            a_next = jnp.dot(un_ref[...], xnt_ref[...], preferred_element_type=F32)
        bits = pltpu.bitcast(a_next[0:1, tb - lc:], jnp.uint32)
        bits = lax.shift_right_logical(lax.shift_right_logical(bits, jnp.uint32(16)), jnp.uint32(16))
        zero = pltpu.bitcast(bits, F32)
        for ii in range(per):
            i = k * per + ii
            rows = slice(ii * N_KEYS, (ii + 1) * N_KEYS)
            for l0 in range(0, tb, lc):
                cols = slice(l0, l0 + lc)
                gate = jnp.zeros((N_KEYS, lc), BF16)
                for h in range(PEER_HEADS):
                    c1 = jnp.broadcast_to(c1_ref[h, i:i + 1, cols], (N_KEYS, lc)).astype(BF16)
                    e1 = jnp.broadcast_to(e1_ref[h, i:i + 1, cols], (N_KEYS, lc)).astype(BF16)
                    zero16 = jnp.zeros((N_KEYS, lc), BF16)
                    gate = gate + jnp.where(r2_ref[h, :, cols] < c1, e2_ref[h, :, cols], zero16) * e1
                act = gate.astype(F32) * _gelu(a_ref[k, rows, cols])
                if ii == per - 1 and l0 + lc == tb:
                    act = act + zero
                coef_ref[k, rows, cols] = act.astype(BF16)
        if k + 1 == nck:
            a_ref[0] = a_next
        y_ref[...] += jnp.dot(vt_ref[:, k * ck:(k + 1) * ck], coef_ref[k], preferred_element_type=F32)


def _peer_sweep(xnt, u, vt, layer, r2, e2, c1, e1):
    d, n = xnt.shape
    tb = min(TB_PEER, n)
    eb = EB_PEER
    ck = CK_PEER
    lc = min(LC_PEER, tb)
    rspec = pl.BlockSpec((PEER_HEADS, N_KEYS, tb), lambda t, e: (0, 0, t))
    ispec = pl.BlockSpec((PEER_HEADS, eb // N_KEYS, tb), lambda t, e: (0, e, t))
    last_ck = N_EXPERTS // ck - 1
    return pl.pallas_call(
        functools.partial(_peer_kernel, eb=eb, ck=ck, lc=lc),
        grid=(n // tb, N_EXPERTS // eb),
        in_specs=[
            pl.BlockSpec((d, tb), lambda t, e: (0, t)),
            pl.BlockSpec((None, eb, d), lambda t, e: (layer, e, 0)),
            pl.BlockSpec((None, ck, d), lambda t, e: (layer, jnp.minimum((e + 1) * (eb // ck), last_ck), 0)),
            pl.BlockSpec((None, d, eb), lambda t, e: (layer, 0, e)),
            rspec, rspec, ispec, ispec,
        ],
        out_specs=pl.BlockSpec((d, tb), lambda t, e: (0, t)),
        out_shape=jax.ShapeDtypeStruct((d, n), F32),
        scratch_shapes=[pltpu.VMEM((eb // ck, ck, tb), F32), pltpu.VMEM((eb // ck, ck, tb), BF16)],
        compiler_params=_cparams(("parallel", "arbitrary")),
        name="peer_sweep",
    )(xnt, u, u, vt, r2, e2, c1, e1)


def _add_t_kernel(x_ref, yt_ref, g_ref, o_ref, *, final):
    x = x_ref[...] + yt_ref[...].T
    if final:
        x = _rms(x, g_ref[...])
    o_ref[...] = x


def _add_transposed(x1, yt, g, final):
    n, d = x1.shape
    tm = min(TM_RES, n)
    return pl.pallas_call(
        functools.partial(_add_t_kernel, final=final),
        grid=(n // tm,),
        in_specs=[
            pl.BlockSpec((tm, d), lambda i: (i, 0)),
            pl.BlockSpec((d, tm), lambda i: (0, i)),
            pl.BlockSpec((1, d), lambda i: (0, 0)),
        ],
        out_specs=pl.BlockSpec((tm, d), lambda i: (i, 0)),
        out_shape=jax.ShapeDtypeStruct((n, d), F32),
        compiler_params=_cparams(("parallel",)),
        name="peer_residual",
    )(x1, yt, g.reshape(1, d))


def _trunk(x, mem, k_cache, v_cache, s_cache, mk_cache, mv_cache, p):
    prompt = mem is not None
    bsz, t, d = x.shape
    n = bsz * t
    depth = p["g_norm1"].shape[0]
    x2 = x.reshape(n, d)
    new_k, new_v, new_s, new_mk, new_mv = [], [], [], [], []
    for i in range(depth):
        j = i // 2
        attn = i % 2 == 0
        w_in = p["w_in_a"] if attn else p["w_in_b"]
        w_out = p["w_out_a"] if attn else p["w_out_b"]
        z = _norm_matmul(x2, p["g_norm1"][i], w_in, j)
        z3 = z.reshape(bsz, t, z.shape[1])
        mq_blk = (z.shape[1] - MEM_W) // MEM_W

        if prompt:
            mkv = _norm_matmul(mem.reshape(bsz * N_MEM, d), p["g_mem"][i], p["w_mem_kv"], i)
            new_mk.append(mkv[:, :MEM_W].reshape(bsz, N_MEM, N_MEM_HEADS, HEAD_DIM))
            new_mv.append(mkv[:, MEM_W:].reshape(bsz, N_MEM, N_MEM_HEADS, HEAD_DIM))
            mkv3 = mkv.reshape(bsz, N_MEM, 2 * MEM_W)
            kspec = pl.BlockSpec((None, N_MEM, MEM_W), lambda b, q: (b, 0, 0))
            vspec = pl.BlockSpec((None, N_MEM, MEM_W), lambda b, q: (b, 0, 1))
            o_mem = _mem_attention(z3, mq_blk, mkv3, kspec, mkv3, vspec, head_major=False)
        else:
            cspec = pl.BlockSpec((None, None, N_MEM_HEADS, N_MEM, HEAD_DIM), lambda b, q, i=i: (i, b, 0, 0, 0))
            o_mem = _mem_attention(z3, mq_blk, jnp.transpose(mk_cache, (0, 1, 3, 2, 4)), cspec,
                                   jnp.transpose(mv_cache, (0, 1, 3, 2, 4)), cspec, head_major=True)

        if attn:
            k_new = z3[:, :, MIX_W:2 * MIX_W]
            v_new = z3[:, :, 2 * MIX_W:3 * MIX_W]
            if prompt:
                bias = _band_bias(p["rel_bias"][j], CHUNK, WINDOW + CHUNK, WINDOW)
                o_mix = _band_attention_prompt(z3, bias)
                keep = min(WINDOW, t)
                k_new, v_new = k_new[:, t - keep:], v_new[:, t - keep:]
            else:
                p_len = k_cache.shape[2]
                bias = _band_bias(p["rel_bias"][j], t, p_len + t, p_len)
                o_mix = _band_attention_sample(z3, k_cache, v_cache, j, bias)
            new_k.append(k_new.reshape(bsz, -1, N_MIX_HEADS, HEAD_DIM))
            new_v.append(v_new.reshape(bsz, -1, N_MIX_HEADS, HEAD_DIM))
        else:
            o_mix, s_new = _hgrn(z3, p["lb_param"], p["g_hgrn"][j], None if prompt else s_cache, j, i)
            new_s.append(s_new)

        x1, xnt = _out_proj(o_mix.reshape(n, MIX_W), o_mem.reshape(n, MEM_W), w_out, j, x2, p["g_norm2"][i])
        r2, e2, c1, e1 = _peer_route(_peer_scores(xnt, p["w_pq_t"], p["peer_keys"], i))
        yt = _peer_sweep(xnt, p["peer_u"], p["peer_v_t"], i, r2, e2, c1, e1)
        x2 = _add_transposed(x1, yt, p["g_final"], final=(i == depth - 1))

    y = x2.reshape(bsz, t, d)
    mk_out = jnp.stack(new_mk) if new_mk else None
    mv_out = jnp.stack(new_mv) if new_mv else None
    return y, jnp.stack(new_k), jnp.stack(new_v), jnp.stack(new_s), mk_out, mv_out


def kernel(x_prompt, x_sample, cache_attn_k, cache_attn_v, state_hgrn, cache_mem_k, cache_mem_v, mem_prompt,
           g_norm1, g_norm2, g_mem, g_final, w_in_a, w_out_a, rel_bias, w_in_b, w_out_b, lb_param, g_hgrn,
           w_mem_kv, w_pq, peer_keys, peer_u, peer_v):
    p = dict(
        g_norm1=g_norm1, g_norm2=g_norm2, g_mem=g_mem, g_final=g_final, rel_bias=rel_bias,
        lb_param=lb_param, g_hgrn=g_hgrn,
        w_in_a=w_in_a.astype(BF16), w_out_a=w_out_a.astype(BF16),
        w_in_b=w_in_b.astype(BF16), w_out_b=w_out_b.astype(BF16),
        w_mem_kv=w_mem_kv.astype(BF16),
        w_pq_t=jnp.swapaxes(w_pq, 1, 2).astype(BF16),
        peer_keys=peer_keys.astype(BF16),
        peer_u=peer_u.astype(BF16),
        peer_v_t=jnp.swapaxes(peer_v, 1, 2).astype(BF16),
    )
    y_p, k_p, v_p, s_p, mk_p, mv_p = _trunk(x_prompt, mem_prompt, None, None, None, None, None, p)
    y_s, k_s, v_s, s_s, _, _ = _trunk(x_sample, None, cache_attn_k, cache_attn_v, state_hgrn,
                                      cache_mem_k, cache_mem_v, p)
    return (y_p, y_s, k_p, v_p, k_s, v_s, s_p, s_s, mk_p, mv_p)
```

```python
import functools

import jax
import jax.numpy as jnp
from jax import lax
from jax.experimental import pallas as pl
from jax.experimental.pallas import tpu as pltpu

F32 = jnp.float32
BF16 = jnp.bfloat16

PAST_LEN = 1024
CHUNK = 64
LEFT_CHUNKS = 8
WINDOW = LEFT_CHUNKS * CHUNK
HEAD_DIM = 128
N_MIX_HEADS = 12
N_MEM_HEADS = 4
MIX_W = N_MIX_HEADS * HEAD_DIM
MEM_W = N_MEM_HEADS * HEAD_DIM
N_MEM = 256
REL_CLIP = 256
PEER_HEADS = 8
N_KEYS = 128
N_EXPERTS = N_KEYS * N_KEYS
PEER_TOPK = 16
EPS = 1e-6
NEG_INF = -1e30

LANES = 128
SUBLANES = 8
SUB = SUBLANES
VMEM_LIMIT = 56 * 1024 * 1024

TM_PROJ = 1024
TN_PROJ = 1664
TM_OUT = 512
TQ_MEM = 512
HG_BIAS = 4
TB_SCORES = 512
TB_ROUTE = 256
HR_ROUTE = 2
TB_PEER = 512
EB_PEER = 1024
CK_PEER = 512
LC_PEER = 256
QB_ATTN = 256
HB_HGRN = 12


def _cparams(sem):
    return pltpu.CompilerParams(dimension_semantics=sem, vmem_limit_bytes=VMEM_LIMIT)


def _rms(x, g):
    return x * lax.rsqrt(jnp.mean(x * x, axis=-1, keepdims=True) + EPS) * g


def _norm_matmul_kernel(x_ref, g_ref, w_ref, o_ref, hn_ref):
    @pl.when(pl.program_id(1) == 0)
    def _():
        hn_ref[...] = _rms(x_ref[...], g_ref[...]).astype(BF16)

    o_ref[...] = jnp.dot(hn_ref[...], w_ref[...], preferred_element_type=F32)


def _norm_matmul(x, g, w, layer):
    m, d = x.shape
    n = w.shape[2]
    tm = min(TM_PROJ, m)
    tn = n // -(-n // TN_PROJ)
    assert n % tn == 0 and tn % LANES == 0
    return pl.pallas_call(
        _norm_matmul_kernel,
        grid=(m // tm, n // tn),
        in_specs=[
            pl.BlockSpec((tm, d), lambda i, j: (i, 0)),
            pl.BlockSpec((1, d), lambda i, j: (0, 0)),
            pl.BlockSpec((None, d, tn), lambda i, j: (layer, 0, j)),
        ],
        out_specs=pl.BlockSpec((tm, tn), lambda i, j: (i, j)),
        out_shape=jax.ShapeDtypeStruct((m, n), F32),
        scratch_shapes=[pltpu.VMEM((tm, d), BF16)],
        compiler_params=_cparams(("parallel", "arbitrary")),
        name="norm_matmul",
    )(x, g.reshape(1, d), w)


def _bias_kernel(tab_ref, o_ref, *, tq, tk, off, hg):
    h0 = pl.program_id(0) * hg
    clip = lambda d: max(min(d, REL_CLIP), -REL_CLIP) + REL_CLIP
    for c0 in range(0, tk, LANES):
        w = min(LANES, tk - c0)
        qi = lax.broadcasted_iota(jnp.int32, (tq, w), 0)
        ki = lax.broadcasted_iota(jnp.int32, (tq, w), 1) + c0
        rel = jnp.clip(qi - ki + off, -REL_CLIP, REL_CLIP) + REL_CLIP
        lo = clip(-(c0 + w - 1) + off)
        hi = clip((tq - 1) - c0 + off)

        def body(r, accs, rel=rel):
            hit = rel == r
            return tuple(jnp.where(hit, tab_ref[h0 + h, r], acc) for h, acc in enumerate(accs))

        accs = lax.fori_loop(lo, hi + 1, body, tuple(jnp.zeros((tq, w), F32) for _ in range(hg)))
        for h in range(hg):
            o_ref[h, :, c0:c0 + w] = accs[h]


def _band_bias(table, tq, tk, off, hg=HG_BIAS):
    nh = table.shape[1]
    return pl.pallas_call(
        functools.partial(_bias_kernel, tq=tq, tk=tk, off=off, hg=hg),
        grid=(nh // hg,),
        in_specs=[pl.BlockSpec(memory_space=pltpu.SMEM)],
        out_specs=pl.BlockSpec((hg, tq, tk), lambda g: (g, 0, 0)),
        out_shape=jax.ShapeDtypeStruct((nh, tq, tk), F32),
        compiler_params=_cparams(("arbitrary",)),
        name="band_bias",
    )(table.T)


def _softmax_rows(s):
    m = jnp.max(s, axis=-1, keepdims=True)
    p = jnp.exp(s - m)
    return p / jnp.sum(p, axis=-1, keepdims=True)


def _nt_dot(a, b):
    return lax.dot_general(a, b, (((1,), (1,)), ((), ())), preferred_element_type=F32)


def _tn_dot(a, b):
    return lax.dot_general(a, b, (((0,), (0,)), ((), ())), preferred_element_type=F32)


def _band_prompt_kernel(*refs, hb, qb, nwb):
    q_ref = refs[0]
    k_refs = refs[1:1 + nwb]
    v_refs = refs[1 + nwb:1 + 2 * nwb]
    bias_ref, o_ref = refs[1 + 2 * nwb:]
    c = pl.program_id(2)
    tk = WINDOW + qb
    k_pos = lax.broadcasted_iota(jnp.int32, (qb, tk), 1) + (c * qb - WINDOW)
    live = k_pos >= 0
    scale = HEAD_DIM ** -0.5
    scores = []
    for h in range(hb):
        sl = slice(h * HEAD_DIM, (h + 1) * HEAD_DIM)
        kw = jnp.concatenate([r[:, sl].astype(BF16) for r in k_refs], axis=0)
        scores.append(_nt_dot(q_ref[:, sl].astype(BF16), kw))
    for h in range(hb):
        sl = slice(h * HEAD_DIM, (h + 1) * HEAD_DIM)
        vw = jnp.concatenate([r[:, sl].astype(BF16) for r in v_refs], axis=0)
        s = scores[h] * scale + bias_ref[h]
        s = jnp.where(live, s, NEG_INF)
        p = _softmax_rows(s).astype(BF16)
        o_ref[:, sl] = jnp.dot(p, vw, preferred_element_type=F32).astype(BF16)


def _band_attention_prompt(z3, bias, hb=N_MIX_HEADS):
    b, s, _ = z3.shape
    qb = min(QB_ATTN, s)
    assert WINDOW % qb == 0
    nqc = qb // CHUNK
    nwb = WINDOW // qb + 1
    ng = N_MIX_HEADS // hb
    w = hb * HEAD_DIM
    tk = WINDOW + qb

    def win_spec(j, seg):
        return pl.BlockSpec((None, qb, w), lambda i, g, c, j=j, seg=seg:
                            (i, jnp.maximum(c + j - (nwb - 1), 0), seg * ng + g))
    bias_blk = jnp.concatenate(
        [jnp.pad(bias, ((0, 0), (0, 0), (j * CHUNK, (nqc - 1 - j) * CHUNK)), constant_values=NEG_INF)
         for j in range(nqc)], axis=1)
    return pl.pallas_call(
        functools.partial(_band_prompt_kernel, hb=hb, qb=qb, nwb=nwb),
        grid=(b, ng, s // qb),
        in_specs=[
            win_spec(nwb - 1, 0),
            *[win_spec(j, 1) for j in range(nwb)],
            *[win_spec(j, 2) for j in range(nwb)],
            pl.BlockSpec((hb, qb, tk), lambda i, g, c: (g, 0, 0), pipeline_mode=pl.Buffered(1)),
        ],
        out_specs=pl.BlockSpec((None, qb, w), lambda i, g, c: (i, c, g)),
        out_shape=jax.ShapeDtypeStruct((b, s, MIX_W), BF16),
        compiler_params=_cparams(("parallel", "parallel", "arbitrary")),
        name="band_attn_prompt",
    )(*([z3] * (1 + 2 * nwb)), bias_blk)


def _band_sample_kernel(*refs, hb, p_len):
    q_ref, kn_ref, vn_ref = refs[:3]
    kc_refs = refs[3:3 + hb]
    vc_refs = refs[3 + hb:3 + 2 * hb]
    bias_ref, o_ref = refs[3 + 2 * hb:]
    scale = HEAD_DIM ** -0.5
    scores = []
    for h in range(hb):
        sl = slice(h * HEAD_DIM, (h + 1) * HEAD_DIM)
        q = q_ref[:, sl].astype(BF16)
        scores.append((_nt_dot(q, kc_refs[h][...].astype(BF16)), _nt_dot(q, kn_ref[:, sl].astype(BF16))))
    for h in range(hb):
        sl = slice(h * HEAD_DIM, (h + 1) * HEAD_DIM)
        s_c = scores[h][0] * scale + bias_ref[h, :, :p_len]
        s_n = scores[h][1] * scale + bias_ref[h, :, p_len:]
        m = jnp.maximum(jnp.max(s_c, axis=-1, keepdims=True), jnp.max(s_n, axis=-1, keepdims=True))
        e_c = jnp.exp(s_c - m)
        e_n = jnp.exp(s_n - m)
        l = jnp.sum(e_c, axis=-1, keepdims=True) + jnp.sum(e_n, axis=-1, keepdims=True)
        o = jnp.dot((e_c / l).astype(BF16), vc_refs[h][...].astype(BF16), preferred_element_type=F32)
        o = o + jnp.dot((e_n / l).astype(BF16), vn_ref[:, sl].astype(BF16), preferred_element_type=F32)
        o_ref[:, sl] = o.astype(BF16)


def _band_attention_sample(z3, k_cache, v_cache, layer, bias, hb=N_MIX_HEADS):
    b, t, _ = z3.shape
    p_len = k_cache.shape[2]
    assert PAST_LEN % CHUNK == 0 and p_len == WINDOW and PAST_LEN >= p_len and t <= CHUNK
    ng = N_MIX_HEADS // hb
    w = hb * HEAD_DIM
    k_cache = jnp.transpose(k_cache, (0, 1, 3, 2, 4))
    v_cache = jnp.transpose(v_cache, (0, 1, 3, 2, 4))

    def cache_spec(h):
        return pl.BlockSpec((None, None, None, p_len, HEAD_DIM), lambda i, g, h=h: (layer, i, g * hb + h, 0, 0))

    cache_specs = [cache_spec(h) for h in range(hb)]
    return pl.pallas_call(
        functools.partial(_band_sample_kernel, hb=hb, p_len=p_len),
        grid=(b, ng),
        in_specs=[
            pl.BlockSpec((None, t, w), lambda i, g: (i, 0, g)),
            pl.BlockSpec((None, t, w), lambda i, g: (i, 0, ng + g)),
            pl.BlockSpec((None, t, w), lambda i, g: (i, 0, 2 * ng + g)),
            *cache_specs, *cache_specs,
            pl.BlockSpec((hb, t, p_len + t), lambda i, g: (g, 0, 0)),
        ],
        out_specs=pl.BlockSpec((None, t, w), lambda i, g: (i, 0, g)),
        out_shape=jax.ShapeDtypeStruct((b, t, MIX_W), BF16),
        compiler_params=_cparams(("parallel", "parallel")),
        name="band_attn_sample",
    )(z3, z3, z3, *([k_cache] * hb), *([v_cache] * hb), bias)


def _mem_attn_kernel(q_ref, k_ref, v_ref, o_ref, *, head_major):
    scores = []
    for h in range(N_MEM_HEADS):
        sl = slice(h * HEAD_DIM, (h + 1) * HEAD_DIM)
        k = k_ref[h] if head_major else k_ref[:, sl]
        scores.append(_nt_dot(q_ref[:, sl].astype(BF16), k.astype(BF16)))
    for h in range(N_MEM_HEADS):
        sl = slice(h * HEAD_DIM, (h + 1) * HEAD_DIM)
        v = v_ref[h] if head_major else v_ref[:, sl]
        s = scores[h] * (HEAD_DIM ** -0.5)
        p = _softmax_rows(s).astype(BF16)
        o_ref[:, sl] = jnp.dot(p, v.astype(BF16), preferred_element_type=F32).astype(BF16)


def _mem_attention(z3, q_blk, karr, kspec, varr, vspec, head_major):
    b, t, _ = z3.shape
    tq = min(t, TQ_MEM)
    return pl.pallas_call(
        functools.partial(_mem_attn_kernel, head_major=head_major),
        grid=(b, t // tq),
        in_specs=[
            pl.BlockSpec((None, tq, MEM_W), lambda i, j: (i, j, q_blk)),
            kspec, vspec,
        ],
        out_specs=pl.BlockSpec((None, tq, MEM_W), lambda i, j: (i, j, 0)),
        out_shape=jax.ShapeDtypeStruct((b, t, MEM_W), BF16),
        compiler_params=_cparams(("parallel", "arbitrary")),
        name="mem_attn",
    )(z3, karr, varr)


def _hgrn_kernel(*refs, hb, t, layer, has_state):
    if has_state:
        q_ref, f_ref, i_ref, g_ref, lbp_ref, gh_ref, s0_ref, o_ref, sout_ref, st_ref = refs
    else:
        q_ref, f_ref, i_ref, g_ref, lbp_ref, gh_ref, o_ref, sout_ref, st_ref = refs
    c = pl.program_id(2)
    nsub = t // SUB

    @pl.when(c == 0)
    def _():
        for h in range(hb):
            if has_state:
                st_ref[h] = s0_ref[h].T
            else:
                st_ref[h] = jnp.zeros((HEAD_DIM, HEAD_DIM), F32)

    row = lax.broadcasted_iota(jnp.int32, (t, t), 0)
    col = lax.broadcasted_iota(jnp.int32, (t, t), 1)
    rowd = lax.broadcasted_iota(jnp.int32, (t, HEAD_DIM), 0)
    hits = [jnp.logical_and(row - col == dl, jnp.bitwise_and(row, SUB - 1) >= dl) for dl in range(SUB)]
    sub_shift = SUB.bit_length() - 1
    far = jnp.right_shift(col, sub_shift) < jnp.right_shift(row, sub_shift)

    pending = []
    for h in range(hb):
        sl = slice(h * HEAD_DIM, (h + 1) * HEAD_DIM)
        q = q_ref[:, sl]
        ff = f_ref[:, sl]
        v = i_ref[:, sl]

        lbp = lbp_ref[:, sl]
        e = jnp.exp(lbp - jnp.max(lbp, axis=0, keepdims=True))
        soft = e / jnp.sum(e, axis=0, keepdims=True)
        acc = soft[0:1]
        for r in range(1, layer + 1):
            acc = acc + soft[r:r + 1]
        lb = acc - soft[0:1]

        en = jnp.exp(-jnp.abs(ff))
        rec = 1.0 / (1.0 + en)
        sig_pos = jnp.where(ff >= 0.0, rec, en * rec)
        sig_neg = jnp.where(ff >= 0.0, en * rec, rec)
        f = lb + (1.0 - lb) * sig_pos
        log_f = jnp.where(f > 0.0, jnp.log(f), ff)
        k = (1.0 - lb) * sig_neg

        b = log_f
        sh = 1
        while sh < t:
            b = b + jnp.where(rowd >= sh, pltpu.roll(b, sh, 0), 0.0)
            sh *= 2
        b_end = b[t - 1:t]

        st = st_ref[h]
        o = _nt_dot((q * jnp.exp(b)).astype(BF16), st.astype(BF16))

        sc = jnp.zeros((t, t), F32)
        for dl in range(SUB):
            if dl == 0:
                prod = q * k
            else:
                prod = q * pltpu.roll(k, dl, 0) * jnp.exp(jnp.minimum(b - pltpu.roll(b, dl, 0), 0.0))
            r = jnp.sum(prod, axis=1, keepdims=True)
            sc = jnp.where(hits[dl], r, sc)

        q_parts = []
        k_rows = []
        for i in range(nsub - 1):
            lo = (i + 1) * SUB
            b_ref = b[lo - 1:lo]
            later = q[lo:] * jnp.exp(jnp.minimum(b[lo:] - b_ref, 0.0))
            q_parts.append(jnp.concatenate([jnp.zeros((lo, HEAD_DIM), F32), later], axis=0).astype(BF16))
            kh = k[lo - SUB:lo] * jnp.exp(jnp.minimum(b_ref - b[lo - SUB:lo], 0.0))
            zero = jnp.zeros_like(kh)
            k_rows.append(jnp.concatenate([kh if j == i else zero for j in range(nsub - 1)], axis=1))
        k_rows.append(jnp.zeros((SUB, (nsub - 1) * HEAD_DIM), F32))
        q_cat = jnp.concatenate(q_parts, axis=1)
        k_bd = jnp.concatenate(k_rows, axis=0).astype(BF16)
        sc_far = _nt_dot(q_cat, k_bd)
        scores = jnp.where(far, sc_far, sc)

        k_dec = (k * jnp.exp(jnp.minimum(b_end - b, 0.0))).astype(BF16)
        st_ref[h] = st * jnp.exp(b_end) + _tn_dot(v.astype(BF16), k_dec)
        pending.append((o, scores.astype(BF16)))

    for h in range(hb):
        sl = slice(h * HEAD_DIM, (h + 1) * HEAD_DIM)
        o, scores = pending[h]
        g = g_ref[:, sl]
        o = o + jnp.dot(scores, i_ref[:, sl].astype(BF16), preferred_element_type=F32)
        o = _rms(o, gh_ref[:, sl])
        o = o * (g * jax.nn.sigmoid(g))
        o_ref[:, sl] = o.astype(BF16)

    @pl.when(c == pl.num_programs(2) - 1)
    def _():
        for h in range(hb):
            sout_ref[h] = st_ref[h].T


def _hgrn(z3, lb_param, g_head, state, state_idx, layer, hb=HB_HGRN):
    assert SUB & (SUB - 1) == 0
    b, s, _ = z3.shape
    t = min(CHUNK, s)
    nc = s // t
    ng = N_MIX_HEADS // hb
    w = hb * HEAD_DIM
    depth = lb_param.shape[0]
    has_state = state is not None

    def zspec(seg):
        return pl.BlockSpec((None, t, w), lambda i, g, c, seg=seg: (i, c, seg * ng + g))

    in_specs = [zspec(0), zspec(1), zspec(2), zspec(3),
                pl.BlockSpec((depth, w), lambda i, g, c: (0, g)),
                pl.BlockSpec((1, w), lambda i, g, c: (0, g))]
    args = [z3, z3, z3, z3, lb_param, g_head.reshape(1, MIX_W)]
    if has_state:
        in_specs.append(pl.BlockSpec((None, None, hb, HEAD_DIM, HEAD_DIM),
                                     lambda i, g, c: (state_idx, i, g, 0, 0)))
        args.append(state)
    return pl.pallas_call(
        functools.partial(_hgrn_kernel, hb=hb, t=t, layer=layer, has_state=has_state),
        grid=(b, ng, nc),
        in_specs=in_specs,
        out_specs=[
            pl.BlockSpec((None, t, w), lambda i, g, c: (i, c, g)),
            pl.BlockSpec((None, hb, HEAD_DIM, HEAD_DIM), lambda i, g, c: (i, g, 0, 0)),
        ],
        out_shape=[
            jax.ShapeDtypeStruct((b, s, MIX_W), BF16),
            jax.ShapeDtypeStruct((b, N_MIX_HEADS, HEAD_DIM, HEAD_DIM), F32),
        ],
        scratch_shapes=[pltpu.VMEM((hb, HEAD_DIM, HEAD_DIM), F32)],
        compiler_params=_cparams(("parallel", "parallel", "arbitrary")),
        name="hgrn2",
    )(*args)


def _out_proj_kernel(om_ref, oe_ref, wm_ref, we_ref, x_ref, g_ref, x1_ref, xnt_ref):
    x1 = x_ref[...] + jnp.dot(om_ref[...], wm_ref[...], preferred_element_type=F32)
    x1 = x1 + jnp.dot(oe_ref[...], we_ref[...], preferred_element_type=F32)
    x1_ref[...] = x1
    xnt_ref[...] = _rms(x1, g_ref[...]).T.astype(BF16)


def _out_proj(o_mix, o_mem, w_out, layer, x, g2):
    n, d = x.shape
    tm = min(TM_OUT, n)
    once = pl.Buffered(1)
    return pl.pallas_call(
        _out_proj_kernel,
        grid=(n // tm,),
        in_specs=[
            pl.BlockSpec((tm, MIX_W), lambda i: (i, 0)),
            pl.BlockSpec((tm, MEM_W), lambda i: (i, 0)),
            pl.BlockSpec((None, MIX_W, d), lambda i: (layer, 0, 0), pipeline_mode=once),
            pl.BlockSpec((None, MEM_W, d), lambda i: (layer, MIX_W // MEM_W, 0), pipeline_mode=once),
            pl.BlockSpec((tm, d), lambda i: (i, 0)),
            pl.BlockSpec((1, d), lambda i: (0, 0)),
        ],
        out_specs=[
            pl.BlockSpec((tm, d), lambda i: (i, 0)),
            pl.BlockSpec((d, tm), lambda i: (0, i)),
        ],
        out_shape=[jax.ShapeDtypeStruct((n, d), F32), jax.ShapeDtypeStruct((d, n), BF16)],
        compiler_params=_cparams(("parallel",)),
        name="out_proj",
    )(o_mix, o_mem, w_out, w_out, x, g2.reshape(1, d))


_REMOVED = 2.0 ** 100


def _extract_exact(vals, order, rounds):
    rank = jnp.full(vals.shape, float(PEER_TOPK), F32)
    tops = []
    for r in range(rounds):
        m = jnp.max(vals, axis=0, keepdims=True)
        pick = jnp.min(jnp.where(vals == m, order, 1e9), axis=0, keepdims=True)
        sel = order == pick
        rank = jnp.where(sel, float(r), rank)
        vals = jnp.where(sel, -jnp.inf, vals)
        tops.append(m)
    return jnp.concatenate(tops, axis=0), rank


def _extract_fast(vals, rounds):
    tops = []
    for r in range(rounds):
        m = jnp.max(vals, axis=0, keepdims=True)
        vals = jnp.where(vals == m, -_REMOVED * (r + 1), vals)
        tops.append(m)
    rank = jnp.where(vals <= -_REMOVED, vals * (-1.0 / _REMOVED) - 1.0, float(PEER_TOPK))
    return jnp.concatenate(tops, axis=0), rank


def _route_lanes(s1, s2, exact):
    lanes = s1.shape[1]
    if exact:
        key_id = lax.broadcasted_iota(jnp.int32, (N_KEYS, lanes), 0).astype(F32)
        top1, rank1 = _extract_exact(s1, key_id, PEER_TOPK)
        top2, rank2 = _extract_exact(s2, key_id, PEER_TOPK)
    else:
        top1, rank1 = _extract_fast(s1, PEER_TOPK)
        top2, rank2 = _extract_fast(s2, PEER_TOPK)

    blk_a = top1[0:1] + top2
    blk_b = [top1[a:a + 1] + top2[0:8] for a in range(1, 8)]
    blk_c = top1[8:16] + top2[0:1]
    cand = jnp.concatenate([blk_a] + blk_b + [blk_c], axis=0)
    nrow = cand.shape[0]
    r = lax.broadcasted_iota(jnp.int32, (nrow, lanes), 0)
    a_mid = 1 + jnp.right_shift(r - 16, 3)
    b_mid = jnp.bitwise_and(r - 16, 7)
    valid = jnp.logical_or(jnp.logical_or(r < 16, r >= 72), (a_mid + 1) * (b_mid + 1) <= PEER_TOPK)
    cand = jnp.where(valid, cand, -jnp.inf)
    if exact:
        cand_id = jnp.where(r < 16, r, jnp.where(r < 72, a_mid * PEER_TOPK + b_mid, (r - 64) * PEER_TOPK))
        best, crank = _extract_exact(cand, cand_id.astype(F32), PEER_TOPK)
    else:
        best, crank = _extract_fast(cand, PEER_TOPK)
    chosen = jnp.logical_and(valid, crank < PEER_TOPK).astype(F32)
    z = jnp.sum(jnp.exp(best - best[0:1]), axis=0, keepdims=True)

    counts = [jnp.sum(chosen[0:16], axis=0, keepdims=True)]
    counts += [jnp.sum(chosen[16 + 8 * (a - 1):16 + 8 * a], axis=0, keepdims=True) for a in range(1, 8)]
    counts += [chosen[72 + a - 8:72 + a - 7] for a in range(8, 16)]
    c1 = jnp.zeros((N_KEYS, lanes), F32)
    for a in range(PEER_TOPK):
        c1 = jnp.where(rank1 == a, counts[a], c1)

    ranked = (jnp.sum((rank1 < PEER_TOPK).astype(F32), axis=0, keepdims=True)
              + jnp.sum((rank2 < PEER_TOPK).astype(F32), axis=0, keepdims=True)
              + jnp.sum(chosen, axis=0, keepdims=True))
    low = jnp.minimum(jnp.min(s1, axis=0, keepdims=True), jnp.min(s2, axis=0, keepdims=True))
    redo = jnp.logical_or(ranked != 3.0 * PEER_TOPK, jnp.logical_not(low > -0.25 * _REMOVED)).astype(F32)
    e1 = jnp.exp(s1 - top1[0:1])
    e2 = jnp.exp(s2 - top2[0:1]) / z
    return rank2, e2, c1, e1, redo


def _scores_kernel(xnt_ref, wq_ref, keys_ref, s_ref):
    qq = jnp.dot(wq_ref[...], xnt_ref[...], preferred_element_type=F32).astype(BF16)
    for h in range(PEER_HEADS):
        for c in range(2):
            row = (2 * h + c) * N_KEYS
            s_ref[2 * h + c] = jnp.dot(keys_ref[c, h], qq[row:row + N_KEYS], preferred_element_type=F32)


def _peer_scores(xnt, w_pq_t, keys, layer):
    d, n = xnt.shape
    tb = min(TB_SCORES, n)
    return pl.pallas_call(
        _scores_kernel,
        grid=(n // tb,),
        in_specs=[
            pl.BlockSpec((d, tb), lambda i: (0, i)),
            pl.BlockSpec((None,) + w_pq_t.shape[1:], lambda i: (layer, 0, 0)),
            pl.BlockSpec((None,) + keys.shape[1:], lambda i: (layer, 0, 0, 0, 0)),
        ],
        out_specs=pl.BlockSpec((2 * PEER_HEADS, N_KEYS, tb), lambda i: (0, 0, i)),
        out_shape=jax.ShapeDtypeStruct((2 * PEER_HEADS, N_KEYS, n), F32),
        compiler_params=_cparams(("parallel",)),
        name="peer_scores",
    )(xnt, w_pq_t, keys)


def _route_kernel(s_ref, r2_ref, e2_ref, c1_ref, e1_ref, *, hr):
    def emit(h, exact):
        rank2, e2, c1, e1, redo = _route_lanes(s_ref[2 * h], s_ref[2 * h + 1], exact)
        r2_ref[h] = rank2.astype(BF16)
        e2_ref[h] = e2.astype(BF16)
        c1_ref[h] = c1
        e1_ref[h] = e1
        return redo

    redo = [emit(h, exact=False) for h in range(hr)]
    for h in range(hr):
        @pl.when(jnp.max(redo[h]) > 0.0)
        def _():
            emit(h, exact=True)


def _peer_route(scores, hr=HR_ROUTE):
    n = scores.shape[2]
    tb = min(TB_ROUTE, n)
    out = jax.ShapeDtypeStruct((PEER_HEADS, N_KEYS, n), F32)
    out16 = jax.ShapeDtypeStruct((PEER_HEADS, N_KEYS, n), BF16)
    ospec = pl.BlockSpec((hr, N_KEYS, tb), lambda i, h: (h, 0, i))
    return pl.pallas_call(
        functools.partial(_route_kernel, hr=hr),
        grid=(n // tb, PEER_HEADS // hr),
        in_specs=[pl.BlockSpec((2 * hr, N_KEYS, tb), lambda i, h: (h, 0, i))],
        out_specs=[ospec, ospec, ospec, ospec],
        out_shape=[out16, out16, out, out],
        compiler_params=_cparams(("parallel", "parallel")),
        name="peer_route",
    )(scores)


def _gelu(x):
    return 0.5 * x * (1.0 + lax.erf(x * (0.5 ** 0.5)))


def _peer_kernel(xnt_ref, u_ref, un_ref, vt_ref, r2_ref, e2_ref, c1_ref, e1_ref, x1_ref, g_ref, o_ref,
                 y_ref, a_ref, coef_ref, *, eb, ck, lc, final):
    e = pl.program_id(1)
    tb = xnt_ref.shape[1]
    per = ck // N_KEYS
    nck = eb // ck

    @pl.when(e == 0)
    def _():
        y_ref[...] = jnp.zeros_like(y_ref)
        a_ref[0] = jnp.dot(u_ref[0:ck, :], xnt_ref[...], preferred_element_type=F32)

    for k in range(nck):
        if k + 1 < nck:
            a_next = jnp.dot(u_ref[(k + 1) * ck:(k + 2) * ck, :], xnt_ref[...], preferred_element_type=F32)
            a_ref[k + 1] = a_next
        else:
            a_next = jnp.dot(un_ref[...], xnt_ref[...], preferred_element_type=F32)
        bits = pltpu.bitcast(a_next[0:1, tb - lc:], jnp.uint32)
        bits = lax.shift_right_logical(lax.shift_right_logical(bits, jnp.uint32(16)), jnp.uint32(16))
        zero = pltpu.bitcast(bits, F32)
        for ii in range(per):
            i = k * per + ii
            rows = slice(ii * N_KEYS, (ii + 1) * N_KEYS)
            for l0 in range(0, tb, lc):
                cols = slice(l0, l0 + lc)
                gate = jnp.zeros((N_KEYS, lc), BF16)
                for h in range(PEER_HEADS):
                    c1 = jnp.broadcast_to(c1_ref[h, i:i + 1, cols], (N_KEYS, lc)).astype(BF16)
                    e1 = jnp.broadcast_to(e1_ref[h, i:i + 1, cols], (N_KEYS, lc)).astype(BF16)
                    zero16 = jnp.zeros((N_KEYS, lc), BF16)
                    gate = gate + jnp.where(r2_ref[h, :, cols] < c1, e2_ref[h, :, cols], zero16) * e1
                act = gate.astype(F32) * _gelu(a_ref[k, rows, cols])
                if ii == per - 1 and l0 + lc == tb:
                    act = act + zero
                coef_ref[k, rows, cols] = act.astype(BF16)
        if k + 1 == nck:
            a_ref[0] = a_next
        y_ref[...] += jnp.dot(vt_ref[:, k * ck:(k + 1) * ck], coef_ref[k], preferred_element_type=F32)

    @pl.when(e == pl.num_programs(1) - 1)
    def _():
        x = x1_ref[...] + y_ref[...].T
        if final:
            x = _rms(x, g_ref[...])
        o_ref[...] = x


def _peer_sweep(xnt, u, vt, layer, r2, e2, c1, e1, x1, g, final):
    d, n = xnt.shape
    tb = min(TB_PEER, n)
    eb = EB_PEER
    ck = CK_PEER
    lc = min(LC_PEER, tb)
    rspec = pl.BlockSpec((PEER_HEADS, N_KEYS, tb), lambda t, e: (0, 0, t))
    ispec = pl.BlockSpec((PEER_HEADS, eb // N_KEYS, tb), lambda t, e: (0, e, t))
    last_ck = N_EXPERTS // ck - 1
    return pl.pallas_call(
        functools.partial(_peer_kernel, eb=eb, ck=ck, lc=lc, final=final),
        grid=(n // tb, N_EXPERTS // eb),
        in_specs=[
            pl.BlockSpec((d, tb), lambda t, e: (0, t)),
            pl.BlockSpec((None, eb, d), lambda t, e: (layer, e, 0)),
            pl.BlockSpec((None, ck, d), lambda t, e: (layer, jnp.minimum((e + 1) * (eb // ck), last_ck), 0)),
            pl.BlockSpec((None, d, eb), lambda t, e: (layer, 0, e)),
            rspec, rspec, ispec, ispec,
            pl.BlockSpec((tb, d), lambda t, e: (t, 0)),
            pl.BlockSpec((1, d), lambda t, e: (0, 0)),
        ],
        out_specs=pl.BlockSpec((tb, d), lambda t, e: (t, 0)),
        out_shape=jax.ShapeDtypeStruct((n, d), F32),
        scratch_shapes=[pltpu.VMEM((d, tb), F32),
                        pltpu.VMEM((eb // ck, ck, tb), F32), pltpu.VMEM((eb // ck, ck, tb), BF16)],
        compiler_params=_cparams(("parallel", "arbitrary")),
        name="peer_sweep",
    )(xnt, u, u, vt, r2, e2, c1, e1, x1, g.reshape(1, d))


def _trunk(x, mem, k_cache, v_cache, s_cache, mk_cache, mv_cache, p):
    prompt = mem is not None
    bsz, t, d = x.shape
    n = bsz * t
    depth = p["g_norm1"].shape[0]
    x2 = x.reshape(n, d)
    new_k, new_v, new_s, new_mk, new_mv = [], [], [], [], []
    for i in range(depth):
        j = i // 2
        attn = i % 2 == 0
        w_in = p["w_in_a"] if attn else p["w_in_b"]
        w_out = p["w_out_a"] if attn else p["w_out_b"]
        z = _norm_matmul(x2, p["g_norm1"][i], w_in, j)
        z3 = z.reshape(bsz, t, z.shape[1])
        mq_blk = (z.shape[1] - MEM_W) // MEM_W

        if prompt:
            mkv = _norm_matmul(mem.reshape(bsz * N_MEM, d), p["g_mem"][i], p["w_mem_kv"], i)
            new_mk.append(mkv[:, :MEM_W].reshape(bsz, N_MEM, N_MEM_HEADS, HEAD_DIM))
            new_mv.append(mkv[:, MEM_W:].reshape(bsz, N_MEM, N_MEM_HEADS, HEAD_DIM))
            mkv3 = mkv.reshape(bsz, N_MEM, 2 * MEM_W)
            kspec = pl.BlockSpec((None, N_MEM, MEM_W), lambda b, q: (b, 0, 0))
            vspec = pl.BlockSpec((None, N_MEM, MEM_W), lambda b, q: (b, 0, 1))
            o_mem = _mem_attention(z3, mq_blk, mkv3, kspec, mkv3, vspec, head_major=False)
        else:
            cspec = pl.BlockSpec((None, None, N_MEM_HEADS, N_MEM, HEAD_DIM), lambda b, q, i=i: (i, b, 0, 0, 0))
            o_mem = _mem_attention(z3, mq_blk, jnp.transpose(mk_cache, (0, 1, 3, 2, 4)), cspec,
                                   jnp.transpose(mv_cache, (0, 1, 3, 2, 4)), cspec, head_major=True)

        if attn:
            k_new = z3[:, :, MIX_W:2 * MIX_W]
            v_new = z3[:, :, 2 * MIX_W:3 * MIX_W]
            if prompt:
                bias = _band_bias(p["rel_bias"][j], CHUNK, WINDOW + CHUNK, WINDOW)
                o_mix = _band_attention_prompt(z3, bias)
                keep = min(WINDOW, t)
                k_new, v_new = k_new[:, t - keep:], v_new[:, t - keep:]
            else:
                p_len = k_cache.shape[2]
                bias = _band_bias(p["rel_bias"][j], t, p_len + t, p_len)
                o_mix = _band_attention_sample(z3, k_cache, v_cache, j, bias)
            new_k.append(k_new.reshape(bsz, -1, N_MIX_HEADS, HEAD_DIM))
            new_v.append(v_new.reshape(bsz, -1, N_MIX_HEADS, HEAD_DIM))
        else:
            o_mix, s_new = _hgrn(z3, p["lb_param"], p["g_hgrn"][j], None if prompt else s_cache, j, i)
            new_s.append(s_new)

        x1, xnt = _out_proj(o_mix.reshape(n, MIX_W), o_mem.reshape(n, MEM_W), w_out, j, x2, p["g_norm2"][i])
        r2, e2, c1, e1 = _peer_route(_peer_scores(xnt, p["w_pq_t"], p["peer_keys"], i))
        x2 = _peer_sweep(xnt, p["peer_u"], p["peer_v_t"], i, r2, e2, c1, e1, x1, p["g_final"],
                         final=(i == depth - 1))

    y = x2.reshape(bsz, t, d)
    mk_out = jnp.stack(new_mk) if new_mk else None
    mv_out = jnp.stack(new_mv) if new_mv else None
    return y, jnp.stack(new_k), jnp.stack(new_v), jnp.stack(new_s), mk_out, mv_out


def kernel(x_prompt, x_sample, cache_attn_k, cache_attn_v, state_hgrn, cache_mem_k, cache_mem_v, mem_prompt,
           g_norm1, g_norm2, g_mem, g_final, w_in_a, w_out_a, rel_bias, w_in_b, w_out_b, lb_param, g_hgrn,
           w_mem_kv, w_pq, peer_keys, peer_u, peer_v):
    p = dict(
        g_norm1=g_norm1, g_norm2=g_norm2, g_mem=g_mem, g_final=g_final, rel_bias=rel_bias,
        lb_param=lb_param, g_hgrn=g_hgrn,
        w_in_a=w_in_a.astype(BF16), w_out_a=w_out_a.astype(BF16),
        w_in_b=w_in_b.astype(BF16), w_out_b=w_out_b.astype(BF16),
        w_mem_kv=w_mem_kv.astype(BF16),
        w_pq_t=jnp.swapaxes(w_pq, 1, 2).astype(BF16),
        peer_keys=peer_keys.astype(BF16),
        peer_u=peer_u.astype(BF16),
        peer_v_t=jnp.swapaxes(peer_v, 1, 2).astype(BF16),
    )
    y_p, k_p, v_p, s_p, mk_p, mv_p = _trunk(x_prompt, mem_prompt, None, None, None, None, None, p)
    y_s, k_s, v_s, s_s, _, _ = _trunk(x_sample, None, cache_attn_k, cache_attn_v, state_hgrn,
                                      cache_mem_k, cache_mem_v, p)
    return (y_p, y_s, k_p, v_p, k_s, v_s, s_p, s_s, mk_p, mv_p)
```

```python
import functools

import jax
import jax.numpy as jnp
from jax import lax
from jax.experimental import pallas as pl
from jax.experimental.pallas import tpu as pltpu

F32 = jnp.float32
BF16 = jnp.bfloat16

PAST_LEN = 1024
CHUNK = 64
LEFT_CHUNKS = 8
WINDOW = LEFT_CHUNKS * CHUNK
HEAD_DIM = 128
N_MIX_HEADS = 12
N_MEM_HEADS = 4
MIX_W = N_MIX_HEADS * HEAD_DIM
MEM_W = N_MEM_HEADS * HEAD_DIM
N_MEM = 256
REL_CLIP = 256
PEER_HEADS = 8
N_KEYS = 128
N_EXPERTS = N_KEYS * N_KEYS
PEER_TOPK = 16
EPS = 1e-6
NEG_INF = -1e30

LANES = 128
SUBLANES = 8
SUB = SUBLANES
VMEM_LIMIT = 56 * 1024 * 1024

TM_PROJ = 1024
TN_PROJ = 1664
TM_OUT = 512
TQ_MEM = 512
HG_BIAS = 4
TB_SCORES = 512
TB_ROUTE = 256
HR_ROUTE = 2
TB_PEER = 512
EB_PEER = 1024
CK_PEER = 512
LC_PEER = 256
QB_ATTN = 256
HB_HGRN = 12


def _cparams(sem):
    return pltpu.CompilerParams(dimension_semantics=sem, vmem_limit_bytes=VMEM_LIMIT)


def _rms(x, g):
    return x * lax.rsqrt(jnp.mean(x * x, axis=-1, keepdims=True) + EPS) * g


def _norm_matmul_kernel(x_ref, g_ref, w_ref, o_ref, hn_ref):
    @pl.when(pl.program_id(1) == 0)
    def _():
        hn_ref[...] = _rms(x_ref[...], g_ref[...]).astype(BF16)

    o_ref[...] = jnp.dot(hn_ref[...], w_ref[...], preferred_element_type=F32)


def _norm_matmul(x, g, w, layer):
    m, d = x.shape
    n = w.shape[2]
    tm = min(TM_PROJ, m)
    tn = n // -(-n // TN_PROJ)
    assert n % tn == 0 and tn % LANES == 0
    return pl.pallas_call(
        _norm_matmul_kernel,
        grid=(m // tm, n // tn),
        in_specs=[
            pl.BlockSpec((tm, d), lambda i, j: (i, 0)),
            pl.BlockSpec((1, d), lambda i, j: (0, 0)),
            pl.BlockSpec((None, d, tn), lambda i, j: (layer, 0, j)),
        ],
        out_specs=pl.BlockSpec((tm, tn), lambda i, j: (i, j)),
        out_shape=jax.ShapeDtypeStruct((m, n), F32),
        scratch_shapes=[pltpu.VMEM((tm, d), BF16)],
        compiler_params=_cparams(("parallel", "arbitrary")),
        name="norm_matmul",
    )(x, g.reshape(1, d), w)


def _bias_kernel(tab_ref, o_ref, *, tq, tk, off, hg):
    h0 = pl.program_id(0) * hg
    clip = lambda d: max(min(d, REL_CLIP), -REL_CLIP) + REL_CLIP
    for c0 in range(0, tk, LANES):
        w = min(LANES, tk - c0)
        qi = lax.broadcasted_iota(jnp.int32, (tq, w), 0)
        ki = lax.broadcasted_iota(jnp.int32, (tq, w), 1) + c0
        rel = jnp.clip(qi - ki + off, -REL_CLIP, REL_CLIP) + REL_CLIP
        lo = clip(-(c0 + w - 1) + off)
        hi = clip((tq - 1) - c0 + off)

        def body(r, accs, rel=rel):
            hit = rel == r
            return tuple(jnp.where(hit, tab_ref[h0 + h, r], acc) for h, acc in enumerate(accs))

        accs = lax.fori_loop(lo, hi + 1, body, tuple(jnp.zeros((tq, w), F32) for _ in range(hg)))
        for h in range(hg):
            o_ref[h, :, c0:c0 + w] = accs[h]


def _band_bias(table, tq, tk, off, hg=HG_BIAS):
    nh = table.shape[1]
    return pl.pallas_call(
        functools.partial(_bias_kernel, tq=tq, tk=tk, off=off, hg=hg),
        grid=(nh // hg,),
        in_specs=[pl.BlockSpec(memory_space=pltpu.SMEM)],
        out_specs=pl.BlockSpec((hg, tq, tk), lambda g: (g, 0, 0)),
        out_shape=jax.ShapeDtypeStruct((nh, tq, tk), F32),
        compiler_params=_cparams(("arbitrary",)),
        name="band_bias",
    )(table.T)


def _softmax_rows(s):
    m = jnp.max(s, axis=-1, keepdims=True)
    p = jnp.exp(s - m)
    return p / jnp.sum(p, axis=-1, keepdims=True)


def _nt_dot(a, b):
    return lax.dot_general(a, b, (((1,), (1,)), ((), ())), preferred_element_type=F32)


def _tn_dot(a, b):
    return lax.dot_general(a, b, (((0,), (0,)), ((), ())), preferred_element_type=F32)


def _band_prompt_kernel(*refs, hb, qb, nwb):
    q_ref = refs[0]
    k_refs = refs[1:1 + nwb]
    v_refs = refs[1 + nwb:1 + 2 * nwb]
    bias_ref, o_ref = refs[1 + 2 * nwb:]
    c = pl.program_id(2)
    tk = WINDOW + qb
    k_pos = lax.broadcasted_iota(jnp.int32, (qb, tk), 1) + (c * qb - WINDOW)
    live = k_pos >= 0
    scale = HEAD_DIM ** -0.5
    scores = []
    for h in range(hb):
        sl = slice(h * HEAD_DIM, (h + 1) * HEAD_DIM)
        kw = jnp.concatenate([r[:, sl].astype(BF16) for r in k_refs], axis=0)
        scores.append(_nt_dot(q_ref[:, sl].astype(BF16), kw))
    for h in range(hb):
        sl = slice(h * HEAD_DIM, (h + 1) * HEAD_DIM)
        vw = jnp.concatenate([r[:, sl].astype(BF16) for r in v_refs], axis=0)
        s = scores[h] * scale + bias_ref[h]
        s = jnp.where(live, s, NEG_INF)
        p = _softmax_rows(s).astype(BF16)
        o_ref[:, sl] = jnp.dot(p, vw, preferred_element_type=F32).astype(BF16)


def _band_attention_prompt(z3, bias, hb=N_MIX_HEADS):
    b, s, _ = z3.shape
    qb = min(QB_ATTN, s)
    assert WINDOW % qb == 0
    nqc = qb // CHUNK
    nwb = WINDOW // qb + 1
    ng = N_MIX_HEADS // hb
    w = hb * HEAD_DIM
    tk = WINDOW + qb

    def win_spec(j, seg):
        return pl.BlockSpec((None, qb, w), lambda i, g, c, j=j, seg=seg:
                            (i, jnp.maximum(c + j - (nwb - 1), 0), seg * ng + g))
    bias_blk = jnp.concatenate(
        [jnp.pad(bias, ((0, 0), (0, 0), (j * CHUNK, (nqc - 1 - j) * CHUNK)), constant_values=NEG_INF)
         for j in range(nqc)], axis=1)
    return pl.pallas_call(
        functools.partial(_band_prompt_kernel, hb=hb, qb=qb, nwb=nwb),
        grid=(b, ng, s // qb),
        in_specs=[
            win_spec(nwb - 1, 0),
            *[win_spec(j, 1) for j in range(nwb)],
            *[win_spec(j, 2) for j in range(nwb)],
            pl.BlockSpec((hb, qb, tk), lambda i, g, c: (g, 0, 0), pipeline_mode=pl.Buffered(1)),
        ],
        out_specs=pl.BlockSpec((None, qb, w), lambda i, g, c: (i, c, g)),
        out_shape=jax.ShapeDtypeStruct((b, s, MIX_W), BF16),
        compiler_params=_cparams(("parallel", "parallel", "arbitrary")),
        name="band_attn_prompt",
    )(*([z3] * (1 + 2 * nwb)), bias_blk)


def _band_sample_kernel(*refs, hb, p_len):
    q_ref, kn_ref, vn_ref = refs[:3]
    kc_refs = refs[3:3 + hb]
    vc_refs = refs[3 + hb:3 + 2 * hb]
    bias_ref, o_ref = refs[3 + 2 * hb:]
    scale = HEAD_DIM ** -0.5
    scores = []
    for h in range(hb):
        sl = slice(h * HEAD_DIM, (h + 1) * HEAD_DIM)
        q = q_ref[:, sl].astype(BF16)
        scores.append((_nt_dot(q, kc_refs[h][...].astype(BF16)), _nt_dot(q, kn_ref[:, sl].astype(BF16))))
    for h in range(hb):
        sl = slice(h * HEAD_DIM, (h + 1) * HEAD_DIM)
        s_c = scores[h][0] * scale + bias_ref[h, :, :p_len]
        s_n = scores[h][1] * scale + bias_ref[h, :, p_len:]
        m = jnp.maximum(jnp.max(s_c, axis=-1, keepdims=True), jnp.max(s_n, axis=-1, keepdims=True))
        e_c = jnp.exp(s_c - m)
        e_n = jnp.exp(s_n - m)
        l = jnp.sum(e_c, axis=-1, keepdims=True) + jnp.sum(e_n, axis=-1, keepdims=True)
        o = jnp.dot((e_c / l).astype(BF16), vc_refs[h][...].astype(BF16), preferred_element_type=F32)
        o = o + jnp.dot((e_n / l).astype(BF16), vn_ref[:, sl].astype(BF16), preferred_element_type=F32)
        o_ref[:, sl] = o.astype(BF16)


def _band_attention_sample(z3, k_cache, v_cache, layer, bias, hb=N_MIX_HEADS):
    b, t, _ = z3.shape
    p_len = k_cache.shape[2]
    assert PAST_LEN % CHUNK == 0 and p_len == WINDOW and PAST_LEN >= p_len and t <= CHUNK
    ng = N_MIX_HEADS // hb
    w = hb * HEAD_DIM
    k_cache = jnp.transpose(k_cache, (0, 1, 3, 2, 4))
    v_cache = jnp.transpose(v_cache, (0, 1, 3, 2, 4))

    def cache_spec(h):
        return pl.BlockSpec((None, None, None, p_len, HEAD_DIM), lambda i, g, h=h: (layer, i, g * hb + h, 0, 0))

    cache_specs = [cache_spec(h) for h in range(hb)]
    return pl.pallas_call(
        functools.partial(_band_sample_kernel, hb=hb, p_len=p_len),
        grid=(b, ng),
        in_specs=[
            pl.BlockSpec((None, t, w), lambda i, g: (i, 0, g)),
            pl.BlockSpec((None, t, w), lambda i, g: (i, 0, ng + g)),
            pl.BlockSpec((None, t, w), lambda i, g: (i, 0, 2 * ng + g)),
            *cache_specs, *cache_specs,
            pl.BlockSpec((hb, t, p_len + t), lambda i, g: (g, 0, 0)),
        ],
        out_specs=pl.BlockSpec((None, t, w), lambda i, g: (i, 0, g)),
        out_shape=jax.ShapeDtypeStruct((b, t, MIX_W), BF16),
        compiler_params=_cparams(("parallel", "parallel")),
        name="band_attn_sample",
    )(z3, z3, z3, *([k_cache] * hb), *([v_cache] * hb), bias)


def _mem_attn_kernel(q_ref, k_ref, v_ref, o_ref, *, head_major):
    scores = []
    for h in range(N_MEM_HEADS):
        sl = slice(h * HEAD_DIM, (h + 1) * HEAD_DIM)
        k = k_ref[h] if head_major else k_ref[:, sl]
        scores.append(_nt_dot(q_ref[:, sl].astype(BF16), k.astype(BF16)))
    for h in range(N_MEM_HEADS):
        sl = slice(h * HEAD_DIM, (h + 1) * HEAD_DIM)
        v = v_ref[h] if head_major else v_ref[:, sl]
        s = scores[h] * (HEAD_DIM ** -0.5)
        p = _softmax_rows(s).astype(BF16)
        o_ref[:, sl] = jnp.dot(p, v.astype(BF16), preferred_element_type=F32).astype(BF16)


def _mem_attention(z3, q_blk, karr, kspec, varr, vspec, head_major):
    b, t, _ = z3.shape
    tq = min(t, TQ_MEM)
    return pl.pallas_call(
        functools.partial(_mem_attn_kernel, head_major=head_major),
        grid=(b, t // tq),
        in_specs=[
            pl.BlockSpec((None, tq, MEM_W), lambda i, j: (i, j, q_blk)),
            kspec, vspec,
        ],
        out_specs=pl.BlockSpec((None, tq, MEM_W), lambda i, j: (i, j, 0)),
        out_shape=jax.ShapeDtypeStruct((b, t, MEM_W), BF16),
        compiler_params=_cparams(("parallel", "arbitrary")),
        name="mem_attn",
    )(z3, karr, varr)


def _hgrn_kernel(*refs, hb, t, layer, has_state):
    if has_state:
        q_ref, f_ref, i_ref, g_ref, lbp_ref, gh_ref, s0_ref, o_ref, sout_ref, st_ref = refs
    else:
        q_ref, f_ref, i_ref, g_ref, lbp_ref, gh_ref, o_ref, sout_ref, st_ref = refs
    c = pl.program_id(2)
    nsub = t // SUB

    @pl.when(c == 0)
    def _():
        for h in range(hb):
            if has_state:
                st_ref[h] = s0_ref[h].T
            else:
                st_ref[h] = jnp.zeros((HEAD_DIM, HEAD_DIM), F32)

    row = lax.broadcasted_iota(jnp.int32, (t, t), 0)
    col = lax.broadcasted_iota(jnp.int32, (t, t), 1)
    rowd = lax.broadcasted_iota(jnp.int32, (t, HEAD_DIM), 0)
    hits = [jnp.logical_and(row - col == dl, jnp.bitwise_and(row, SUB - 1) >= dl) for dl in range(SUB)]
    sub_shift = SUB.bit_length() - 1
    far = jnp.right_shift(col, sub_shift) < jnp.right_shift(row, sub_shift)

    pending = []
    for h in range(hb):
        sl = slice(h * HEAD_DIM, (h + 1) * HEAD_DIM)
        q = q_ref[:, sl]
        ff = f_ref[:, sl]
        v = i_ref[:, sl]

        lbp = lbp_ref[:, sl]
        e = jnp.exp(lbp - jnp.max(lbp, axis=0, keepdims=True))
        soft = e / jnp.sum(e, axis=0, keepdims=True)
        acc = soft[0:1]
        for r in range(1, layer + 1):
            acc = acc + soft[r:r + 1]
        lb = acc - soft[0:1]

        en = jnp.exp(-jnp.abs(ff))
        rec = 1.0 / (1.0 + en)
        sig_pos = jnp.where(ff >= 0.0, rec, en * rec)
        sig_neg = jnp.where(ff >= 0.0, en * rec, rec)
        f = lb + (1.0 - lb) * sig_pos
        log_f = jnp.where(f > 0.0, jnp.log(f), ff)
        k = (1.0 - lb) * sig_neg

        b = log_f
        sh = 1
        while sh < t:
            b = b + jnp.where(rowd >= sh, pltpu.roll(b, sh, 0), 0.0)
            sh *= 2
        b_end = b[t - 1:t]

        st = st_ref[h]
        o = _nt_dot((q * jnp.exp(b)).astype(BF16), st.astype(BF16))

        sc = jnp.zeros((t, t), F32)
        for dl in range(SUB):
            if dl == 0:
                prod = q * k
            else:
                prod = q * pltpu.roll(k, dl, 0) * jnp.exp(jnp.minimum(b - pltpu.roll(b, dl, 0), 0.0))
            r = jnp.sum(prod, axis=1, keepdims=True)
            sc = jnp.where(hits[dl], r, sc)

        q_parts = []
        k_rows = []
        for i in range(nsub - 1):
            lo = (i + 1) * SUB
            b_ref = b[lo - 1:lo]
            later = q[lo:] * jnp.exp(jnp.minimum(b[lo:] - b_ref, 0.0))
            q_parts.append(jnp.concatenate([jnp.zeros((lo, HEAD_DIM), F32), later], axis=0).astype(BF16))
            kh = k[lo - SUB:lo] * jnp.exp(jnp.minimum(b_ref - b[lo - SUB:lo], 0.0))
            zero = jnp.zeros_like(kh)
            k_rows.append(jnp.concatenate([kh if j == i else zero for j in range(nsub - 1)], axis=1))
        k_rows.append(jnp.zeros((SUB, (nsub - 1) * HEAD_DIM), F32))
        q_cat = jnp.concatenate(q_parts, axis=1)
        k_bd = jnp.concatenate(k_rows, axis=0).astype(BF16)
        sc_far = _nt_dot(q_cat, k_bd)
        scores = jnp.where(far, sc_far, sc)

        k_dec = (k * jnp.exp(jnp.minimum(b_end - b, 0.0))).astype(BF16)
        st_ref[h] = st * jnp.exp(b_end) + _tn_dot(v.astype(BF16), k_dec)
        pending.append((o, scores.astype(BF16)))

    for h in range(hb):
        sl = slice(h * HEAD_DIM, (h + 1) * HEAD_DIM)
        o, scores = pending[h]
        g = g_ref[:, sl]
        o = o + jnp.dot(scores, i_ref[:, sl].astype(BF16), preferred_element_type=F32)
        o = _rms(o, gh_ref[:, sl])
        o = o * (g * jax.nn.sigmoid(g))
        o_ref[:, sl] = o.astype(BF16)

    @pl.when(c == pl.num_programs(2) - 1)
    def _():
        for h in range(hb):
            sout_ref[h] = st_ref[h].T


def _hgrn(z3, lb_param, g_head, state, state_idx, layer, hb=HB_HGRN):
    assert SUB & (SUB - 1) == 0
    b, s, _ = z3.shape
    t = min(CHUNK, s)
    nc = s // t
    ng = N_MIX_HEADS // hb
    w = hb * HEAD_DIM
    depth = lb_param.shape[0]
    has_state = state is not None

    def zspec(seg):
        return pl.BlockSpec((None, t, w), lambda i, g, c, seg=seg: (i, c, seg * ng + g))

    in_specs = [zspec(0), zspec(1), zspec(2), zspec(3),
                pl.BlockSpec((depth, w), lambda i, g, c: (0, g)),
                pl.BlockSpec((1, w), lambda i, g, c: (0, g))]
    args = [z3, z3, z3, z3, lb_param, g_head.reshape(1, MIX_W)]
    if has_state:
        in_specs.append(pl.BlockSpec((None, None, hb, HEAD_DIM, HEAD_DIM),
                                     lambda i, g, c: (state_idx, i, g, 0, 0)))
        args.append(state)
    return pl.pallas_call(
        functools.partial(_hgrn_kernel, hb=hb, t=t, layer=layer, has_state=has_state),
        grid=(b, ng, nc),
        in_specs=in_specs,
        out_specs=[
            pl.BlockSpec((None, t, w), lambda i, g, c: (i, c, g)),
            pl.BlockSpec((None, hb, HEAD_DIM, HEAD_DIM), lambda i, g, c: (i, g, 0, 0)),
        ],
        out_shape=[
            jax.ShapeDtypeStruct((b, s, MIX_W), BF16),
            jax.ShapeDtypeStruct((b, N_MIX_HEADS, HEAD_DIM, HEAD_DIM), F32),
        ],
        scratch_shapes=[pltpu.VMEM((hb, HEAD_DIM, HEAD_DIM), F32)],
        compiler_params=_cparams(("parallel", "parallel", "arbitrary")),
        name="hgrn2",
    )(*args)


def _out_proj_kernel(om_ref, oe_ref, wm_ref, we_ref, x_ref, g_ref, x1_ref, xnt_ref):
    x1 = x_ref[...] + jnp.dot(om_ref[...], wm_ref[...], preferred_element_type=F32)
    x1 = x1 + jnp.dot(oe_ref[...], we_ref[...], preferred_element_type=F32)
    x1_ref[...] = x1
    xnt_ref[...] = _rms(x1, g_ref[...]).T.astype(BF16)


def _out_proj(o_mix, o_mem, w_out, layer, x, g2):
    n, d = x.shape
    tm = min(TM_OUT, n)
    once = pl.Buffered(1)
    return pl.pallas_call(
        _out_proj_kernel,
        grid=(n // tm,),
        in_specs=[
            pl.BlockSpec((tm, MIX_W), lambda i: (i, 0)),
            pl.BlockSpec((tm, MEM_W), lambda i: (i, 0)),
            pl.BlockSpec((None, MIX_W, d), lambda i: (layer, 0, 0), pipeline_mode=once),
            pl.BlockSpec((None, MEM_W, d), lambda i: (layer, MIX_W // MEM_W, 0), pipeline_mode=once),
            pl.BlockSpec((tm, d), lambda i: (i, 0)),
            pl.BlockSpec((1, d), lambda i: (0, 0)),
        ],
        out_specs=[
            pl.BlockSpec((tm, d), lambda i: (i, 0)),
            pl.BlockSpec((d, tm), lambda i: (0, i)),
        ],
        out_shape=[jax.ShapeDtypeStruct((n, d), F32), jax.ShapeDtypeStruct((d, n), BF16)],
        compiler_params=_cparams(("parallel",)),
        name="out_proj",
    )(o_mix, o_mem, w_out, w_out, x, g2.reshape(1, d))


_REMOVED = 2.0 ** 100


def _extract_exact(vals, order, rounds):
    rank = jnp.full(vals.shape, float(PEER_TOPK), F32)
    tops = []
    for r in range(rounds):
        m = jnp.max(vals, axis=0, keepdims=True)
        pick = jnp.min(jnp.where(vals == m, order, 1e9), axis=0, keepdims=True)
        sel = order == pick
        rank = jnp.where(sel, float(r), rank)
        vals = jnp.where(sel, -jnp.inf, vals)
        tops.append(m)
    return jnp.concatenate(tops, axis=0), rank


def _extract_fast(vals, rounds):
    tops = []
    for r in range(rounds):
        m = jnp.max(vals, axis=0, keepdims=True)
        vals = jnp.where(vals == m, -_REMOVED * (r + 1), vals)
        tops.append(m)
    rank = jnp.where(vals <= -_REMOVED, vals * (-1.0 / _REMOVED) - 1.0, float(PEER_TOPK))
    return jnp.concatenate(tops, axis=0), rank


def _route_lanes(s1, s2, exact):
    lanes = s1.shape[1]
    if exact:
        key_id = lax.broadcasted_iota(jnp.int32, (N_KEYS, lanes), 0).astype(F32)
        top1, rank1 = _extract_exact(s1, key_id, PEER_TOPK)
        top2, rank2 = _extract_exact(s2, key_id, PEER_TOPK)
    else:
        top1, rank1 = _extract_fast(s1, PEER_TOPK)
        top2, rank2 = _extract_fast(s2, PEER_TOPK)

    blk_a = top1[0:1] + top2
    blk_b = [top1[a:a + 1] + top2[0:8] for a in range(1, 8)]
    blk_c = top1[8:16] + top2[0:1]
    cand = jnp.concatenate([blk_a] + blk_b + [blk_c], axis=0)
    nrow = cand.shape[0]
    r = lax.broadcasted_iota(jnp.int32, (nrow, lanes), 0)
    a_mid = 1 + jnp.right_shift(r - 16, 3)
    b_mid = jnp.bitwise_and(r - 16, 7)
    valid = jnp.logical_or(jnp.logical_or(r < 16, r >= 72), (a_mid + 1) * (b_mid + 1) <= PEER_TOPK)
    cand = jnp.where(valid, cand, -jnp.inf)
    if exact:
        cand_id = jnp.where(r < 16, r, jnp.where(r < 72, a_mid * PEER_TOPK + b_mid, (r - 64) * PEER_TOPK))
        best, crank = _extract_exact(cand, cand_id.astype(F32), PEER_TOPK)
    else:
        best, crank = _extract_fast(cand, PEER_TOPK)
    chosen = jnp.logical_and(valid, crank < PEER_TOPK).astype(F32)
    z = jnp.sum(jnp.exp(best - best[0:1]), axis=0, keepdims=True)

    counts = [jnp.sum(chosen[0:16], axis=0, keepdims=True)]
    counts += [jnp.sum(chosen[16 + 8 * (a - 1):16 + 8 * a], axis=0, keepdims=True) for a in range(1, 8)]
    counts += [chosen[72 + a - 8:72 + a - 7] for a in range(8, 16)]
    c1 = jnp.zeros((N_KEYS, lanes), F32)
    for a in range(PEER_TOPK):
        c1 = jnp.where(rank1 == a, counts[a], c1)

    ranked = (jnp.sum((rank1 < PEER_TOPK).astype(F32), axis=0, keepdims=True)
              + jnp.sum((rank2 < PEER_TOPK).astype(F32), axis=0, keepdims=True)
              + jnp.sum(chosen, axis=0, keepdims=True))
    low = jnp.minimum(jnp.min(s1, axis=0, keepdims=True), jnp.min(s2, axis=0, keepdims=True))
    redo = jnp.logical_or(ranked != 3.0 * PEER_TOPK, jnp.logical_not(low > -0.25 * _REMOVED)).astype(F32)
    e1 = jnp.exp(s1 - top1[0:1])
    e2 = jnp.exp(s2 - top2[0:1]) / z
    return rank2, e2, c1, e1, redo


def _scores_kernel(xnt_ref, wq_ref, keys_ref, s_ref):
    qq = jnp.dot(wq_ref[...], xnt_ref[...], preferred_element_type=F32).astype(BF16)
    for h in range(PEER_HEADS):
        for c in range(2):
            row = (2 * h + c) * N_KEYS
            s_ref[2 * h + c] = jnp.dot(keys_ref[c, h], qq[row:row + N_KEYS], preferred_element_type=F32)


def _peer_scores(xnt, w_pq_t, keys, layer):
    d, n = xnt.shape
    tb = min(TB_SCORES, n)
    return pl.pallas_call(
        _scores_kernel,
        grid=(n // tb,),
        in_specs=[
            pl.BlockSpec((d, tb), lambda i: (0, i)),
            pl.BlockSpec((None,) + w_pq_t.shape[1:], lambda i: (layer, 0, 0)),
            pl.BlockSpec((None,) + keys.shape[1:], lambda i: (layer, 0, 0, 0, 0)),
        ],
        out_specs=pl.BlockSpec((2 * PEER_HEADS, N_KEYS, tb), lambda i: (0, 0, i)),
        out_shape=jax.ShapeDtypeStruct((2 * PEER_HEADS, N_KEYS, n), F32),
        compiler_params=_cparams(("parallel",)),
        name="peer_scores",
    )(xnt, w_pq_t, keys)


def _route_kernel(s_ref, r2_ref, e2_ref, c1_ref, e1_ref, *, hr):
    def emit(h, exact):
        rank2, e2, c1, e1, redo = _route_lanes(s_ref[2 * h], s_ref[2 * h + 1], exact)
        r2_ref[h] = rank2.astype(BF16)
        e2_ref[h] = e2.astype(BF16)
        c1_ref[h] = c1
        e1_ref[h] = e1
        return redo

    redo = [emit(h, exact=False) for h in range(hr)]
    for h in range(hr):
        @pl.when(jnp.max(redo[h]) > 0.0)
        def _():
            emit(h, exact=True)


def _peer_route(scores, hr=HR_ROUTE):
    n = scores.shape[2]
    tb = min(TB_ROUTE, n)
    out = jax.ShapeDtypeStruct((PEER_HEADS, N_KEYS, n), F32)
    out16 = jax.ShapeDtypeStruct((PEER_HEADS, N_KEYS, n), BF16)
    ospec = pl.BlockSpec((hr, N_KEYS, tb), lambda i, h: (h, 0, i))
    return pl.pallas_call(
        functools.partial(_route_kernel, hr=hr),
        grid=(n // tb, PEER_HEADS // hr),
        in_specs=[pl.BlockSpec((2 * hr, N_KEYS, tb), lambda i, h: (h, 0, i))],
        out_specs=[ospec, ospec, ospec, ospec],
        out_shape=[out16, out16, out, out],
        compiler_params=_cparams(("parallel", "parallel")),
        name="peer_route",
    )(scores)


def _gelu(x):
    return 0.5 * x * (1.0 + lax.erf(x * (0.5 ** 0.5)))


def _u_chunk_copy(u_hbm, ubuf_ref, usem_ref, c, *, layer, ck):
    c = jnp.asarray(c, jnp.int32)
    slot = c % ubuf_ref.shape[0]
    row = pl.multiple_of((c % (N_EXPERTS // ck)) * ck, ck)
    return pltpu.make_async_copy(u_hbm.at[layer, pl.ds(row, ck), :], ubuf_ref.at[slot], usem_ref.at[slot])


def _peer_kernel(xnt_ref, u_hbm, vt_ref, r2_ref, e2_ref, c1_ref, e1_ref, x1_ref, g_ref, o_ref,
                 y_ref, a_ref, coef_ref, ubuf_ref, usem_ref, *, layer, eb, ck, lc, final):
    e = pl.program_id(1)
    tb = xnt_ref.shape[1]
    per = ck // N_KEYS
    nck = eb // ck
    nslot = ubuf_ref.shape[0]
    step = pl.program_id(0) * pl.num_programs(1) + e
    base = step * nck
    last_c = pl.num_programs(0) * pl.num_programs(1) * nck
    copy = functools.partial(_u_chunk_copy, u_hbm, ubuf_ref, usem_ref, layer=layer, ck=ck)

    def start_if_in_stream(c):
        @pl.when(c <= last_c)
        def _():
            copy(c).start()

    @pl.when(step == 0)
    def _():
        for c in range(2 * nck + 1):
            copy(c).start()
        copy(0).wait()

    for j in range(nck - 1):
        start_if_in_stream(base + 2 * nck + 1 + j)
    for k in range(nck):
        copy(base + k + 1).wait()

    @pl.when(e == 0)
    def _():
        y_ref[...] = jnp.zeros_like(y_ref)
        a_ref[0] = jnp.dot(ubuf_ref[base % nslot], xnt_ref[...], preferred_element_type=F32)

    start_if_in_stream(base + 3 * nck)

    for k in range(nck):
        a_next = jnp.dot(ubuf_ref[(base + k + 1) % nslot], xnt_ref[...], preferred_element_type=F32)
        if k + 1 < nck:
            a_ref[k + 1] = a_next
        bits = pltpu.bitcast(a_next[0:1, tb - lc:], jnp.uint32)
        bits = lax.shift_right_logical(lax.shift_right_logical(bits, jnp.uint32(16)), jnp.uint32(16))
        zero = pltpu.bitcast(bits, F32)
        for ii in range(per):
            i = k * per + ii
            rows = slice(ii * N_KEYS, (ii + 1) * N_KEYS)
            for l0 in range(0, tb, lc):
                cols = slice(l0, l0 + lc)
                gate = jnp.zeros((N_KEYS, lc), BF16)
                for h in range(PEER_HEADS):
                    c1 = jnp.broadcast_to(c1_ref[h, i:i + 1, cols], (N_KEYS, lc)).astype(BF16)
                    e1 = jnp.broadcast_to(e1_ref[h, i:i + 1, cols], (N_KEYS, lc)).astype(BF16)
                    zero16 = jnp.zeros((N_KEYS, lc), BF16)
                    gate = gate + jnp.where(r2_ref[h, :, cols] < c1, e2_ref[h, :, cols], zero16) * e1
                act = gate.astype(F32) * _gelu(a_ref[k, rows, cols])
                if ii == per - 1 and l0 + lc == tb:
                    act = act + zero
                coef_ref[k, rows, cols] = act.astype(BF16)
        if k + 1 == nck:
            a_ref[0] = a_next
        y_ref[...] += jnp.dot(vt_ref[:, k * ck:(k + 1) * ck], coef_ref[k], preferred_element_type=F32)

    @pl.when(e == pl.num_programs(1) - 1)
    def _():
        x = x1_ref[...] + y_ref[...].T
        if final:
            x = _rms(x, g_ref[...])
        o_ref[...] = x


def _peer_sweep(xnt, u, vt, layer, r2, e2, c1, e1, x1, g, final):
    d, n = xnt.shape
    tb = min(TB_PEER, n)
    eb = EB_PEER
    ck = CK_PEER
    lc = min(LC_PEER, tb)
    rspec = pl.BlockSpec((PEER_HEADS, N_KEYS, tb), lambda t, e: (0, 0, t))
    ispec = pl.BlockSpec((PEER_HEADS, eb // N_KEYS, tb), lambda t, e: (0, e, t))
    nck = eb // ck
    assert N_EXPERTS % eb == 0 and eb % ck == 0 and (n // tb) * (N_EXPERTS // eb) >= 3
    return pl.pallas_call(
        functools.partial(_peer_kernel, layer=layer, eb=eb, ck=ck, lc=lc, final=final),
        grid=(n // tb, N_EXPERTS // eb),
        in_specs=[
            pl.BlockSpec((d, tb), lambda t, e: (0, t)),
            pl.BlockSpec(memory_space=pl.ANY),
            pl.BlockSpec((None, d, eb), lambda t, e: (layer, 0, e)),
            rspec, rspec, ispec, ispec,
            pl.BlockSpec((tb, d), lambda t, e: (t, 0)),
            pl.BlockSpec((1, d), lambda t, e: (0, 0)),
        ],
        out_specs=pl.BlockSpec((tb, d), lambda t, e: (t, 0)),
        out_shape=jax.ShapeDtypeStruct((n, d), F32),
        scratch_shapes=[pltpu.VMEM((d, tb), F32),
                        pltpu.VMEM((nck, ck, tb), F32), pltpu.VMEM((nck, ck, tb), BF16),
                        pltpu.VMEM((3 * nck, ck, d), BF16), pltpu.SemaphoreType.DMA((3 * nck,))],
        compiler_params=_cparams(("arbitrary", "arbitrary")),
        name="peer_sweep",
    )(xnt, u, vt, r2, e2, c1, e1, x1, g.reshape(1, d))


def _trunk(x, mem, k_cache, v_cache, s_cache, mk_cache, mv_cache, p):
    prompt = mem is not None
    bsz, t, d = x.shape
    n = bsz * t
    depth = p["g_norm1"].shape[0]
    x2 = x.reshape(n, d)
    new_k, new_v, new_s, new_mk, new_mv = [], [], [], [], []
    for i in range(depth):
        j = i // 2
        attn = i % 2 == 0
        w_in = p["w_in_a"] if attn else p["w_in_b"]
        w_out = p["w_out_a"] if attn else p["w_out_b"]
        z = _norm_matmul(x2, p["g_norm1"][i], w_in, j)
        z3 = z.reshape(bsz, t, z.shape[1])
        mq_blk = (z.shape[1] - MEM_W) // MEM_W

        if prompt:
            mkv = _norm_matmul(mem.reshape(bsz * N_MEM, d), p["g_mem"][i], p["w_mem_kv"], i)
            new_mk.append(mkv[:, :MEM_W].reshape(bsz, N_MEM, N_MEM_HEADS, HEAD_DIM))
            new_mv.append(mkv[:, MEM_W:].reshape(bsz, N_MEM, N_MEM_HEADS, HEAD_DIM))
            mkv3 = mkv.reshape(bsz, N_MEM, 2 * MEM_W)
            kspec = pl.BlockSpec((None, N_MEM, MEM_W), lambda b, q: (b, 0, 0))
            vspec = pl.BlockSpec((None, N_MEM, MEM_W), lambda b, q: (b, 0, 1))
            o_mem = _mem_attention(z3, mq_blk, mkv3, kspec, mkv3, vspec, head_major=False)
        else:
            cspec = pl.BlockSpec((None, None, N_MEM_HEADS, N_MEM, HEAD_DIM), lambda b, q, i=i: (i, b, 0, 0, 0))
            o_mem = _mem_attention(z3, mq_blk, jnp.transpose(mk_cache, (0, 1, 3, 2, 4)), cspec,
                                   jnp.transpose(mv_cache, (0, 1, 3, 2, 4)), cspec, head_major=True)

        if attn:
            k_new = z3[:, :, MIX_W:2 * MIX_W]
            v_new = z3[:, :, 2 * MIX_W:3 * MIX_W]
            if prompt:
                bias = _band_bias(p["rel_bias"][j], CHUNK, WINDOW + CHUNK, WINDOW)
                o_mix = _band_attention_prompt(z3, bias)
                keep = min(WINDOW, t)
                k_new, v_new = k_new[:, t - keep:], v_new[:, t - keep:]
            else:
                p_len = k_cache.shape[2]
                bias = _band_bias(p["rel_bias"][j], t, p_len + t, p_len)
                o_mix = _band_attention_sample(z3, k_cache, v_cache, j, bias)
            new_k.append(k_new.reshape(bsz, -1, N_MIX_HEADS, HEAD_DIM))
            new_v.append(v_new.reshape(bsz, -1, N_MIX_HEADS, HEAD_DIM))
        else:
            o_mix, s_new = _hgrn(z3, p["lb_param"], p["g_hgrn"][j], None if prompt else s_cache, j, i)
            new_s.append(s_new)

        x1, xnt = _out_proj(o_mix.reshape(n, MIX_W), o_mem.reshape(n, MEM_W), w_out, j, x2, p["g_norm2"][i])
        r2, e2, c1, e1 = _peer_route(_peer_scores(xnt, p["w_pq_t"], p["peer_keys"], i))
        x2 = _peer_sweep(xnt, p["peer_u"], p["peer_v_t"], i, r2, e2, c1, e1, x1, p["g_final"],
                         final=(i == depth - 1))

    y = x2.reshape(bsz, t, d)
    mk_out = jnp.stack(new_mk) if new_mk else None
    mv_out = jnp.stack(new_mv) if new_mv else None
    return y, jnp.stack(new_k), jnp.stack(new_v), jnp.stack(new_s), mk_out, mv_out


def kernel(x_prompt, x_sample, cache_attn_k, cache_attn_v, state_hgrn, cache_mem_k, cache_mem_v, mem_prompt,
           g_norm1, g_norm2, g_mem, g_final, w_in_a, w_out_a, rel_bias, w_in_b, w_out_b, lb_param, g_hgrn,
           w_mem_kv, w_pq, peer_keys, peer_u, peer_v):
    p = dict(
        g_norm1=g_norm1, g_norm2=g_norm2, g_mem=g_mem, g_final=g_final, rel_bias=rel_bias,
        lb_param=lb_param, g_hgrn=g_hgrn,
        w_in_a=w_in_a.astype(BF16), w_out_a=w_out_a.astype(BF16),
        w_in_b=w_in_b.astype(BF16), w_out_b=w_out_b.astype(BF16),
        w_mem_kv=w_mem_kv.astype(BF16),
        w_pq_t=jnp.swapaxes(w_pq, 1, 2).astype(BF16),
        peer_keys=peer_keys.astype(BF16),
        peer_u=peer_u.astype(BF16),
        peer_v_t=jnp.swapaxes(peer_v, 1, 2).astype(BF16),
    )
    y_p, k_p, v_p, s_p, mk_p, mv_p = _trunk(x_prompt, mem_prompt, None, None, None, None, None, p)
    y_s, k_s, v_s, s_s, _, _ = _trunk(x_sample, None, cache_attn_k, cache_attn_v, state_hgrn,
                                      cache_mem_k, cache_mem_v, p)
    return (y_p, y_s, k_p, v_p, k_s, v_s, s_p, s_s, mk_p, mv_p)
```
